```python
import jax, jax.numpy as jnp
from jax import lax
import numpy as np

D_MODEL = 1024
BATCH = 8
SEQ = 2048
DEPTH = 4

HEAD_DIM = 64
SB_HEADS = 6
RET_HEADS = 6
MEM_HEADS = 4
MEM_TOKENS = 256
SB_WIDTH = SB_HEADS * HEAD_DIM
RET_WIDTH = RET_HEADS * HEAD_DIM
MEM_WIDTH = MEM_HEADS * HEAD_DIM
MIX_WIDTH = SB_WIDTH + RET_WIDTH + MEM_WIDTH
IN_WIDTH = 4 * SB_WIDTH + 4 * RET_WIDTH + 2 * MEM_WIDTH
Q_BLOCK = 128
RET_CHUNK = 128
ROPE_BASE = 10000.0
EPS = 1e-6

kernel_name = "hybrid_stickbreaking_retention_memory"


def rms_norm(x, g):
    xf = x.astype(jnp.float32)
    y = xf * lax.rsqrt(jnp.mean(xf * xf, axis=-1, keepdims=True) + EPS)
    return (y * g.astype(jnp.float32)).astype(x.dtype)


def split_heads(t, n_heads):
    b, s, _ = t.shape
    return t.reshape(b, s, n_heads, HEAD_DIM).transpose(0, 2, 1, 3)


def merge_heads(t):
    b, h, s, d = t.shape
    return t.transpose(0, 2, 1, 3).reshape(b, s, h * d)


def rotary(t, positions):
    half = t.shape[-1] // 2
    inv_freq = ROPE_BASE ** (-jnp.arange(half, dtype=jnp.float32) / half)
    ang = positions.astype(jnp.float32)[:, None, :, None] * inv_freq
    cos, sin = jnp.cos(ang), jnp.sin(ang)
    tf = t.astype(jnp.float32)
    t1, t2 = tf[..., :half], tf[..., half:]
    return jnp.concatenate([t1 * cos - t2 * sin, t1 * sin + t2 * cos], axis=-1).astype(t.dtype)


def stick_breaking_attention(q, k, v):
    b, h, s, d = q.shape
    nb = s // Q_BLOCK
    qb = q.reshape(b, h, nb, Q_BLOCK, d).transpose(2, 0, 1, 3, 4)
    kf = k.astype(jnp.float32)
    vf = v.astype(jnp.float32)
    scale = d ** -0.5
    key_pos = jnp.arange(s)

    def block(args):
        qblk, b_idx = args
        z = jnp.einsum('bhqd,bhkd->bhqk', qblk.astype(jnp.float32), kf) * scale
        q_pos = b_idx * Q_BLOCK + jnp.arange(Q_BLOCK)
        causal = key_pos[None, :] < q_pos[:, None]
        log_beta = jax.nn.log_sigmoid(z)
        log_keep = jnp.where(causal, jax.nn.log_sigmoid(-z), 0.0)
        later = lax.cumsum(log_keep, axis=3, reverse=True) - log_keep
        w = jnp.where(causal, jnp.exp(log_beta + later), 0.0)
        return jnp.einsum('bhqk,bhkd->bhqd', w, vf)

    out = lax.map(block, (qb, jnp.arange(nb)))
    return out.transpose(1, 2, 0, 3, 4).reshape(b, h, s, d).astype(q.dtype)


def retention_chunkwise(q, k, v):
    b, h, s, d = q.shape
    c = RET_CHUNK
    nc = s // c
    log_gamma = jnp.log(1.0 - 2.0 ** (-5.0 - jnp.arange(h, dtype=jnp.float32)))
    kf = k.astype(jnp.float32) * d ** -0.5

    def chunks(t):
        return t.astype(jnp.float32).reshape(b, h, nc, c, d).transpose(2, 0, 1, 3, 4)

    qc, kc, vc = chunks(q), chunks(kf), chunks(v)
    idx = jnp.arange(c, dtype=jnp.float32)
    diff = idx[:, None] - idx[None, :]
    lg = log_gamma[:, None, None]
    decay_intra = jnp.where(diff >= 0, jnp.exp(lg * jnp.maximum(diff, 0.0)), 0.0)
    q_decay = jnp.exp(log_gamma[:, None] * (idx + 1.0))
    k_decay = jnp.exp(log_gamma[:, None] * (c - 1.0 - idx))
    chunk_decay = jnp.exp(log_gamma * c)

    def step(state, inp):
        qi, ki, vi = inp
        scores = jnp.einsum('bhid,bhjd->bhij', qi, ki) * decay_intra
        intra = jnp.einsum('bhij,bhjd->bhid', scores, vi)
        cross = jnp.einsum('bhid,bhde->bhie', qi, state) * q_decay[None, :, :, None]
        new_state = state * chunk_decay[None, :, None, None] + jnp.einsum(
            'bhjd,bhje->bhde', ki * k_decay[None, :, :, None], vi)
        return new_state, intra + cross

    state0 = jnp.zeros((b, h, d, d), jnp.float32)
    _, out = lax.scan(step, state0, (qc, kc, vc))
    return out.transpose(1, 2, 0, 3, 4).reshape(b, h, s, d)


def memory_cross_attention(q, mk, mv, g_q, g_k):
    qn = rms_norm(q, g_q).astype(jnp.float32)
    kn = rms_norm(mk, g_k).astype(jnp.float32)
    scores = jnp.einsum('bhsd,bhmd->bhsm', qn, kn) * (q.shape[-1] ** -0.5)
    p = jax.nn.softmax(scores, axis=-1)
    return jnp.einsum('bhsm,bhmd->bhsd', p, mv.astype(jnp.float32)).astype(q.dtype)


def setup_inputs(seed: int = 0) -> dict:
    key = jax.random.key(seed)
    ks = jax.random.split(key, 12)
    f32 = jnp.float32
    x = jax.random.normal(ks[0], (BATCH, SEQ, D_MODEL), f32)
    mem = jax.random.normal(ks[1], (BATCH, MEM_TOKENS, D_MODEL), f32)
    offsets = jax.random.randint(ks[2], (BATCH, 1), 0, 1024, dtype=jnp.int32)
    positions = (jnp.arange(SEQ, dtype=jnp.int32)[None, :] + offsets).astype(jnp.int32)
    norm_g = 1.0 + 0.02 * jax.random.normal(ks[3], (DEPTH, D_MODEL), f32)
    w_in = jax.random.normal(ks[4], (DEPTH, D_MODEL, IN_WIDTH), f32) * D_MODEL ** -0.5
    w_out = jax.random.normal(ks[5], (DEPTH, MIX_WIDTH, D_MODEL), f32) * MIX_WIDTH ** -0.5
    mem_norm_g = 1.0 + 0.02 * jax.random.normal(ks[6], (DEPTH, D_MODEL), f32)
    w_mem_kv = jax.random.normal(ks[7], (DEPTH, D_MODEL, 2 * MEM_WIDTH), f32) * D_MODEL ** -0.5
    mem_q_norm_g = 1.0 + 0.02 * jax.random.normal(ks[8], (DEPTH, HEAD_DIM), f32)
    mem_k_norm_g = 1.0 + 0.02 * jax.random.normal(ks[9], (DEPTH, HEAD_DIM), f32)
    ret_norm_g = 1.0 + 0.02 * jax.random.normal(ks[10], (DEPTH, RET_WIDTH), f32)
    return {"x": x, "mem": mem, "positions": positions, "norm_g": norm_g, "w_in": w_in,
            "w_out": w_out, "mem_norm_g": mem_norm_g, "w_mem_kv": w_mem_kv,
            "mem_q_norm_g": mem_q_norm_g, "mem_k_norm_g": mem_k_norm_g, "ret_norm_g": ret_norm_g}


def reference(x, mem, positions, norm_g, w_in, w_out, mem_norm_g, w_mem_kv,
              mem_q_norm_g, mem_k_norm_g, ret_norm_g):
    widths = [SB_WIDTH] * 4 + [RET_WIDTH] * 4 + [MEM_WIDTH] * 2
    split_at = [int(v) for v in np.cumsum(widths)[:-1]]
    b, s, _ = x.shape
    for l in range(DEPTH):
        h = rms_norm(x, norm_g[l])
        proj = jnp.einsum('bsd,de->bse', h, w_in[l])
        (sb_q, sb_k, sb_v, sb_g, ret_q, ret_k, ret_v, ret_g,
         mem_q, mem_g) = jnp.split(proj, split_at, axis=-1)

        sb_o = stick_breaking_attention(split_heads(sb_q, SB_HEADS), split_heads(sb_k, SB_HEADS),
                                        split_heads(sb_v, SB_HEADS))
        sb_o = merge_heads(sb_o) * jax.nn.silu(sb_g)

        rq = rotary(split_heads(ret_q, RET_HEADS), positions)
        rk = rotary(split_heads(ret_k, RET_HEADS), positions)
        ret_o = retention_chunkwise(rq, rk, split_heads(ret_v, RET_HEADS))
        ret_o = ret_o * lax.rsqrt(jnp.mean(ret_o * ret_o, axis=-1, keepdims=True) + EPS)
        ret_o = (merge_heads(ret_o) * ret_norm_g[l].astype(jnp.float32)).astype(x.dtype)
        ret_o = ret_o * jax.nn.silu(ret_g)

        mh = rms_norm(mem, mem_norm_g[l])
        mkv = jnp.einsum('bmd,de->bme', mh, w_mem_kv[l])
        mk, mv = jnp.split(mkv, 2, axis=-1)
        mem_o = memory_cross_attention(split_heads(mem_q, MEM_HEADS), split_heads(mk, MEM_HEADS),
                                       split_heads(mv, MEM_HEADS), mem_q_norm_g[l], mem_k_norm_g[l])
        mem_o = merge_heads(mem_o) * jax.nn.silu(mem_g)

        mixed = jnp.concatenate([sb_o, ret_o, mem_o], axis=-1)
        x = x + jnp.einsum('bse,ed->bsd', mixed, w_out[l])
    return x
```

```python
import functools

import jax
import jax.numpy as jnp
from jax import lax
from jax.experimental import pallas as pl
from jax.experimental.pallas import tpu as pltpu

D_MODEL = 1024
HEAD_DIM = 64
SB_HEADS = 6
RET_HEADS = 6
MEM_HEADS = 4
SB_WIDTH = SB_HEADS * HEAD_DIM
RET_WIDTH = RET_HEADS * HEAD_DIM
MEM_WIDTH = MEM_HEADS * HEAD_DIM
MIX_WIDTH = SB_WIDTH + RET_WIDTH + MEM_WIDTH
IN_WIDTH = 4 * SB_WIDTH + 4 * RET_WIDTH + 2 * MEM_WIDTH
ROPE_BASE = 10000.0
EPS = 1e-6
QK_SCALE = HEAD_DIM ** -0.5

LANES = 128
PAIR = LANES // HEAD_DIM
SB_PAIRS = SB_HEADS // PAIR
RET_PAIRS = RET_HEADS // PAIR
MEM_PAIRS = MEM_HEADS // PAIR

COL_SB_Q = 0
COL_SB_K = COL_SB_Q + SB_PAIRS
COL_SB_V = COL_SB_K + SB_PAIRS
COL_SB_G = COL_SB_V + SB_PAIRS
COL_RET_Q = COL_SB_G + SB_PAIRS
COL_RET_K = COL_RET_Q + RET_PAIRS
COL_RET_V = COL_RET_K + RET_PAIRS
COL_RET_G = COL_RET_V + RET_PAIRS
COL_MEM_Q = COL_RET_G + RET_PAIRS
COL_MEM_G = COL_MEM_Q + MEM_PAIRS

PROJ_ROWS = 512
PROJ_COLS = 512
SB_BLOCK = 256
RET_CHUNK = 256
MEM_ROWS = 512
VMEM_LIMIT = 48 * 1024 * 1024

F32 = jnp.float32
BF16 = jnp.bfloat16


def _dot(a, b):
    return jnp.dot(a, b, preferred_element_type=F32)


def _dot_nt(a, b):
    return lax.dot_general(a, b, (((1,), (1,)), ((), ())), preferred_element_type=F32)


def _dot_tn(a, b):
    return lax.dot_general(a, b, (((0,), (0,)), ((), ())), preferred_element_type=F32)


def _silu(g):
    return g / (1.0 + jnp.exp(-g))


def _head_masks():
    lane = lax.broadcasted_iota(jnp.int32, (1, LANES), 1)
    first = lane < HEAD_DIM
    return first, jnp.logical_not(first)


def _pair_mean_square(t, first):
    sq = t * t
    s_a = jnp.sum(jnp.where(first, sq, 0.0), axis=-1, keepdims=True)
    s_b = jnp.sum(jnp.where(first, 0.0, sq), axis=-1, keepdims=True)
    return jnp.where(first, s_a, s_b) * (1.0 / HEAD_DIM)


def _rope_kernel(pos_ref, invf_ref, cos_ref, sin_ref):
    ang = pos_ref[...].astype(F32) * invf_ref[...]
    lane = lax.broadcasted_iota(jnp.int32, (1, LANES), 1)
    sign = jnp.where((lane % HEAD_DIM) < HEAD_DIM // 2, -1.0, 1.0)
    cos_ref[...] = jnp.cos(ang)
    sin_ref[...] = jnp.sin(ang) * sign


def _rope_tables(positions):
    b, s = positions.shape
    half = HEAD_DIM // 2
    inv_freq = ROPE_BASE ** (-jnp.arange(half, dtype=F32) / half)
    invf = jnp.tile(inv_freq, LANES // half)[None, :]
    pos = positions.reshape(b * s, 1)
    rows = s
    return pl.pallas_call(
        _rope_kernel,
        grid=(b * s // rows,),
        in_specs=[pl.BlockSpec((rows, 1), lambda i: (i, 0)),
                  pl.BlockSpec((1, LANES), lambda i: (0, 0))],
        out_specs=[pl.BlockSpec((rows, LANES), lambda i: (i, 0)),
                   pl.BlockSpec((rows, LANES), lambda i: (i, 0))],
        out_shape=[jax.ShapeDtypeStruct((b * s, LANES), F32)] * 2,
        name="rope_tables",
    )(pos, invf)


def _in_proj_kernel(x_ref, g_ref, w_ref, o_ref):
    x = x_ref[...]
    ms = jnp.mean(x * x, axis=-1, keepdims=True)
    h = (x * lax.rsqrt(ms + EPS) * g_ref[...]).astype(BF16)
    for j in range(IN_WIDTH // PROJ_COLS):
        cols = slice(j * PROJ_COLS, (j + 1) * PROJ_COLS)
        o_ref[:, cols] = _dot(h, w_ref[:, cols]).astype(BF16)


def _in_proj(x2, g, w):
    m = x2.shape[0]
    return pl.pallas_call(
        _in_proj_kernel,
        grid=(m // PROJ_ROWS,),
        in_specs=[pl.BlockSpec((PROJ_ROWS, D_MODEL), lambda i: (i, 0)),
                  pl.BlockSpec((1, D_MODEL), lambda i: (0, 0)),
                  pl.BlockSpec((D_MODEL, IN_WIDTH), lambda i: (0, 0))],
        out_specs=pl.BlockSpec((PROJ_ROWS, IN_WIDTH), lambda i: (i, 0)),
        out_shape=jax.ShapeDtypeStruct((m, IN_WIDTH), BF16),
        compiler_params=pltpu.CompilerParams(vmem_limit_bytes=VMEM_LIMIT),
        name="in_proj",
    )(x2, g, w)


def _softplus(z):
    return jnp.maximum(z, 0.0) + jnp.log1p(jnp.exp(-jnp.abs(z)))


def _sb_kernel(q_ref, k_ref, v_ref, g_ref, o_ref):
    t = SB_BLOCK
    i = pl.program_id(2)
    first, second = _head_masks()
    q = q_ref[...] * QK_SCALE
    q_heads = (jnp.where(first, q, 0), jnp.where(second, q, 0))
    row = lax.broadcasted_iota(jnp.int32, (t, t), 0)
    col = lax.broadcasted_iota(jnp.int32, (t, t), 1)
    tri = (row >= col).astype(BF16)
    causal = col < row

    def block(kb, carry, diag):
        acc, r_a, r_b = carry
        start = pl.multiple_of(kb * t, t)
        k = k_ref[pl.ds(start, t), :]
        v = v_ref[pl.ds(start, t), :]
        new_r = []
        for qh, r, mask in ((q_heads[0], r_a, first), (q_heads[1], r_b, second)):
            z = _dot_nt(qh, k)
            sp = _softplus(z)
            if diag:
                sp = jnp.where(causal, sp, 0.0)
            hi = sp.astype(BF16)
            lo = (sp - hi.astype(F32)).astype(BF16)
            cum = _dot(hi, tri) + _dot(lo, tri)
            w = jnp.exp(z - cum - r)
            if diag:
                w = jnp.where(causal, w, 0.0)
            acc = acc + _dot(w.astype(BF16), jnp.where(mask, v, 0))
            new_r.append(r + cum[:, 0:1])
        return acc, new_r[0], new_r[1]

    zero_r = jnp.zeros((t, 1), F32)
    carry = block(i, (jnp.zeros((t, LANES), F32), zero_r, zero_r), True)
    carry = lax.fori_loop(0, i, lambda j, c: block(i - 1 - j, c, False), carry)
    o_ref[...] = (carry[0] * _silu(g_ref[...].astype(F32))).astype(BF16)


def _sb_attention(proj, b, s):
    nq = s // SB_BLOCK
    return pl.pallas_call(
        _sb_kernel,
        grid=(b, SB_PAIRS, nq),
        in_specs=[pl.BlockSpec((SB_BLOCK, LANES), lambda bi, p, i: (bi * nq + i, COL_SB_Q + p)),
                  pl.BlockSpec((s, LANES), lambda bi, p, i: (bi, COL_SB_K + p)),
                  pl.BlockSpec((s, LANES), lambda bi, p, i: (bi, COL_SB_V + p)),
                  pl.BlockSpec((SB_BLOCK, LANES), lambda bi, p, i: (bi * nq + i, COL_SB_G + p))],
        out_specs=pl.BlockSpec((SB_BLOCK, LANES), lambda bi, p, i: (bi * nq + i, p)),
        out_shape=jax.ShapeDtypeStruct((b * s, SB_WIDTH), BF16),
        compiler_params=pltpu.CompilerParams(vmem_limit_bytes=VMEM_LIMIT),
        name="sb_attention",
    )(proj, proj, proj, proj)


def _swap_halves(t):
    lane = lax.broadcasted_iota(jnp.int32, (1, LANES), 1)
    half = HEAD_DIM // 2
    lower = (lane % HEAD_DIM) < half
    return jnp.where(lower, pltpu.roll(t, LANES - half, 1), pltpu.roll(t, half, 1))


def _ret_kernel(q_ref, k_ref, v_ref, g_ref, cos_ref, sin_ref, ng_ref, lg_ref, o_ref, state_ref):
    c = RET_CHUNK
    first, second = _head_masks()
    nchunks = q_ref.shape[0] // c

    lg_lane = lg_ref[0]
    lg_a = lg_lane[:, 0:1]
    lg_b = lg_lane[:, HEAD_DIM:HEAD_DIM + 1]

    ri = lax.broadcasted_iota(jnp.int32, (c, c), 0)
    ci = lax.broadcasted_iota(jnp.int32, (c, c), 1)
    diff = (ri - ci).astype(F32)
    keep = ri >= ci
    decay_a = jnp.where(keep, jnp.exp(lg_a * jnp.maximum(diff, 0.0)), 0.0)
    decay_b = jnp.where(keep, jnp.exp(lg_b * jnp.maximum(diff, 0.0)), 0.0)
    idx = lax.broadcasted_iota(jnp.int32, (c, LANES), 0).astype(F32)
    q_decay = jnp.exp(lg_lane * (idx + 1.0))
    k_decay = jnp.exp(lg_lane * (c - 1.0 - idx))
    si = lax.broadcasted_iota(jnp.int32, (LANES, LANES), 0)
    sj = lax.broadcasted_iota(jnp.int32, (LANES, LANES), 1)
    same_head = (si < HEAD_DIM) == (sj < HEAD_DIM)
    state_decay = jnp.where(same_head, jnp.exp(lg_lane * float(c)), 0.0)
    norm_g = ng_ref[...]

    state_ref[...] = jnp.zeros_like(state_ref)

    def chunk(n, _):
        rows = pl.ds(pl.multiple_of(n * c, c), c)
        cos = cos_ref[rows, :]
        sin = sin_ref[rows, :]
        qf = q_ref[rows, :].astype(F32)
        kf = k_ref[rows, :].astype(F32)
        qr = qf * cos + _swap_halves(qf) * sin
        kr = (kf * cos + _swap_halves(kf) * sin) * QK_SCALE
        qb = qr.astype(BF16)
        kb = kr.astype(BF16)
        v = v_ref[rows, :]
        state = state_ref[...]

        out = _dot(qb, state.astype(BF16)) * q_decay
        for mask, decay in ((first, decay_a), (second, decay_b)):
            scores = _dot_nt(jnp.where(mask, qb, 0), kb) * decay
            out = out + _dot(scores.astype(BF16), jnp.where(mask, v, 0))
        kd = (kr * k_decay).astype(BF16)
        state_ref[...] = state * state_decay + jnp.where(same_head, _dot_tn(kd, v), 0.0)

        inv = lax.rsqrt(_pair_mean_square(out, first) + EPS)
        y = out * inv * norm_g
        o_ref[rows, :] = (y * _silu(g_ref[rows, :].astype(F32))).astype(BF16)
        return 0

    lax.fori_loop(0, nchunks, chunk, 0)


def _retention(proj, cos, sin, ret_norm_g, b, s):
    seq = lambda col: pl.BlockSpec((s, LANES), lambda bi, p: (bi, col + p))
    tab = pl.BlockSpec((s, LANES), lambda bi, p: (bi, 0))
    gamma = 1.0 - 2.0 ** (-5.0 - jnp.arange(RET_HEADS, dtype=F32))
    log_gamma = jnp.repeat(jnp.log(gamma), HEAD_DIM).reshape(RET_PAIRS, 1, LANES)
    return pl.pallas_call(
        _ret_kernel,
        grid=(b, RET_PAIRS),
        in_specs=[seq(COL_RET_Q), seq(COL_RET_K), seq(COL_RET_V), seq(COL_RET_G), tab, tab,
                  pl.BlockSpec((1, LANES), lambda bi, p: (0, p)),
                  pl.BlockSpec((1, 1, LANES), lambda bi, p: (p, 0, 0))],
        out_specs=pl.BlockSpec((s, LANES), lambda bi, p: (bi, p)),
        out_shape=jax.ShapeDtypeStruct((b * s, RET_WIDTH), BF16),
        scratch_shapes=[pltpu.VMEM((LANES, LANES), F32)],
        compiler_params=pltpu.CompilerParams(vmem_limit_bytes=VMEM_LIMIT),
        name="retention",
    )(proj, proj, proj, proj, cos, sin, ret_norm_g, log_gamma)


def _mem_kv_kernel(mem_ref, g_ref, w_ref, kg_ref, k_ref, v_ref):
    first, _ = _head_masks()
    x = mem_ref[...]
    ms = jnp.mean(x * x, axis=-1, keepdims=True)
    h = (x * lax.rsqrt(ms + EPS) * g_ref[0]).astype(BF16)
    kv = _dot(h, w_ref[0])
    kg = kg_ref[0]
    for pair in range(MEM_PAIRS):
        cols = slice(pair * LANES, (pair + 1) * LANES)
        kp = kv[:, cols]
        kn = kp * lax.rsqrt(_pair_mean_square(kp, first) + EPS) * kg
        k_ref[0, :, cols] = kn.astype(BF16)
    v_ref[0] = kv[:, MEM_WIDTH:].astype(BF16)


def _mem_kv(mem2, mem_norm_g, w_mem_kv, k_norm_g, b, tokens):
    depth = w_mem_kv.shape[0]
    out = jax.ShapeDtypeStruct((depth, b * tokens, MEM_WIDTH), BF16)
    return pl.pallas_call(
        _mem_kv_kernel,
        grid=(depth, b),
        in_specs=[pl.BlockSpec((tokens, D_MODEL), lambda l, bi: (bi, 0)),
                  pl.BlockSpec((1, 1, D_MODEL), lambda l, bi: (l, 0, 0)),
                  pl.BlockSpec((1, D_MODEL, 2 * MEM_WIDTH), lambda l, bi: (l, 0, 0)),
                  pl.BlockSpec((1, 1, LANES), lambda l, bi: (l, 0, 0))],
        out_specs=[pl.BlockSpec((1, tokens, MEM_WIDTH), lambda l, bi: (l, bi, 0)),
                   pl.BlockSpec((1, tokens, MEM_WIDTH), lambda l, bi: (l, bi, 0))],
        out_shape=[out, out],
        compiler_params=pltpu.CompilerParams(vmem_limit_bytes=VMEM_LIMIT),
        name="mem_kv",
    )(mem2, mem_norm_g, w_mem_kv, k_norm_g)


def _mem_attn_kernel(q_ref, g_ref, k_ref, v_ref, qg_ref, o_ref):
    first, second = _head_masks()
    q = q_ref[...].astype(F32)
    qn = (q * lax.rsqrt(_pair_mean_square(q, first) + EPS) * qg_ref[...]).astype(BF16)
    k = k_ref[...]
    v = v_ref[...]
    out = jnp.zeros(q.shape, F32)
    for mask in (first, second):
        sc = _dot_nt(jnp.where(mask, qn, 0), k) * QK_SCALE
        e = jnp.exp(sc - jnp.max(sc, axis=-1, keepdims=True))
        denom = jnp.sum(e, axis=-1, keepdims=True)
        out = out + _dot(e.astype(BF16), jnp.where(mask, v, 0)) / denom
    o_ref[...] = (out * _silu(g_ref[...].astype(F32))).astype(BF16)


def _mem_attention(proj, mk, mv, q_norm_g, b, s, tokens):
    nq = s // MEM_ROWS
    kv_spec = pl.BlockSpec((tokens, LANES), lambda bi, p, i: (bi, p))
    return pl.pallas_call(
        _mem_attn_kernel,
        grid=(b, MEM_PAIRS, nq),
        in_specs=[pl.BlockSpec((MEM_ROWS, LANES), lambda bi, p, i: (bi * nq + i, COL_MEM_Q + p)),
                  pl.BlockSpec((MEM_ROWS, LANES), lambda bi, p, i: (bi * nq + i, COL_MEM_G + p)),
                  kv_spec, kv_spec,
                  pl.BlockSpec((1, LANES), lambda bi, p, i: (0, 0))],
        out_specs=pl.BlockSpec((MEM_ROWS, LANES), lambda bi, p, i: (bi * nq + i, p)),
        out_shape=jax.ShapeDtypeStruct((b * s, MEM_WIDTH), BF16),
        compiler_params=pltpu.CompilerParams(vmem_limit_bytes=VMEM_LIMIT),
        name="mem_attention",
    )(proj, proj, mk, mv, q_norm_g)


def _out_proj_kernel(sb_ref, ret_ref, mem_ref, w_ref, x_ref, o_ref):
    y = _dot(sb_ref[...], w_ref[0:SB_WIDTH, :])
    y = y + _dot(ret_ref[...], w_ref[SB_WIDTH:SB_WIDTH + RET_WIDTH, :])
    y = y + _dot(mem_ref[...], w_ref[SB_WIDTH + RET_WIDTH:MIX_WIDTH, :])
    o_ref[...] = x_ref[...] + y


def _out_proj(sb_o, ret_o, mem_o, w, x2):
    m = x2.shape[0]
    rows = lambda width: pl.BlockSpec((PROJ_ROWS, width), lambda i: (i, 0))
    return pl.pallas_call(
        _out_proj_kernel,
        grid=(m // PROJ_ROWS,),
        in_specs=[rows(SB_WIDTH), rows(RET_WIDTH), rows(MEM_WIDTH),
                  pl.BlockSpec((MIX_WIDTH, D_MODEL), lambda i: (0, 0)),
                  rows(D_MODEL)],
        out_specs=rows(D_MODEL),
        out_shape=jax.ShapeDtypeStruct((m, D_MODEL), F32),
        compiler_params=pltpu.CompilerParams(vmem_limit_bytes=VMEM_LIMIT),
        name="out_proj",
    )(sb_o, ret_o, mem_o, w, x2)


def kernel(x, mem, positions, norm_g, w_in, w_out, mem_norm_g, w_mem_kv, mem_q_norm_g,
           mem_k_norm_g, ret_norm_g):
    b, s, d = x.shape
    tokens = mem.shape[1]
    depth = w_in.shape[0]
    assert d == D_MODEL and w_in.shape[2] == IN_WIDTH
    assert s % SB_BLOCK == 0 and s % RET_CHUNK == 0 and s % MEM_ROWS == 0
    assert (b * s) % PROJ_ROWS == 0

    w_in_b = w_in.astype(BF16)
    w_out_b = w_out.astype(BF16)
    w_kv_b = w_mem_kv.astype(BF16)
    q_norm_g2 = jnp.tile(mem_q_norm_g, (1, PAIR))
    k_norm_g3 = jnp.tile(mem_k_norm_g, (1, PAIR))[:, None, :]

    cos, sin = _rope_tables(positions)
    mk_all, mv_all = _mem_kv(mem.reshape(b * tokens, d), mem_norm_g[:, None, :], w_kv_b,
                             k_norm_g3, b, tokens)

    x2 = x.reshape(b * s, d)
    for l in range(depth):
        proj = _in_proj(x2, norm_g[l][None, :], w_in_b[l])
        sb_o = _sb_attention(proj, b, s)
        ret_o = _retention(proj, cos, sin, ret_norm_g[l][None, :], b, s)
        mem_o = _mem_attention(proj, mk_all[l], mv_all[l], q_norm_g2[l][None, :], b, s, tokens)
        x2 = _out_proj(sb_o, ret_o, mem_o, w_out_b[l], x2)
    return x2.reshape(b, s, d)
```

```python
import functools
import math

import jax
import jax.numpy as jnp
from jax import lax
from jax.experimental import pallas as pl
from jax.experimental.pallas import tpu as pltpu

D_MODEL = 1024
HEAD_DIM = 64
SB_HEADS = 6
RET_HEADS = 6
MEM_HEADS = 4
SB_WIDTH = SB_HEADS * HEAD_DIM
RET_WIDTH = RET_HEADS * HEAD_DIM
MEM_WIDTH = MEM_HEADS * HEAD_DIM
MIX_WIDTH = SB_WIDTH + RET_WIDTH + MEM_WIDTH
IN_WIDTH = 4 * SB_WIDTH + 4 * RET_WIDTH + 2 * MEM_WIDTH
ROPE_BASE = 10000.0
EPS = 1e-6
QK_SCALE = HEAD_DIM ** -0.5
SB_Q_SCALE = QK_SCALE * math.log2(math.e)

LANES = 128
PAIR = LANES // HEAD_DIM
SB_PAIRS = SB_HEADS // PAIR
RET_PAIRS = RET_HEADS // PAIR
MEM_PAIRS = MEM_HEADS // PAIR

COL_SB_Q = 0
COL_SB_K = COL_SB_Q + SB_PAIRS
COL_SB_V = COL_SB_K + SB_PAIRS
COL_SB_G = COL_SB_V + SB_PAIRS
COL_RET_Q = COL_SB_G + SB_PAIRS
COL_RET_K = COL_RET_Q + RET_PAIRS
COL_RET_V = COL_RET_K + RET_PAIRS
COL_RET_G = COL_RET_V + RET_PAIRS
COL_MEM_Q = COL_RET_G + RET_PAIRS
COL_MEM_G = COL_MEM_Q + MEM_PAIRS

PROJ_ROWS = 512
PROJ_COLS = 512
SB_BLOCK = 256
RET_CHUNK = 256
MEM_ROWS = 512
VMEM_LIMIT = 48 * 1024 * 1024

F32 = jnp.float32
BF16 = jnp.bfloat16


def _dot(a, b):
    return jnp.dot(a, b, preferred_element_type=F32)


def _dot_nt(a, b):
    return lax.dot_general(a, b, (((1,), (1,)), ((), ())), preferred_element_type=F32)


def _dot_tn(a, b):
    return lax.dot_general(a, b, (((0,), (0,)), ((), ())), preferred_element_type=F32)


def _silu(g):
    return g / (1.0 + jnp.exp(-g))


def _head_masks():
    lane = lax.broadcasted_iota(jnp.int32, (1, LANES), 1)
    first = lane < HEAD_DIM
    return first, jnp.logical_not(first)


def _pair_mean_square(t, first):
    sq = t * t
    s_a = jnp.sum(jnp.where(first, sq, 0.0), axis=-1, keepdims=True)
    s_b = jnp.sum(jnp.where(first, 0.0, sq), axis=-1, keepdims=True)
    return jnp.where(first, s_a, s_b) * (1.0 / HEAD_DIM)


def _rope_kernel(pos_ref, invf_ref, cos_ref, sin_ref):
    ang = pos_ref[...].astype(F32) * invf_ref[...]
    lane = lax.broadcasted_iota(jnp.int32, (1, LANES), 1)
    sign = jnp.where((lane % HEAD_DIM) < HEAD_DIM // 2, -1.0, 1.0)
    cos_ref[...] = jnp.cos(ang)
    sin_ref[...] = jnp.sin(ang) * sign


def _rope_tables(positions):
    b, s = positions.shape
    half = HEAD_DIM // 2
    inv_freq = ROPE_BASE ** (-jnp.arange(half, dtype=F32) / half)
    invf = jnp.tile(inv_freq, LANES // half)[None, :]
    pos = positions.reshape(b * s, 1)
    rows = s
    return pl.pallas_call(
        _rope_kernel,
        grid=(b * s // rows,),
        in_specs=[pl.BlockSpec((rows, 1), lambda i: (i, 0)),
                  pl.BlockSpec((1, LANES), lambda i: (0, 0))],
        out_specs=[pl.BlockSpec((rows, LANES), lambda i: (i, 0)),
                   pl.BlockSpec((rows, LANES), lambda i: (i, 0))],
        out_shape=[jax.ShapeDtypeStruct((b * s, LANES), F32)] * 2,
        name="rope_tables",
    )(pos, invf)


def _in_proj_kernel(x_ref, g_ref, w_ref, o_ref):
    x = x_ref[...]
    ms = jnp.mean(x * x, axis=-1, keepdims=True)
    h = (x * lax.rsqrt(ms + EPS) * g_ref[...]).astype(BF16)
    for j in range(IN_WIDTH // PROJ_COLS):
        cols = slice(j * PROJ_COLS, (j + 1) * PROJ_COLS)
        y = _dot(h, w_ref[:, cols])
        if cols.start < SB_WIDTH:
            col_id = lax.broadcasted_iota(jnp.int32, (1, PROJ_COLS), 1) + cols.start
            y = y * jnp.where(col_id < SB_WIDTH, SB_Q_SCALE, 1.0)
        o_ref[:, cols] = y.astype(BF16)


def _in_proj(x2, g, w):
    m = x2.shape[0]
    return pl.pallas_call(
        _in_proj_kernel,
        grid=(m // PROJ_ROWS,),
        in_specs=[pl.BlockSpec((PROJ_ROWS, D_MODEL), lambda i: (i, 0)),
                  pl.BlockSpec((1, D_MODEL), lambda i: (0, 0)),
                  pl.BlockSpec((D_MODEL, IN_WIDTH), lambda i: (0, 0))],
        out_specs=pl.BlockSpec((PROJ_ROWS, IN_WIDTH), lambda i: (i, 0)),
        out_shape=jax.ShapeDtypeStruct((m, IN_WIDTH), BF16),
        compiler_params=pltpu.CompilerParams(vmem_limit_bytes=VMEM_LIMIT),
        name="in_proj",
    )(x2, g, w)


SB_LAG = 2
SB_MASKED = -1e30


def _sb_schedule(nq):
    items = [(i, kb, int(kb == i)) for i in range(nq) for kb in range(i, -1, -1)]
    n_items = len(items)
    pad = [(0, 0, 0)] * SB_LAG
    table = pad + items + pad
    if len(table) % 2:
        table.append((0, 0, 0))
    return n_items, jnp.asarray(table, jnp.int32).T


def _sb_kernel(tab_ref, q_ref, k_ref, v_ref, g_ref, o_ref,
               tri_ref, z_ref, hi_ref, lo_ref, w_ref, r_ref, acc_ref, *, n_steps):
    t = SB_BLOCK
    first, second = _head_masks()
    row = lax.broadcasted_iota(jnp.int32, (t, t), 0)
    col = lax.broadcasted_iota(jnp.int32, (t, t), 1)
    tri_ref[...] = (row >= col).astype(BF16)
    z_ref[...] = jnp.zeros_like(z_ref)
    hi_ref[...] = jnp.zeros_like(hi_ref)
    lo_ref[...] = jnp.zeros_like(lo_ref)
    w_ref[...] = jnp.zeros_like(w_ref)
    r_ref[...] = jnp.zeros_like(r_ref)
    acc_ref[...] = jnp.zeros_like(acc_ref)

    def rows(blk):
        return pl.ds(pl.multiple_of(blk * t, t), t)

    def scores(j, slot):
        qb, kb = tab_ref[0, j], tab_ref[1, j]
        q = q_ref[rows(qb), :]
        q2 = jnp.concatenate([jnp.where(first, q, 0), jnp.where(second, q, 0)], axis=0)
        z = _dot_nt(q2, k_ref[rows(kb), :])
        causal = (col - row) < (qb - kb) * t
        z = jnp.where(jnp.concatenate([causal, causal], axis=0), z, SB_MASKED)
        neg_abs = pltpu.bitcast(pltpu.bitcast(z, jnp.uint32) | jnp.uint32(0x80000000), F32)
        sp = jnp.maximum(z, 0.0) + jnp.log2(1.0 + jnp.exp2(neg_abs))
        hi = sp.astype(BF16)
        z_ref[slot] = z
        hi_ref[slot] = hi
        lo_ref[slot] = (sp - hi.astype(F32)).astype(BF16)

    def weights(j, slot):
        is_first = tab_ref[2, j] == 1
        tri = tri_ref[...]
        cum = _dot(hi_ref[slot], tri) + _dot(lo_ref[slot], tri)
        r = jnp.where(is_first, 0.0, r_ref[...])
        z = z_ref[slot]
        halves = [jnp.exp2(z[:, c:c + LANES] - cum[:, c:c + LANES] - r) for c in (0, LANES)]
        w_ref[slot] = jnp.concatenate(halves, axis=1).astype(BF16)
        r_ref[...] = r + cum[:, 0:1]

    def values(j, slot):
        qb, kb = tab_ref[0, j], tab_ref[1, j]
        is_first = tab_ref[2, j] == 1
        pv = _dot(w_ref[slot], v_ref[rows(kb), :])
        contrib = jnp.where(first, pv[:t], pv[t:])
        acc_ref[rows(qb), :] = jnp.where(is_first, 0.0, acc_ref[rows(qb), :]) + contrib

    def step(n, _):
        for parity in (0, 1):
            c = 2 * n + parity
            values(c, parity)
            weights(c + 1, 1 - parity)
            scores(c + 2, parity)
        return 0

    lax.fori_loop(0, n_steps, step, 0)
    o_ref[...] = (acc_ref[...] * _silu(g_ref[...].astype(F32))).astype(BF16)


def _sb_attention(proj, b, s):
    assert SB_LAG == 2
    nq = s // SB_BLOCK
    n_items, table = _sb_schedule(nq)
    n_steps = (n_items + SB_LAG + 1) // 2
    assert table.shape[1] >= 2 * n_steps + SB_LAG
    t = SB_BLOCK
    seq = lambda col: pl.BlockSpec((s, LANES), lambda bi, p, tab: (bi, col + p))
    return pl.pallas_call(
        functools.partial(_sb_kernel, n_steps=n_steps),
        grid_spec=pltpu.PrefetchScalarGridSpec(
            num_scalar_prefetch=1,
            grid=(b, SB_PAIRS),
            in_specs=[seq(COL_SB_Q), seq(COL_SB_K), seq(COL_SB_V), seq(COL_SB_G)],
            out_specs=pl.BlockSpec((s, LANES), lambda bi, p, tab: (bi, p)),
            scratch_shapes=[pltpu.VMEM((t, t), BF16),
                            pltpu.VMEM((2, 2 * t, t), F32),
                            pltpu.VMEM((2, 2 * t, t), BF16),
                            pltpu.VMEM((2, 2 * t, t), BF16),
                            pltpu.VMEM((2, 2 * t, t), BF16),
                            pltpu.VMEM((2 * t, LANES), F32),
                            pltpu.VMEM((s, LANES), F32)]),
        out_shape=jax.ShapeDtypeStruct((b * s, SB_WIDTH), BF16),
        compiler_params=pltpu.CompilerParams(vmem_limit_bytes=VMEM_LIMIT),
        name="sb_attention",
    )(table, proj, proj, proj, proj)


def _swap_halves(t):
    lane = lax.broadcasted_iota(jnp.int32, (1, LANES), 1)
    half = HEAD_DIM // 2
    lower = (lane % HEAD_DIM) < half
    return jnp.where(lower, pltpu.roll(t, LANES - half, 1), pltpu.roll(t, half, 1))


def _ret_kernel(q_ref, k_ref, v_ref, g_ref, cos_ref, sin_ref, ng_ref, lg_ref, o_ref, state_ref):
    c = RET_CHUNK
    first, second = _head_masks()
    nchunks = q_ref.shape[0] // c

    lg_lane = lg_ref[0]
    lg_a = lg_lane[:, 0:1]
    lg_b = lg_lane[:, HEAD_DIM:HEAD_DIM + 1]

    ri = lax.broadcasted_iota(jnp.int32, (c, c), 0)
    ci = lax.broadcasted_iota(jnp.int32, (c, c), 1)
    diff = (ri - ci).astype(F32)
    keep = ri >= ci
    decay_a = jnp.where(keep, jnp.exp(lg_a * jnp.maximum(diff, 0.0)), 0.0)
    decay_b = jnp.where(keep, jnp.exp(lg_b * jnp.maximum(diff, 0.0)), 0.0)
    idx = lax.broadcasted_iota(jnp.int32, (c, LANES), 0).astype(F32)
    q_decay = jnp.exp(lg_lane * (idx + 1.0))
    k_decay = jnp.exp(lg_lane * (c - 1.0 - idx))
    si = lax.broadcasted_iota(jnp.int32, (LANES, LANES), 0)
    sj = lax.broadcasted_iota(jnp.int32, (LANES, LANES), 1)
    same_head = (si < HEAD_DIM) == (sj < HEAD_DIM)
    state_decay = jnp.where(same_head, jnp.exp(lg_lane * float(c)), 0.0)
    norm_g = ng_ref[...]

    state_ref[...] = jnp.zeros_like(state_ref)

    def chunk(n, _):
        rows = pl.ds(pl.multiple_of(n * c, c), c)
        cos = cos_ref[rows, :]
        sin = sin_ref[rows, :]
        qf = q_ref[rows, :].astype(F32)
        kf = k_ref[rows, :].astype(F32)
        qr = qf * cos + _swap_halves(qf) * sin
        kr = (kf * cos + _swap_halves(kf) * sin) * QK_SCALE
        qb = qr.astype(BF16)
        kb = kr.astype(BF16)
        v = v_ref[rows, :]
        state = state_ref[...]

        out = _dot(qb, state.astype(BF16)) * q_decay
        for mask, decay in ((first, decay_a), (second, decay_b)):
            scores = _dot_nt(jnp.where(mask, qb, 0), kb) * decay
            out = out + _dot(scores.astype(BF16), jnp.where(mask, v, 0))
        kd = (kr * k_decay).astype(BF16)
        state_ref[...] = state * state_decay + jnp.where(same_head, _dot_tn(kd, v), 0.0)

        inv = lax.rsqrt(_pair_mean_square(out, first) + EPS)
        y = out * inv * norm_g
        o_ref[rows, :] = (y * _silu(g_ref[rows, :].astype(F32))).astype(BF16)
        return 0

    lax.fori_loop(0, nchunks, chunk, 0)


def _retention(proj, cos, sin, ret_norm_g, b, s):
    seq = lambda col: pl.BlockSpec((s, LANES), lambda bi, p: (bi, col + p))
    tab = pl.BlockSpec((s, LANES), lambda bi, p: (bi, 0))
    gamma = 1.0 - 2.0 ** (-5.0 - jnp.arange(RET_HEADS, dtype=F32))
    log_gamma = jnp.repeat(jnp.log(gamma), HEAD_DIM).reshape(RET_PAIRS, 1, LANES)
    return pl.pallas_call(
        _ret_kernel,
        grid=(b, RET_PAIRS),
        in_specs=[seq(COL_RET_Q), seq(COL_RET_K), seq(COL_RET_V), seq(COL_RET_G), tab, tab,
                  pl.BlockSpec((1, LANES), lambda bi, p: (0, p)),
                  pl.BlockSpec((1, 1, LANES), lambda bi, p: (p, 0, 0))],
        out_specs=pl.BlockSpec((s, LANES), lambda bi, p: (bi, p)),
        out_shape=jax.ShapeDtypeStruct((b * s, RET_WIDTH), BF16),
        scratch_shapes=[pltpu.VMEM((LANES, LANES), F32)],
        compiler_params=pltpu.CompilerParams(vmem_limit_bytes=VMEM_LIMIT),
        name="retention",
    )(proj, proj, proj, proj, cos, sin, ret_norm_g, log_gamma)


def _mem_kv_kernel(mem_ref, g_ref, w_ref, kg_ref, k_ref, v_ref):
    first, _ = _head_masks()
    x = mem_ref[...]
    ms = jnp.mean(x * x, axis=-1, keepdims=True)
    h = (x * lax.rsqrt(ms + EPS) * g_ref[0]).astype(BF16)
    kv = _dot(h, w_ref[0])
    kg = kg_ref[0]
    for pair in range(MEM_PAIRS):
        cols = slice(pair * LANES, (pair + 1) * LANES)
        kp = kv[:, cols]
        kn = kp * lax.rsqrt(_pair_mean_square(kp, first) + EPS) * kg
        k_ref[0, :, cols] = kn.astype(BF16)
    v_ref[0] = kv[:, MEM_WIDTH:].astype(BF16)


def _mem_kv(mem2, mem_norm_g, w_mem_kv, k_norm_g, b, tokens):
    depth = w_mem_kv.shape[0]
    out = jax.ShapeDtypeStruct((depth, b * tokens, MEM_WIDTH), BF16)
    return pl.pallas_call(
        _mem_kv_kernel,
        grid=(depth, b),
        in_specs=[pl.BlockSpec((tokens, D_MODEL), lambda l, bi: (bi, 0)),
                  pl.BlockSpec((1, 1, D_MODEL), lambda l, bi: (l, 0, 0)),
                  pl.BlockSpec((1, D_MODEL, 2 * MEM_WIDTH), lambda l, bi: (l, 0, 0)),
                  pl.BlockSpec((1, 1, LANES), lambda l, bi: (l, 0, 0))],
        out_specs=[pl.BlockSpec((1, tokens, MEM_WIDTH), lambda l, bi: (l, bi, 0)),
                   pl.BlockSpec((1, tokens, MEM_WIDTH), lambda l, bi: (l, bi, 0))],
        out_shape=[out, out],
        compiler_params=pltpu.CompilerParams(vmem_limit_bytes=VMEM_LIMIT),
        name="mem_kv",
    )(mem2, mem_norm_g, w_mem_kv, k_norm_g)


def _mem_attn_kernel(q_ref, g_ref, k_ref, v_ref, qg_ref, o_ref):
    first, second = _head_masks()
    q = q_ref[...].astype(F32)
    qn = (q * lax.rsqrt(_pair_mean_square(q, first) + EPS) * qg_ref[...]).astype(BF16)
    k = k_ref[...]
    v = v_ref[...]
    out = jnp.zeros(q.shape, F32)
    for mask in (first, second):
        sc = _dot_nt(jnp.where(mask, qn, 0), k) * QK_SCALE
        e = jnp.exp(sc - jnp.max(sc, axis=-1, keepdims=True))
        denom = jnp.sum(e, axis=-1, keepdims=True)
        out = out + _dot(e.astype(BF16), jnp.where(mask, v, 0)) / denom
    o_ref[...] = (out * _silu(g_ref[...].astype(F32))).astype(BF16)


def _mem_attention(proj, mk, mv, q_norm_g, b, s, tokens):
    nq = s // MEM_ROWS
    kv_spec = pl.BlockSpec((tokens, LANES), lambda bi, p, i: (bi, p))
    return pl.pallas_call(
        _mem_attn_kernel,
        grid=(b, MEM_PAIRS, nq),
        in_specs=[pl.BlockSpec((MEM_ROWS, LANES), lambda bi, p, i: (bi * nq + i, COL_MEM_Q + p)),
                  pl.BlockSpec((MEM_ROWS, LANES), lambda bi, p, i: (bi * nq + i, COL_MEM_G + p)),
                  kv_spec, kv_spec,
                  pl.BlockSpec((1, LANES), lambda bi, p, i: (0, 0))],
        out_specs=pl.BlockSpec((MEM_ROWS, LANES), lambda bi, p, i: (bi * nq + i, p)),
        out_shape=jax.ShapeDtypeStruct((b * s, MEM_WIDTH), BF16),
        compiler_params=pltpu.CompilerParams(vmem_limit_bytes=VMEM_LIMIT),
        name="mem_attention",
    )(proj, proj, mk, mv, q_norm_g)


def _out_proj_kernel(sb_ref, ret_ref, mem_ref, w_ref, x_ref, o_ref):
    y = _dot(sb_ref[...], w_ref[0:SB_WIDTH, :])
    y = y + _dot(ret_ref[...], w_ref[SB_WIDTH:SB_WIDTH + RET_WIDTH, :])
    y = y + _dot(mem_ref[...], w_ref[SB_WIDTH + RET_WIDTH:MIX_WIDTH, :])
    o_ref[...] = x_ref[...] + y


def _out_proj(sb_o, ret_o, mem_o, w, x2):
    m = x2.shape[0]
    rows = lambda width: pl.BlockSpec((PROJ_ROWS, width), lambda i: (i, 0))
    return pl.pallas_call(
        _out_proj_kernel,
        grid=(m // PROJ_ROWS,),
        in_specs=[rows(SB_WIDTH), rows(RET_WIDTH), rows(MEM_WIDTH),
                  pl.BlockSpec((MIX_WIDTH, D_MODEL), lambda i: (0, 0)),
                  rows(D_MODEL)],
        out_specs=rows(D_MODEL),
        out_shape=jax.ShapeDtypeStruct((m, D_MODEL), F32),
        compiler_params=pltpu.CompilerParams(vmem_limit_bytes=VMEM_LIMIT),
        name="out_proj",
    )(sb_o, ret_o, mem_o, w, x2)


def kernel(x, mem, positions, norm_g, w_in, w_out, mem_norm_g, w_mem_kv, mem_q_norm_g,
           mem_k_norm_g, ret_norm_g):
    b, s, d = x.shape
    tokens = mem.shape[1]
    depth = w_in.shape[0]
    assert d == D_MODEL and w_in.shape[2] == IN_WIDTH
    assert s % SB_BLOCK == 0 and s % RET_CHUNK == 0 and s % MEM_ROWS == 0
    assert (b * s) % PROJ_ROWS == 0

    w_in_b = w_in.astype(BF16)
    w_out_b = w_out.astype(BF16)
    w_kv_b = w_mem_kv.astype(BF16)
    q_norm_g2 = jnp.tile(mem_q_norm_g, (1, PAIR))
    k_norm_g3 = jnp.tile(mem_k_norm_g, (1, PAIR))[:, None, :]

    cos, sin = _rope_tables(positions)
    mk_all, mv_all = _mem_kv(mem.reshape(b * tokens, d), mem_norm_g[:, None, :], w_kv_b,
                             k_norm_g3, b, tokens)

    x2 = x.reshape(b * s, d)
    for l in range(depth):
        proj = _in_proj(x2, norm_g[l][None, :], w_in_b[l])
        sb_o = _sb_attention(proj, b, s)
        ret_o = _retention(proj, cos, sin, ret_norm_g[l][None, :], b, s)
        mem_o = _mem_attention(proj, mk_all[l], mv_all[l], q_norm_g2[l][None, :], b, s, tokens)
        x2 = _out_proj(sb_o, ret_o, mem_o, w_out_b[l], x2)
    return x2.reshape(b, s, d)
```

```python
import functools
import math

import jax
import jax.numpy as jnp
from jax import lax
from jax.experimental import pallas as pl
from jax.experimental.pallas import tpu as pltpu

D_MODEL = 1024
HEAD_DIM = 64
SB_HEADS = 6
RET_HEADS = 6
MEM_HEADS = 4
SB_WIDTH = SB_HEADS * HEAD_DIM
RET_WIDTH = RET_HEADS * HEAD_DIM
MEM_WIDTH = MEM_HEADS * HEAD_DIM
MIX_WIDTH = SB_WIDTH + RET_WIDTH + MEM_WIDTH
IN_WIDTH = 4 * SB_WIDTH + 4 * RET_WIDTH + 2 * MEM_WIDTH
ROPE_BASE = 10000.0
EPS = 1e-6
QK_SCALE = HEAD_DIM ** -0.5
SB_Q_SCALE = QK_SCALE * math.log2(math.e)

LANES = 128
PAIR = LANES // HEAD_DIM
SB_PAIRS = SB_HEADS // PAIR
RET_PAIRS = RET_HEADS // PAIR
MEM_PAIRS = MEM_HEADS // PAIR

COL_SB_Q = 0
COL_SB_K = COL_SB_Q + SB_PAIRS
COL_SB_V = COL_SB_K + SB_PAIRS
COL_SB_G = COL_SB_V + SB_PAIRS
COL_RET_Q = COL_SB_G + SB_PAIRS
COL_RET_K = COL_RET_Q + RET_PAIRS
COL_RET_V = COL_RET_K + RET_PAIRS
COL_RET_G = COL_RET_V + RET_PAIRS
COL_MEM_Q = COL_RET_G + RET_PAIRS
COL_MEM_G = COL_MEM_Q + MEM_PAIRS

PROJ_ROWS = 512
PROJ_COLS = 512
SB_BLOCK = 256
RET_CHUNK = 256
MEM_ROWS = 512
VMEM_LIMIT = 48 * 1024 * 1024

F32 = jnp.float32
BF16 = jnp.bfloat16


def _dot(a, b):
    return jnp.dot(a, b, preferred_element_type=F32)


def _dot_nt(a, b):
    return lax.dot_general(a, b, (((1,), (1,)), ((), ())), preferred_element_type=F32)


def _dot_tn(a, b):
    return lax.dot_general(a, b, (((0,), (0,)), ((), ())), preferred_element_type=F32)


def _silu(g):
    return g / (1.0 + jnp.exp(-g))


def _head_masks():
    lane = lax.broadcasted_iota(jnp.int32, (1, LANES), 1)
    first = lane < HEAD_DIM
    return first, jnp.logical_not(first)


def _pair_mean_square(t, first):
    sq = t * t
    s_a = jnp.sum(jnp.where(first, sq, 0.0), axis=-1, keepdims=True)
    s_b = jnp.sum(jnp.where(first, 0.0, sq), axis=-1, keepdims=True)
    return jnp.where(first, s_a, s_b) * (1.0 / HEAD_DIM)


def _rope_kernel(pos_ref, invf_ref, cos_ref, sin_ref):
    ang = pos_ref[...].astype(F32) * invf_ref[...]
    lane = lax.broadcasted_iota(jnp.int32, (1, LANES), 1)
    sign = jnp.where((lane % HEAD_DIM) < HEAD_DIM // 2, -1.0, 1.0)
    cos_ref[...] = jnp.cos(ang)
    sin_ref[...] = jnp.sin(ang) * sign


def _rope_tables(positions):
    b, s = positions.shape
    half = HEAD_DIM // 2
    inv_freq = ROPE_BASE ** (-jnp.arange(half, dtype=F32) / half)
    invf = jnp.tile(inv_freq, LANES // half)[None, :]
    pos = positions.reshape(b * s, 1)
    rows = s
    return pl.pallas_call(
        _rope_kernel,
        grid=(b * s // rows,),
        in_specs=[pl.BlockSpec((rows, 1), lambda i: (i, 0)),
                  pl.BlockSpec((1, LANES), lambda i: (0, 0))],
        out_specs=[pl.BlockSpec((rows, LANES), lambda i: (i, 0)),
                   pl.BlockSpec((rows, LANES), lambda i: (i, 0))],
        out_shape=[jax.ShapeDtypeStruct((b * s, LANES), F32)] * 2,
        name="rope_tables",
    )(pos, invf)


def _in_proj_kernel(x_ref, g_ref, w_ref, o_ref):
    x = x_ref[...]
    ms = jnp.mean(x * x, axis=-1, keepdims=True)
    h = (x * lax.rsqrt(ms + EPS) * g_ref[...]).astype(BF16)
    for j in range(IN_WIDTH // PROJ_COLS):
        cols = slice(j * PROJ_COLS, (j + 1) * PROJ_COLS)
        y = _dot(h, w_ref[:, cols])
        if cols.start < SB_WIDTH:
            col_id = lax.broadcasted_iota(jnp.int32, (1, PROJ_COLS), 1) + cols.start
            y = y * jnp.where(col_id < SB_WIDTH, SB_Q_SCALE, 1.0)
        o_ref[:, cols] = y.astype(BF16)


def _in_proj(x2, g, w):
    m = x2.shape[0]
    return pl.pallas_call(
        _in_proj_kernel,
        grid=(m // PROJ_ROWS,),
        in_specs=[pl.BlockSpec((PROJ_ROWS, D_MODEL), lambda i: (i, 0)),
                  pl.BlockSpec((1, D_MODEL), lambda i: (0, 0)),
                  pl.BlockSpec((D_MODEL, IN_WIDTH), lambda i: (0, 0))],
        out_specs=pl.BlockSpec((PROJ_ROWS, IN_WIDTH), lambda i: (i, 0)),
        out_shape=jax.ShapeDtypeStruct((m, IN_WIDTH), BF16),
        compiler_params=pltpu.CompilerParams(vmem_limit_bytes=VMEM_LIMIT),
        name="in_proj",
    )(x2, g, w)


SB_GAP = 1
SB_LAG = 2 * SB_GAP
SB_SLOTS = 4
SB_MASKED = -1e30


def _sb_schedule(nq):
    items = [(i, kb, int(kb == i)) for i in range(nq) for kb in range(i, -1, -1)]
    n_steps = -(-(len(items) + SB_LAG) // SB_SLOTS)
    idle = (0, 0, 0)
    table = [idle] * SB_LAG + items
    table += [idle] * (SB_SLOTS * n_steps + SB_LAG - len(table))
    return n_steps, jnp.asarray(table, jnp.int32).T


def _sb_kernel(tab_ref, q_ref, k_ref, v_ref, g_ref, o_ref,
               tri_ref, bias_ref, z_ref, sp_ref, w_ref, r_ref, acc_ref, *, n_steps):
    t = SB_BLOCK
    first, second = _head_masks()
    row = lax.broadcasted_iota(jnp.int32, (t, t), 0)
    col = lax.broadcasted_iota(jnp.int32, (t, t), 1)
    tri_ref[...] = (row >= col).astype(BF16)
    bias_ref[0] = jnp.zeros((t, t), F32)
    bias_ref[1] = jnp.where(col < row, 0.0, SB_MASKED)
    z_ref[...] = jnp.full(z_ref.shape, SB_MASKED, F32)
    sp_ref[...] = jnp.zeros_like(sp_ref)
    w_ref[...] = jnp.zeros_like(w_ref)
    r_ref[...] = jnp.zeros_like(r_ref)
    acc_ref[...] = jnp.zeros_like(acc_ref)

    def rows(blk):
        return pl.ds(pl.multiple_of(blk * t, t), t)

    def scores(j, slot):
        qb, kb = tab_ref[0, j], tab_ref[1, j]
        q = q_ref[rows(qb), :]
        q2 = jnp.concatenate([jnp.where(first, q, 0), jnp.where(second, q, 0)], axis=0)
        bias = bias_ref[tab_ref[2, j]]
        z = _dot_nt(q2, k_ref[rows(kb), :]) + jnp.concatenate([bias, bias], axis=0)
        sp = jnp.maximum(z, 0.0) + jnp.log2(1.0 + jnp.exp2(-jnp.abs(z)))
        z_ref[slot] = z
        sp_ref[slot] = sp.astype(BF16)

    def weights(j, slot):
        cum = _dot(sp_ref[slot], tri_ref[...])
        r = r_ref[tab_ref[2, j]]
        z = z_ref[slot]
        halves = [jnp.exp2(z[:, c:c + LANES] - cum[:, c:c + LANES] - r) for c in (0, LANES)]
        w_ref[slot] = jnp.concatenate(halves, axis=1).astype(BF16)
        r_ref[0] = r + cum[:, 0:1]

    def values(j, slot):
        qb, kb = tab_ref[0, j], tab_ref[1, j]
        pv = _dot(w_ref[slot], v_ref[rows(kb), :])
        acc_ref[rows(qb), :] += jnp.where(first, pv[:t], pv[t:])

    def step(n, _):
        for parity in range(SB_SLOTS):
            c = SB_SLOTS * n + parity
            values(c, parity)
            weights(c + SB_GAP, (parity + SB_GAP) % SB_SLOTS)
            scores(c + 2 * SB_GAP, (parity + 2 * SB_GAP) % SB_SLOTS)
        return 0

    lax.fori_loop(0, n_steps, step, 0)
    o_ref[...] = (acc_ref[...] * _silu(g_ref[...].astype(F32))).astype(BF16)


def _sb_attention(proj, b, s):
    assert SB_GAP < SB_SLOTS
    nq = s // SB_BLOCK
    n_steps, table = _sb_schedule(nq)
    t = SB_BLOCK
    seq = lambda col: pl.BlockSpec((s, LANES), lambda bi, p, tab: (bi, col + p))
    return pl.pallas_call(
        functools.partial(_sb_kernel, n_steps=n_steps),
        grid_spec=pltpu.PrefetchScalarGridSpec(
            num_scalar_prefetch=1,
            grid=(b, SB_PAIRS),
            in_specs=[seq(COL_SB_Q), seq(COL_SB_K), seq(COL_SB_V), seq(COL_SB_G)],
            out_specs=pl.BlockSpec((s, LANES), lambda bi, p, tab: (bi, p)),
            scratch_shapes=[pltpu.VMEM((t, t), BF16),
                            pltpu.VMEM((2, t, t), F32),
                            pltpu.VMEM((SB_SLOTS, 2 * t, t), F32),
                            pltpu.VMEM((SB_SLOTS, 2 * t, t), BF16),
                            pltpu.VMEM((SB_SLOTS, 2 * t, t), BF16),
                            pltpu.VMEM((2, 2 * t, LANES), F32),
                            pltpu.VMEM((s, LANES), F32)]),
        out_shape=jax.ShapeDtypeStruct((b * s, SB_WIDTH), BF16),
        compiler_params=pltpu.CompilerParams(vmem_limit_bytes=VMEM_LIMIT),
        name="sb_attention",
    )(table, proj, proj, proj, proj)


def _swap_halves(t):
    lane = lax.broadcasted_iota(jnp.int32, (1, LANES), 1)
    half = HEAD_DIM // 2
    lower = (lane % HEAD_DIM) < half
    return jnp.where(lower, pltpu.roll(t, LANES - half, 1), pltpu.roll(t, half, 1))


def _ret_kernel(q_ref, k_ref, v_ref, g_ref, cos_ref, sin_ref, ng_ref, lg_ref, o_ref, state_ref):
    c = RET_CHUNK
    first, second = _head_masks()
    nchunks = q_ref.shape[0] // c

    lg_lane = lg_ref[0]
    lg_a = lg_lane[:, 0:1]
    lg_b = lg_lane[:, HEAD_DIM:HEAD_DIM + 1]

    ri = lax.broadcasted_iota(jnp.int32, (c, c), 0)
    ci = lax.broadcasted_iota(jnp.int32, (c, c), 1)
    diff = (ri - ci).astype(F32)
    keep = ri >= ci
    decay_a = jnp.where(keep, jnp.exp(lg_a * jnp.maximum(diff, 0.0)), 0.0)
    decay_b = jnp.where(keep, jnp.exp(lg_b * jnp.maximum(diff, 0.0)), 0.0)
    idx = lax.broadcasted_iota(jnp.int32, (c, LANES), 0).astype(F32)
    q_decay = jnp.exp(lg_lane * (idx + 1.0))
    k_decay = jnp.exp(lg_lane * (c - 1.0 - idx))
    si = lax.broadcasted_iota(jnp.int32, (LANES, LANES), 0)
    sj = lax.broadcasted_iota(jnp.int32, (LANES, LANES), 1)
    same_head = (si < HEAD_DIM) == (sj < HEAD_DIM)
    state_decay = jnp.where(same_head, jnp.exp(lg_lane * float(c)), 0.0)
    norm_g = ng_ref[...]

    state_ref[...] = jnp.zeros_like(state_ref)

    def chunk(n, _):
        rows = pl.ds(pl.multiple_of(n * c, c), c)
        cos = cos_ref[rows, :]
        sin = sin_ref[rows, :]
        qf = q_ref[rows, :].astype(F32)
        kf = k_ref[rows, :].astype(F32)
        qr = qf * cos + _swap_halves(qf) * sin
        kr = (kf * cos + _swap_halves(kf) * sin) * QK_SCALE
        qb = qr.astype(BF16)
        kb = kr.astype(BF16)
        v = v_ref[rows, :]
        state = state_ref[...]

        out = _dot(qb, state.astype(BF16)) * q_decay
        for mask, decay in ((first, decay_a), (second, decay_b)):
            scores = _dot_nt(jnp.where(mask, qb, 0), kb) * decay
            out = out + _dot(scores.astype(BF16), jnp.where(mask, v, 0))
        kd = (kr * k_decay).astype(BF16)
        state_ref[...] = state * state_decay + jnp.where(same_head, _dot_tn(kd, v), 0.0)

        inv = lax.rsqrt(_pair_mean_square(out, first) + EPS)
        y = out * inv * norm_g
        o_ref[rows, :] = (y * _silu(g_ref[rows, :].astype(F32))).astype(BF16)
        return 0

    lax.fori_loop(0, nchunks, chunk, 0)


def _retention(proj, cos, sin, ret_norm_g, b, s):
    seq = lambda col: pl.BlockSpec((s, LANES), lambda bi, p: (bi, col + p))
    tab = pl.BlockSpec((s, LANES), lambda bi, p: (bi, 0))
    gamma = 1.0 - 2.0 ** (-5.0 - jnp.arange(RET_HEADS, dtype=F32))
    log_gamma = jnp.repeat(jnp.log(gamma), HEAD_DIM).reshape(RET_PAIRS, 1, LANES)
    return pl.pallas_call(
        _ret_kernel,
        grid=(b, RET_PAIRS),
        in_specs=[seq(COL_RET_Q), seq(COL_RET_K), seq(COL_RET_V), seq(COL_RET_G), tab, tab,
                  pl.BlockSpec((1, LANES), lambda bi, p: (0, p)),
                  pl.BlockSpec((1, 1, LANES), lambda bi, p: (p, 0, 0))],
        out_specs=pl.BlockSpec((s, LANES), lambda bi, p: (bi, p)),
        out_shape=jax.ShapeDtypeStruct((b * s, RET_WIDTH), BF16),
        scratch_shapes=[pltpu.VMEM((LANES, LANES), F32)],
        compiler_params=pltpu.CompilerParams(vmem_limit_bytes=VMEM_LIMIT),
        name="retention",
    )(proj, proj, proj, proj, cos, sin, ret_norm_g, log_gamma)


def _mem_kv_kernel(mem_ref, g_ref, w_ref, kg_ref, k_ref, v_ref):
    first, _ = _head_masks()
    x = mem_ref[...]
    ms = jnp.mean(x * x, axis=-1, keepdims=True)
    h = (x * lax.rsqrt(ms + EPS) * g_ref[0]).astype(BF16)
    kv = _dot(h, w_ref[0])
    kg = kg_ref[0]
    for pair in range(MEM_PAIRS):
        cols = slice(pair * LANES, (pair + 1) * LANES)
        kp = kv[:, cols]
        kn = kp * lax.rsqrt(_pair_mean_square(kp, first) + EPS) * kg
        k_ref[0, :, cols] = kn.astype(BF16)
    v_ref[0] = kv[:, MEM_WIDTH:].astype(BF16)


def _mem_kv(mem2, mem_norm_g, w_mem_kv, k_norm_g, b, tokens):
    depth = w_mem_kv.shape[0]
    out = jax.ShapeDtypeStruct((depth, b * tokens, MEM_WIDTH), BF16)
    return pl.pallas_call(
        _mem_kv_kernel,
        grid=(depth, b),
        in_specs=[pl.BlockSpec((tokens, D_MODEL), lambda l, bi: (bi, 0)),
                  pl.BlockSpec((1, 1, D_MODEL), lambda l, bi: (l, 0, 0)),
                  pl.BlockSpec((1, D_MODEL, 2 * MEM_WIDTH), lambda l, bi: (l, 0, 0)),
                  pl.BlockSpec((1, 1, LANES), lambda l, bi: (l, 0, 0))],
        out_specs=[pl.BlockSpec((1, tokens, MEM_WIDTH), lambda l, bi: (l, bi, 0)),
                   pl.BlockSpec((1, tokens, MEM_WIDTH), lambda l, bi: (l, bi, 0))],
        out_shape=[out, out],
        compiler_params=pltpu.CompilerParams(vmem_limit_bytes=VMEM_LIMIT),
        name="mem_kv",
    )(mem2, mem_norm_g, w_mem_kv, k_norm_g)


def _mem_attn_kernel(q_ref, g_ref, k_ref, v_ref, qg_ref, o_ref):
    first, second = _head_masks()
    q = q_ref[...].astype(F32)
    qn = (q * lax.rsqrt(_pair_mean_square(q, first) + EPS) * qg_ref[...]).astype(BF16)
    k = k_ref[...]
    v = v_ref[...]
    out = jnp.zeros(q.shape, F32)
    for mask in (first, second):
        sc = _dot_nt(jnp.where(mask, qn, 0), k) * QK_SCALE
        e = jnp.exp(sc - jnp.max(sc, axis=-1, keepdims=True))
        denom = jnp.sum(e, axis=-1, keepdims=True)
        out = out + _dot(e.astype(BF16), jnp.where(mask, v, 0)) / denom
    o_ref[...] = (out * _silu(g_ref[...].astype(F32))).astype(BF16)


def _mem_attention(proj, mk, mv, q_norm_g, b, s, tokens):
    nq = s // MEM_ROWS
    kv_spec = pl.BlockSpec((tokens, LANES), lambda bi, p, i: (bi, p))
    return pl.pallas_call(
        _mem_attn_kernel,
        grid=(b, MEM_PAIRS, nq),
        in_specs=[pl.BlockSpec((MEM_ROWS, LANES), lambda bi, p, i: (bi * nq + i, COL_MEM_Q + p)),
                  pl.BlockSpec((MEM_ROWS, LANES), lambda bi, p, i: (bi * nq + i, COL_MEM_G + p)),
                  kv_spec, kv_spec,
                  pl.BlockSpec((1, LANES), lambda bi, p, i: (0, 0))],
        out_specs=pl.BlockSpec((MEM_ROWS, LANES), lambda bi, p, i: (bi * nq + i, p)),
        out_shape=jax.ShapeDtypeStruct((b * s, MEM_WIDTH), BF16),
        compiler_params=pltpu.CompilerParams(vmem_limit_bytes=VMEM_LIMIT),
        name="mem_attention",
    )(proj, proj, mk, mv, q_norm_g)


def _out_proj_kernel(sb_ref, ret_ref, mem_ref, w_ref, x_ref, o_ref):
    y = _dot(sb_ref[...], w_ref[0:SB_WIDTH, :])
    y = y + _dot(ret_ref[...], w_ref[SB_WIDTH:SB_WIDTH + RET_WIDTH, :])
    y = y + _dot(mem_ref[...], w_ref[SB_WIDTH + RET_WIDTH:MIX_WIDTH, :])
    o_ref[...] = x_ref[...] + y


def _out_proj(sb_o, ret_o, mem_o, w, x2):
    m = x2.shape[0]
    rows = lambda width: pl.BlockSpec((PROJ_ROWS, width), lambda i: (i, 0))
    return pl.pallas_call(
        _out_proj_kernel,
        grid=(m // PROJ_ROWS,),
        in_specs=[rows(SB_WIDTH), rows(RET_WIDTH), rows(MEM_WIDTH),
                  pl.BlockSpec((MIX_WIDTH, D_MODEL), lambda i: (0, 0)),
                  rows(D_MODEL)],
        out_specs=rows(D_MODEL),
        out_shape=jax.ShapeDtypeStruct((m, D_MODEL), F32),
        compiler_params=pltpu.CompilerParams(vmem_limit_bytes=VMEM_LIMIT),
        name="out_proj",
    )(sb_o, ret_o, mem_o, w, x2)


def kernel(x, mem, positions, norm_g, w_in, w_out, mem_norm_g, w_mem_kv, mem_q_norm_g,
           mem_k_norm_g, ret_norm_g):
    b, s, d = x.shape
    tokens = mem.shape[1]
    depth = w_in.shape[0]
    assert d == D_MODEL and w_in.shape[2] == IN_WIDTH
    assert s % SB_BLOCK == 0 and s % RET_CHUNK == 0 and s % MEM_ROWS == 0
    assert (b * s) % PROJ_ROWS == 0

    w_in_b = w_in.astype(BF16)
    w_out_b = w_out.astype(BF16)
    w_kv_b = w_mem_kv.astype(BF16)
    q_norm_g2 = jnp.tile(mem_q_norm_g, (1, PAIR))
    k_norm_g3 = jnp.tile(mem_k_norm_g, (1, PAIR))[:, None, :]

    cos, sin = _rope_tables(positions)
    mk_all, mv_all = _mem_kv(mem.reshape(b * tokens, d), mem_norm_g[:, None, :], w_kv_b,
                             k_norm_g3, b, tokens)

    x2 = x.reshape(b * s, d)
    for l in range(depth):
        proj = _in_proj(x2, norm_g[l][None, :], w_in_b[l])
        sb_o = _sb_attention(proj, b, s)
        ret_o = _retention(proj, cos, sin, ret_norm_g[l][None, :], b, s)
        mem_o = _mem_attention(proj, mk_all[l], mv_all[l], q_norm_g2[l][None, :], b, s, tokens)
        x2 = _out_proj(sb_o, ret_o, mem_o, w_out_b[l], x2)
    return x2.reshape(b, s, d)
```

```python
import functools
import math

import jax
import jax.numpy as jnp
from jax import lax
from jax.experimental import pallas as pl
from jax.experimental.pallas import tpu as pltpu

D_MODEL = 1024
HEAD_DIM = 64
SB_HEADS = 6
RET_HEADS = 6
MEM_HEADS = 4
SB_WIDTH = SB_HEADS * HEAD_DIM
RET_WIDTH = RET_HEADS * HEAD_DIM
MEM_WIDTH = MEM_HEADS * HEAD_DIM
MIX_WIDTH = SB_WIDTH + RET_WIDTH + MEM_WIDTH
IN_WIDTH = 4 * SB_WIDTH + 4 * RET_WIDTH + 2 * MEM_WIDTH
ROPE_BASE = 10000.0
EPS = 1e-6
QK_SCALE = HEAD_DIM ** -0.5
SB_Q_SCALE = QK_SCALE * math.log2(math.e)

LANES = 128
PAIR = LANES // HEAD_DIM
SB_PAIRS = SB_HEADS // PAIR
RET_PAIRS = RET_HEADS // PAIR
MEM_PAIRS = MEM_HEADS // PAIR

COL_SB_Q = 0
COL_SB_K = COL_SB_Q + SB_PAIRS
COL_SB_V = COL_SB_K + SB_PAIRS
COL_SB_G = COL_SB_V + SB_PAIRS
COL_RET_Q = COL_SB_G + SB_PAIRS
COL_RET_K = COL_RET_Q + RET_PAIRS
COL_RET_V = COL_RET_K + RET_PAIRS
COL_RET_G = COL_RET_V + RET_PAIRS
COL_MEM_Q = COL_RET_G + RET_PAIRS
COL_MEM_G = COL_MEM_Q + MEM_PAIRS

PROJ_ROWS = 512
PROJ_COLS = 512
SB_BLOCK = 256
RET_CHUNK = 256
MEM_ROWS = 512
VMEM_LIMIT = 48 * 1024 * 1024

F32 = jnp.float32
BF16 = jnp.bfloat16


def _dot(a, b):
    return jnp.dot(a, b, preferred_element_type=F32)


def _dot_nt(a, b):
    return lax.dot_general(a, b, (((1,), (1,)), ((), ())), preferred_element_type=F32)


def _dot_tn(a, b):
    return lax.dot_general(a, b, (((0,), (0,)), ((), ())), preferred_element_type=F32)


def _silu(g):
    return g / (1.0 + jnp.exp(-g))


def _head_masks():
    lane = lax.broadcasted_iota(jnp.int32, (1, LANES), 1)
    first = lane < HEAD_DIM
    return first, jnp.logical_not(first)


def _pair_mean_square(t, first):
    sq = t * t
    s_a = jnp.sum(jnp.where(first, sq, 0.0), axis=-1, keepdims=True)
    s_b = jnp.sum(jnp.where(first, 0.0, sq), axis=-1, keepdims=True)
    return jnp.where(first, s_a, s_b) * (1.0 / HEAD_DIM)


def _rope_kernel(pos_ref, invf_ref, cos_ref, sin_ref):
    ang = pos_ref[...].astype(F32) * invf_ref[...]
    lane = lax.broadcasted_iota(jnp.int32, (1, LANES), 1)
    sign = jnp.where((lane % HEAD_DIM) < HEAD_DIM // 2, -1.0, 1.0)
    cos_ref[...] = jnp.cos(ang)
    sin_ref[...] = jnp.sin(ang) * sign


def _rope_tables(positions):
    b, s = positions.shape
    half = HEAD_DIM // 2
    inv_freq = ROPE_BASE ** (-jnp.arange(half, dtype=F32) / half)
    invf = jnp.tile(inv_freq, LANES // half)[None, :]
    pos = positions.reshape(b * s, 1)
    rows = s
    return pl.pallas_call(
        _rope_kernel,
        grid=(b * s // rows,),
        in_specs=[pl.BlockSpec((rows, 1), lambda i: (i, 0)),
                  pl.BlockSpec((1, LANES), lambda i: (0, 0))],
        out_specs=[pl.BlockSpec((rows, LANES), lambda i: (i, 0)),
                   pl.BlockSpec((rows, LANES), lambda i: (i, 0))],
        out_shape=[jax.ShapeDtypeStruct((b * s, LANES), F32)] * 2,
        name="rope_tables",
    )(pos, invf)


def _in_proj_kernel(x_ref, g_ref, w_ref, o_ref):
    x = x_ref[...]
    ms = jnp.mean(x * x, axis=-1, keepdims=True)
    h = (x * lax.rsqrt(ms + EPS) * g_ref[...]).astype(BF16)
    for j in range(IN_WIDTH // PROJ_COLS):
        cols = slice(j * PROJ_COLS, (j + 1) * PROJ_COLS)
        y = _dot(h, w_ref[:, cols])
        if cols.start < SB_WIDTH:
            col_id = lax.broadcasted_iota(jnp.int32, (1, PROJ_COLS), 1) + cols.start
            y = y * jnp.where(col_id < SB_WIDTH, SB_Q_SCALE, 1.0)
        o_ref[:, cols] = y.astype(BF16)


def _in_proj(x2, g, w):
    m = x2.shape[0]
    return pl.pallas_call(
        _in_proj_kernel,
        grid=(m // PROJ_ROWS,),
        in_specs=[pl.BlockSpec((PROJ_ROWS, D_MODEL), lambda i: (i, 0)),
                  pl.BlockSpec((1, D_MODEL), lambda i: (0, 0)),
                  pl.BlockSpec((D_MODEL, IN_WIDTH), lambda i: (0, 0))],
        out_specs=pl.BlockSpec((PROJ_ROWS, IN_WIDTH), lambda i: (i, 0)),
        out_shape=jax.ShapeDtypeStruct((m, IN_WIDTH), BF16),
        compiler_params=pltpu.CompilerParams(vmem_limit_bytes=VMEM_LIMIT),
        name="in_proj",
    )(x2, g, w)


SB_LAG = 2
SB_MASKED = -1e30
SB_DEAD = 160.0
SB_BIAS_NONE, SB_BIAS_DIAG, SB_BIAS_ALL = 0, 1, 2


def _sb_schedule(nq):
    zero_slot, junk_slot = nq, nq + 1
    idle = (0, 0, SB_BIAS_ALL, junk_slot, junk_slot, 1)
    items = [idle] * SB_LAG
    waves = []
    for w in range(nq):
        start = len(items)
        for qb in range(nq - 1, w - 1, -1):
            src = zero_slot if w == 0 else qb
            bias = SB_BIAS_DIAG if w == 0 else SB_BIAS_NONE
            items.append((qb, qb - w, bias, src, qb, 0 if qb > w else 1))
        if (len(items) - start) % 2:
            items.append(idle)
        waves.append((start, (len(items) - start) // 2))
    return jnp.asarray(items, jnp.int32).T, jnp.asarray(waves, jnp.int32).T


def _sb_kernel(tab_ref, wave_ref, q_ref, k_ref, v_ref, g_ref, o_ref,
               tri_ref, bias_ref, z_ref, sp_ref, w_ref, r_ref, m_ref, acc_ref, *, n_waves):
    t = SB_BLOCK
    first, second = _head_masks()
    row = lax.broadcasted_iota(jnp.int32, (t, t), 0)
    col = lax.broadcasted_iota(jnp.int32, (t, t), 1)
    tri_ref[...] = (row >= col).astype(BF16)
    bias_ref[SB_BIAS_NONE] = jnp.zeros((t, t), F32)
    bias_ref[SB_BIAS_DIAG] = jnp.where(col < row, 0.0, SB_MASKED)
    bias_ref[SB_BIAS_ALL] = jnp.full((t, t), SB_MASKED, F32)
    z_ref[...] = jnp.full(z_ref.shape, SB_MASKED, F32)
    sp_ref[...] = jnp.zeros_like(sp_ref)
    w_ref[...] = jnp.zeros_like(w_ref)
    r_ref[...] = jnp.zeros_like(r_ref)
    m_ref[...] = jnp.zeros_like(m_ref)
    acc_ref[...] = jnp.zeros_like(acc_ref)

    def rows(blk):
        return pl.ds(pl.multiple_of(blk * t, t), t)

    def scores(j, slot):
        qb, kb = tab_ref[0, j], tab_ref[1, j]
        q = q_ref[rows(qb), :]
        q2 = jnp.concatenate([jnp.where(first, q, 0), jnp.where(second, q, 0)], axis=0)
        bias = bias_ref[tab_ref[2, j]]
        z = _dot_nt(q2, k_ref[rows(kb), :]) + jnp.concatenate([bias, bias], axis=0)
        sp = jnp.maximum(z, 0.0) + jnp.log2(1.0 + jnp.exp2(-jnp.abs(z)))
        z_ref[slot] = z
        sp_ref[slot] = sp.astype(BF16)

    def weights(j, slot):
        cum = _dot(sp_ref[slot], tri_ref[...])
        r = r_ref[tab_ref[3, j]]
        z = z_ref[slot]
        halves = [jnp.exp2(z[:, c:c + LANES] - cum[:, c:c + LANES] - r) for c in (0, LANES)]
        w_ref[slot] = jnp.concatenate(halves, axis=1).astype(BF16)
        r_new = r + cum[:, 0:1]
        r_ref[tab_ref[4, j]] = r_new
        m_slot = tab_ref[5, j]
        m_ref[m_slot] = jnp.minimum(m_ref[m_slot], r_new)

    def values(j, slot):
        qb, kb = tab_ref[0, j], tab_ref[1, j]
        pv = _dot(w_ref[slot], v_ref[rows(kb), :])
        acc_ref[rows(qb), :] += jnp.where(first, pv[:t], pv[t:])

    def wave(state):
        w = state[0]
        start, n_steps = wave_ref[0, w], wave_ref[1, w]
        m_ref[0] = jnp.full(m_ref.shape[1:], jnp.inf, F32)

        def step(n, _):
            for parity in (0, 1):
                c = start + 2 * n + parity
                values(c - 2, parity)
                weights(c - 1, 1 - parity)
                scores(c, parity)
            return 0

        lax.fori_loop(0, n_steps, step, 0)
        return w + 1, start + 2 * n_steps, jnp.min(m_ref[0])

    _, end, _ = lax.while_loop(lambda s: jnp.logical_and(s[0] < n_waves, s[2] < SB_DEAD), wave,
                               (jnp.int32(0), jnp.int32(SB_LAG), jnp.float32(0.0)))
    values(end - 2, 0)
    weights(end - 1, 1)
    values(end - 1, 1)
    o_ref[...] = (acc_ref[...] * _silu(g_ref[...].astype(F32))).astype(BF16)


def _sb_attention(proj, b, s):
    nq = s // SB_BLOCK
    table, waves = _sb_schedule(nq)
    t = SB_BLOCK
    seq = lambda col: pl.BlockSpec((s, LANES), lambda bi, p, tab, wav: (bi, col + p))
    return pl.pallas_call(
        functools.partial(_sb_kernel, n_waves=nq),
        grid_spec=pltpu.PrefetchScalarGridSpec(
            num_scalar_prefetch=2,
            grid=(b, SB_PAIRS),
            in_specs=[seq(COL_SB_Q), seq(COL_SB_K), seq(COL_SB_V), seq(COL_SB_G)],
            out_specs=pl.BlockSpec((s, LANES), lambda bi, p, tab, wav: (bi, p)),
            scratch_shapes=[pltpu.VMEM((t, t), BF16),
                            pltpu.VMEM((3, t, t), F32),
                            pltpu.VMEM((2, 2 * t, t), F32),
                            pltpu.VMEM((2, 2 * t, t), BF16),
                            pltpu.VMEM((2, 2 * t, t), BF16),
                            pltpu.VMEM((nq + 2, 2 * t, LANES), F32),
                            pltpu.VMEM((2, 2 * t, LANES), F32),
                            pltpu.VMEM((s, LANES), F32)]),
        out_shape=jax.ShapeDtypeStruct((b * s, SB_WIDTH), BF16),
        compiler_params=pltpu.CompilerParams(vmem_limit_bytes=VMEM_LIMIT),
        name="sb_attention",
    )(table, waves, proj, proj, proj, proj)


def _swap_halves(t):
    lane = lax.broadcasted_iota(jnp.int32, (1, LANES), 1)
    half = HEAD_DIM // 2
    lower = (lane % HEAD_DIM) < half
    return jnp.where(lower, pltpu.roll(t, LANES - half, 1), pltpu.roll(t, half, 1))


def _ret_kernel(q_ref, k_ref, v_ref, g_ref, cos_ref, sin_ref, ng_ref, lg_ref, o_ref, state_ref):
    c = RET_CHUNK
    first, second = _head_masks()
    nchunks = q_ref.shape[0] // c

    lg_lane = lg_ref[0]
    lg_a = lg_lane[:, 0:1]
    lg_b = lg_lane[:, HEAD_DIM:HEAD_DIM + 1]

    ri = lax.broadcasted_iota(jnp.int32, (c, c), 0)
    ci = lax.broadcasted_iota(jnp.int32, (c, c), 1)
    diff = (ri - ci).astype(F32)
    keep = ri >= ci
    decay_a = jnp.where(keep, jnp.exp(lg_a * jnp.maximum(diff, 0.0)), 0.0)
    decay_b = jnp.where(keep, jnp.exp(lg_b * jnp.maximum(diff, 0.0)), 0.0)
    idx = lax.broadcasted_iota(jnp.int32, (c, LANES), 0).astype(F32)
    q_decay = jnp.exp(lg_lane * (idx + 1.0))
    k_decay = jnp.exp(lg_lane * (c - 1.0 - idx))
    si = lax.broadcasted_iota(jnp.int32, (LANES, LANES), 0)
    sj = lax.broadcasted_iota(jnp.int32, (LANES, LANES), 1)
    same_head = (si < HEAD_DIM) == (sj < HEAD_DIM)
    state_decay = jnp.where(same_head, jnp.exp(lg_lane * float(c)), 0.0)
    norm_g = ng_ref[...]

    state_ref[...] = jnp.zeros_like(state_ref)

    def chunk(n, _):
        rows = pl.ds(pl.multiple_of(n * c, c), c)
        cos = cos_ref[rows, :]
        sin = sin_ref[rows, :]
        qf = q_ref[rows, :].astype(F32)
        kf = k_ref[rows, :].astype(F32)
        qr = qf * cos + _swap_halves(qf) * sin
        kr = (kf * cos + _swap_halves(kf) * sin) * QK_SCALE
        qb = qr.astype(BF16)
        kb = kr.astype(BF16)
        v = v_ref[rows, :]
        state = state_ref[...]

        out = _dot(qb, state.astype(BF16)) * q_decay
        for mask, decay in ((first, decay_a), (second, decay_b)):
            scores = _dot_nt(jnp.where(mask, qb, 0), kb) * decay
            out = out + _dot(scores.astype(BF16), jnp.where(mask, v, 0))
        kd = (kr * k_decay).astype(BF16)
        state_ref[...] = state * state_decay + jnp.where(same_head, _dot_tn(kd, v), 0.0)

        inv = lax.rsqrt(_pair_mean_square(out, first) + EPS)
        y = out * inv * norm_g
        o_ref[rows, :] = (y * _silu(g_ref[rows, :].astype(F32))).astype(BF16)
        return 0

    lax.fori_loop(0, nchunks, chunk, 0)


def _retention(proj, cos, sin, ret_norm_g, b, s):
    seq = lambda col: pl.BlockSpec((s, LANES), lambda bi, p: (bi, col + p))
    tab = pl.BlockSpec((s, LANES), lambda bi, p: (bi, 0))
    gamma = 1.0 - 2.0 ** (-5.0 - jnp.arange(RET_HEADS, dtype=F32))
    log_gamma = jnp.repeat(jnp.log(gamma), HEAD_DIM).reshape(RET_PAIRS, 1, LANES)
    return pl.pallas_call(
        _ret_kernel,
        grid=(b, RET_PAIRS),
        in_specs=[seq(COL_RET_Q), seq(COL_RET_K), seq(COL_RET_V), seq(COL_RET_G), tab, tab,
                  pl.BlockSpec((1, LANES), lambda bi, p: (0, p)),
                  pl.BlockSpec((1, 1, LANES), lambda bi, p: (p, 0, 0))],
        out_specs=pl.BlockSpec((s, LANES), lambda bi, p: (bi, p)),
        out_shape=jax.ShapeDtypeStruct((b * s, RET_WIDTH), BF16),
        scratch_shapes=[pltpu.VMEM((LANES, LANES), F32)],
        compiler_params=pltpu.CompilerParams(vmem_limit_bytes=VMEM_LIMIT),
        name="retention",
    )(proj, proj, proj, proj, cos, sin, ret_norm_g, log_gamma)


def _mem_kv_kernel(mem_ref, g_ref, w_ref, kg_ref, k_ref, v_ref):
    first, _ = _head_masks()
    x = mem_ref[...]
    ms = jnp.mean(x * x, axis=-1, keepdims=True)
    h = (x * lax.rsqrt(ms + EPS) * g_ref[0]).astype(BF16)
    kv = _dot(h, w_ref[0])
    kg = kg_ref[0]
    for pair in range(MEM_PAIRS):
        cols = slice(pair * LANES, (pair + 1) * LANES)
        kp = kv[:, cols]
        kn = kp * lax.rsqrt(_pair_mean_square(kp, first) + EPS) * kg
        k_ref[0, :, cols] = kn.astype(BF16)
    v_ref[0] = kv[:, MEM_WIDTH:].astype(BF16)


def _mem_kv(mem2, mem_norm_g, w_mem_kv, k_norm_g, b, tokens):
    depth = w_mem_kv.shape[0]
    out = jax.ShapeDtypeStruct((depth, b * tokens, MEM_WIDTH), BF16)
    return pl.pallas_call(
        _mem_kv_kernel,
        grid=(depth, b),
        in_specs=[pl.BlockSpec((tokens, D_MODEL), lambda l, bi: (bi, 0)),
                  pl.BlockSpec((1, 1, D_MODEL), lambda l, bi: (l, 0, 0)),
                  pl.BlockSpec((1, D_MODEL, 2 * MEM_WIDTH), lambda l, bi: (l, 0, 0)),
                  pl.BlockSpec((1, 1, LANES), lambda l, bi: (l, 0, 0))],
        out_specs=[pl.BlockSpec((1, tokens, MEM_WIDTH), lambda l, bi: (l, bi, 0)),
                   pl.BlockSpec((1, tokens, MEM_WIDTH), lambda l, bi: (l, bi, 0))],
        out_shape=[out, out],
        compiler_params=pltpu.CompilerParams(vmem_limit_bytes=VMEM_LIMIT),
        name="mem_kv",
    )(mem2, mem_norm_g, w_mem_kv, k_norm_g)


def _mem_attn_kernel(q_ref, g_ref, k_ref, v_ref, qg_ref, o_ref):
    first, second = _head_masks()
    q = q_ref[...].astype(F32)
    qn = (q * lax.rsqrt(_pair_mean_square(q, first) + EPS) * qg_ref[...]).astype(BF16)
    k = k_ref[...]
    v = v_ref[...]
    out = jnp.zeros(q.shape, F32)
    for mask in (first, second):
        sc = _dot_nt(jnp.where(mask, qn, 0), k) * QK_SCALE
        e = jnp.exp(sc - jnp.max(sc, axis=-1, keepdims=True))
        denom = jnp.sum(e, axis=-1, keepdims=True)
        out = out + _dot(e.astype(BF16), jnp.where(mask, v, 0)) / denom
    o_ref[...] = (out * _silu(g_ref[...].astype(F32))).astype(BF16)


def _mem_attention(proj, mk, mv, q_norm_g, b, s, tokens):
    nq = s // MEM_ROWS
    kv_spec = pl.BlockSpec((tokens, LANES), lambda bi, p, i: (bi, p))
    return pl.pallas_call(
        _mem_attn_kernel,
        grid=(b, MEM_PAIRS, nq),
        in_specs=[pl.BlockSpec((MEM_ROWS, LANES), lambda bi, p, i: (bi * nq + i, COL_MEM_Q + p)),
                  pl.BlockSpec((MEM_ROWS, LANES), lambda bi, p, i: (bi * nq + i, COL_MEM_G + p)),
                  kv_spec, kv_spec,
                  pl.BlockSpec((1, LANES), lambda bi, p, i: (0, 0))],
        out_specs=pl.BlockSpec((MEM_ROWS, LANES), lambda bi, p, i: (bi * nq + i, p)),
        out_shape=jax.ShapeDtypeStruct((b * s, MEM_WIDTH), BF16),
        compiler_params=pltpu.CompilerParams(vmem_limit_bytes=VMEM_LIMIT),
        name="mem_attention",
    )(proj, proj, mk, mv, q_norm_g)


def _out_proj_kernel(sb_ref, ret_ref, mem_ref, w_ref, x_ref, o_ref):
    y = _dot(sb_ref[...], w_ref[0:SB_WIDTH, :])
    y = y + _dot(ret_ref[...], w_ref[SB_WIDTH:SB_WIDTH + RET_WIDTH, :])
    y = y + _dot(mem_ref[...], w_ref[SB_WIDTH + RET_WIDTH:MIX_WIDTH, :])
    o_ref[...] = x_ref[...] + y


def _out_proj(sb_o, ret_o, mem_o, w, x2):
    m = x2.shape[0]
    rows = lambda width: pl.BlockSpec((PROJ_ROWS, width), lambda i: (i, 0))
    return pl.pallas_call(
        _out_proj_kernel,
        grid=(m // PROJ_ROWS,),
        in_specs=[rows(SB_WIDTH), rows(RET_WIDTH), rows(MEM_WIDTH),
                  pl.BlockSpec((MIX_WIDTH, D_MODEL), lambda i: (0, 0)),
                  rows(D_MODEL)],
        out_specs=rows(D_MODEL),
        out_shape=jax.ShapeDtypeStruct((m, D_MODEL), F32),
        compiler_params=pltpu.CompilerParams(vmem_limit_bytes=VMEM_LIMIT),
        name="out_proj",
    )(sb_o, ret_o, mem_o, w, x2)


def kernel(x, mem, positions, norm_g, w_in, w_out, mem_norm_g, w_mem_kv, mem_q_norm_g,
           mem_k_norm_g, ret_norm_g):
    b, s, d = x.shape
    tokens = mem.shape[1]
    depth = w_in.shape[0]
    assert d == D_MODEL and w_in.shape[2] == IN_WIDTH
    assert s % SB_BLOCK == 0 and s % RET_CHUNK == 0 and s % MEM_ROWS == 0
    assert (b * s) % PROJ_ROWS == 0

    w_in_b = w_in.astype(BF16)
    w_out_b = w_out.astype(BF16)
    w_kv_b = w_mem_kv.astype(BF16)
    q_norm_g2 = jnp.tile(mem_q_norm_g, (1, PAIR))
    k_norm_g3 = jnp.tile(mem_k_norm_g, (1, PAIR))[:, None, :]

    cos, sin = _rope_tables(positions)
    mk_all, mv_all = _mem_kv(mem.reshape(b * tokens, d), mem_norm_g[:, None, :], w_kv_b,
                             k_norm_g3, b, tokens)

    x2 = x.reshape(b * s, d)
    for l in range(depth):
        proj = _in_proj(x2, norm_g[l][None, :], w_in_b[l])
        sb_o = _sb_attention(proj, b, s)
        ret_o = _retention(proj, cos, sin, ret_norm_g[l][None, :], b, s)
        mem_o = _mem_attention(proj, mk_all[l], mv_all[l], q_norm_g2[l][None, :], b, s, tokens)
        x2 = _out_proj(sb_o, ret_o, mem_o, w_out_b[l], x2)
    return x2.reshape(b, s, d)
```

```python
import functools
import math

import jax
import jax.numpy as jnp
from jax import lax
from jax.experimental import pallas as pl
from jax.experimental.pallas import tpu as pltpu

D_MODEL = 1024
HEAD_DIM = 64
SB_HEADS = 6
RET_HEADS = 6
MEM_HEADS = 4
SB_WIDTH = SB_HEADS * HEAD_DIM
RET_WIDTH = RET_HEADS * HEAD_DIM
MEM_WIDTH = MEM_HEADS * HEAD_DIM
MIX_WIDTH = SB_WIDTH + RET_WIDTH + MEM_WIDTH
IN_WIDTH = 4 * SB_WIDTH + 4 * RET_WIDTH + 2 * MEM_WIDTH
ROPE_BASE = 10000.0
EPS = 1e-6
QK_SCALE = HEAD_DIM ** -0.5
SB_Q_SCALE = QK_SCALE * math.log2(math.e)

LANES = 128
PAIR = LANES // HEAD_DIM
SB_PAIRS = SB_HEADS // PAIR
RET_PAIRS = RET_HEADS // PAIR
MEM_PAIRS = MEM_HEADS // PAIR

COL_SB_Q = 0
COL_SB_K = COL_SB_Q + SB_PAIRS
COL_SB_V = COL_SB_K + SB_PAIRS
COL_SB_G = COL_SB_V + SB_PAIRS
COL_RET_Q = COL_SB_G + SB_PAIRS
COL_RET_K = COL_RET_Q + RET_PAIRS
COL_RET_V = COL_RET_K + RET_PAIRS
COL_RET_G = COL_RET_V + RET_PAIRS
COL_MEM_Q = COL_RET_G + RET_PAIRS
COL_MEM_G = COL_MEM_Q + MEM_PAIRS

PROJ_ROWS = 512
PROJ_COLS = 512
SB_BLOCK = 256
RET_CHUNK = 256
MEM_ROWS = 512
VMEM_LIMIT = 48 * 1024 * 1024

F32 = jnp.float32
BF16 = jnp.bfloat16


def _dot(a, b):
    return jnp.dot(a, b, preferred_element_type=F32)


def _dot_nt(a, b):
    return lax.dot_general(a, b, (((1,), (1,)), ((), ())), preferred_element_type=F32)


def _dot_tn(a, b):
    return lax.dot_general(a, b, (((0,), (0,)), ((), ())), preferred_element_type=F32)


def _silu(g):
    return g / (1.0 + jnp.exp(-g))


def _head_masks():
    lane = lax.broadcasted_iota(jnp.int32, (1, LANES), 1)
    first = lane < HEAD_DIM
    return first, jnp.logical_not(first)


def _pair_mean_square(t, first):
    sq = t * t
    s_a = jnp.sum(jnp.where(first, sq, 0.0), axis=-1, keepdims=True)
    s_b = jnp.sum(jnp.where(first, 0.0, sq), axis=-1, keepdims=True)
    return jnp.where(first, s_a, s_b) * (1.0 / HEAD_DIM)


def _rope_kernel(pos_ref, invf_ref, cos_ref, sin_ref):
    ang = pos_ref[...].astype(F32) * invf_ref[...]
    lane = lax.broadcasted_iota(jnp.int32, (1, LANES), 1)
    sign = jnp.where((lane % HEAD_DIM) < HEAD_DIM // 2, -1.0, 1.0)
    cos_ref[...] = jnp.cos(ang)
    sin_ref[...] = jnp.sin(ang) * sign


def _rope_tables(positions):
    b, s = positions.shape
    half = HEAD_DIM // 2
    inv_freq = ROPE_BASE ** (-jnp.arange(half, dtype=F32) / half)
    invf = jnp.tile(inv_freq, LANES // half)[None, :]
    pos = positions.reshape(b * s, 1)
    rows = s
    return pl.pallas_call(
        _rope_kernel,
        grid=(b * s // rows,),
        in_specs=[pl.BlockSpec((rows, 1), lambda i: (i, 0)),
                  pl.BlockSpec((1, LANES), lambda i: (0, 0))],
        out_specs=[pl.BlockSpec((rows, LANES), lambda i: (i, 0)),
                   pl.BlockSpec((rows, LANES), lambda i: (i, 0))],
        out_shape=[jax.ShapeDtypeStruct((b * s, LANES), F32)] * 2,
        name="rope_tables",
    )(pos, invf)


def _swap_halves(t):
    lane = lax.broadcasted_iota(jnp.int32, (1, LANES), 1)
    half = HEAD_DIM // 2
    lower = (lane % HEAD_DIM) < half
    return jnp.where(lower, pltpu.roll(t, LANES - half, 1), pltpu.roll(t, half, 1))


def _in_proj_kernel(x_ref, g_ref, w_ref, cos_ref, sin_ref, o_ref):
    x = x_ref[...]
    ms = jnp.mean(x * x, axis=-1, keepdims=True)
    h = (x * lax.rsqrt(ms + EPS) * g_ref[...]).astype(BF16)
    cos = cos_ref[...]
    sin = sin_ref[...]
    blocks_per_dot = PROJ_COLS // LANES
    for j in range(IN_WIDTH // PROJ_COLS):
        y = _dot(h, w_ref[:, j * PROJ_COLS:(j + 1) * PROJ_COLS])
        for b in range(blocks_per_dot):
            blk = j * blocks_per_dot + b
            piece = y[:, b * LANES:(b + 1) * LANES]
            if blk < COL_SB_K:
                piece = piece * SB_Q_SCALE
            elif COL_RET_Q <= blk < COL_RET_V:
                piece = piece * cos + _swap_halves(piece) * sin
                if blk >= COL_RET_K:
                    piece = piece * QK_SCALE
            o_ref[:, blk * LANES:(blk + 1) * LANES] = piece.astype(BF16)


def _in_proj(x2, g, w, cos, sin):
    m = x2.shape[0]
    rows = lambda width: pl.BlockSpec((PROJ_ROWS, width), lambda i: (i, 0))
    return pl.pallas_call(
        _in_proj_kernel,
        grid=(m // PROJ_ROWS,),
        in_specs=[rows(D_MODEL),
                  pl.BlockSpec((1, D_MODEL), lambda i: (0, 0)),
                  pl.BlockSpec((D_MODEL, IN_WIDTH), lambda i: (0, 0)),
                  rows(LANES), rows(LANES)],
        out_specs=rows(IN_WIDTH),
        out_shape=jax.ShapeDtypeStruct((m, IN_WIDTH), BF16),
        compiler_params=pltpu.CompilerParams(vmem_limit_bytes=VMEM_LIMIT),
        name="in_proj",
    )(x2, g, w, cos, sin)


SB_LAG = 2
SB_MASKED = -1e30
SB_DEAD = 160.0
SB_BIAS_NONE, SB_BIAS_DIAG, SB_BIAS_ALL = 0, 1, 2


def _sb_schedule(nq):
    zero_slot, junk_slot = nq, nq + 1
    idle = (0, 0, SB_BIAS_ALL, junk_slot, junk_slot, 1)
    items = [idle] * SB_LAG
    waves = []
    for w in range(nq):
        start = len(items)
        for qb in range(nq - 1, w - 1, -1):
            src = zero_slot if w == 0 else qb
            bias = SB_BIAS_DIAG if w == 0 else SB_BIAS_NONE
            items.append((qb, qb - w, bias, src, qb, 0 if qb > w else 1))
        if (len(items) - start) % 2:
            items.append(idle)
        waves.append((start, (len(items) - start) // 2))
    return jnp.asarray(items, jnp.int32).T, jnp.asarray(waves, jnp.int32).T


def _sb_kernel(tab_ref, wave_ref, q_ref, k_ref, v_ref, g_ref, o_ref,
               tri_ref, bias_ref, z_ref, sp_ref, w_ref, r_ref, m_ref, acc_ref, *, n_waves):
    t = SB_BLOCK
    first, second = _head_masks()
    n_carry = r_ref.shape[0]

    @pl.when(jnp.logical_and(pl.program_id(0) == 0, pl.program_id(1) == 0))
    def _constants():
        row = lax.broadcasted_iota(jnp.int32, (t, t), 0)
        col = lax.broadcasted_iota(jnp.int32, (t, t), 1)
        tri_ref[...] = (row >= col).astype(BF16)
        bias_ref[SB_BIAS_NONE] = jnp.zeros((t, t), F32)
        bias_ref[SB_BIAS_DIAG] = jnp.where(col < row, 0.0, SB_MASKED)
        bias_ref[SB_BIAS_ALL] = jnp.full((t, t), SB_MASKED, F32)
        r_ref[n_carry - 2] = jnp.zeros(r_ref.shape[1:], F32)

    z_ref[...] = jnp.full(z_ref.shape, SB_MASKED, F32)
    sp_ref[...] = jnp.zeros_like(sp_ref)
    w_ref[...] = jnp.zeros_like(w_ref)
    r_ref[n_carry - 1] = jnp.zeros(r_ref.shape[1:], F32)
    m_ref[1] = jnp.zeros(m_ref.shape[1:], F32)
    acc_ref[...] = jnp.zeros_like(acc_ref)

    def rows(blk):
        return pl.ds(pl.multiple_of(blk * t, t), t)

    def scores(j, slot):
        qb, kb = tab_ref[0, j], tab_ref[1, j]
        q = q_ref[rows(qb), :]
        q2 = jnp.concatenate([jnp.where(first, q, 0), jnp.where(second, q, 0)], axis=0)
        bias = bias_ref[tab_ref[2, j]]
        z = _dot_nt(q2, k_ref[rows(kb), :]) + jnp.concatenate([bias, bias], axis=0)
        sp = jnp.maximum(z, 0.0) + jnp.log2(1.0 + jnp.exp2(-jnp.abs(z)))
        z_ref[slot] = z
        sp_ref[slot] = sp.astype(BF16)

    def weights(j, slot):
        cum = _dot(sp_ref[slot], tri_ref[...])
        r = r_ref[tab_ref[3, j]]
        z = z_ref[slot]
        halves = [jnp.exp2(z[:, c:c + LANES] - cum[:, c:c + LANES] - r) for c in (0, LANES)]
        w_ref[slot] = jnp.concatenate(halves, axis=1).astype(BF16)
        r_new = r + cum[:, 0:1]
        r_ref[tab_ref[4, j]] = r_new
        m_slot = tab_ref[5, j]
        m_ref[m_slot] = jnp.minimum(m_ref[m_slot], r_new)

    def values(j, slot):
        qb, kb = tab_ref[0, j], tab_ref[1, j]
        pv = _dot(w_ref[slot], v_ref[rows(kb), :])
        acc_ref[rows(qb), :] += jnp.where(first, pv[:t], pv[t:])

    def wave(state):
        w = state[0]
        start, n_steps = wave_ref[0, w], wave_ref[1, w]
        m_ref[0] = jnp.full(m_ref.shape[1:], jnp.inf, F32)

        def step(n, _):
            for parity in (0, 1):
                c = start + 2 * n + parity
                values(c - 2, parity)
                weights(c - 1, 1 - parity)
                scores(c, parity)
            return 0

        lax.fori_loop(0, n_steps, step, 0)
        return w + 1, start + 2 * n_steps, jnp.min(m_ref[0])

    _, end, _ = lax.while_loop(lambda s: jnp.logical_and(s[0] < n_waves, s[2] < SB_DEAD), wave,
                               (jnp.int32(0), jnp.int32(SB_LAG), jnp.float32(0.0)))
    values(end - 2, 0)
    weights(end - 1, 1)
    values(end - 1, 1)
    o_ref[...] = (acc_ref[...] * _silu(g_ref[...].astype(F32))).astype(BF16)


def _sb_attention(proj, b, s):
    nq = s // SB_BLOCK
    table, waves = _sb_schedule(nq)
    t = SB_BLOCK
    seq = lambda col: pl.BlockSpec((s, LANES), lambda bi, p, tab, wav: (bi, col + p))
    return pl.pallas_call(
        functools.partial(_sb_kernel, n_waves=nq),
        grid_spec=pltpu.PrefetchScalarGridSpec(
            num_scalar_prefetch=2,
            grid=(b, SB_PAIRS),
            in_specs=[seq(COL_SB_Q), seq(COL_SB_K), seq(COL_SB_V), seq(COL_SB_G)],
            out_specs=pl.BlockSpec((s, LANES), lambda bi, p, tab, wav: (bi, p)),
            scratch_shapes=[pltpu.VMEM((t, t), BF16),
                            pltpu.VMEM((3, t, t), F32),
                            pltpu.VMEM((2, 2 * t, t), F32),
                            pltpu.VMEM((2, 2 * t, t), BF16),
                            pltpu.VMEM((2, 2 * t, t), BF16),
                            pltpu.VMEM((nq + 2, 2 * t, LANES), F32),
                            pltpu.VMEM((2, 2 * t, LANES), F32),
                            pltpu.VMEM((s, LANES), F32)]),
        out_shape=jax.ShapeDtypeStruct((b * s, SB_WIDTH), BF16),
        compiler_params=pltpu.CompilerParams(
            dimension_semantics=("arbitrary", "arbitrary"),
            vmem_limit_bytes=VMEM_LIMIT),
        name="sb_attention",
    )(table, waves, proj, proj, proj, proj)


def _ret_kernel(q_ref, k_ref, v_ref, g_ref, ng_ref, lg_ref, o_ref, kv_ref, state_ref):
    c = RET_CHUNK
    first, second = _head_masks()
    nchunks = q_ref.shape[0] // c

    lg_lane = lg_ref[0]
    lg_a = lg_lane[:, 0:1]
    lg_b = lg_lane[:, HEAD_DIM:HEAD_DIM + 1]

    ri = lax.broadcasted_iota(jnp.int32, (c, c), 0)
    ci = lax.broadcasted_iota(jnp.int32, (c, c), 1)
    diff = (ri - ci).astype(F32)
    keep = ri >= ci
    decay_a = jnp.where(keep, jnp.exp(lg_a * jnp.maximum(diff, 0.0)), 0.0)
    decay_b = jnp.where(keep, jnp.exp(lg_b * jnp.maximum(diff, 0.0)), 0.0)
    idx = lax.broadcasted_iota(jnp.int32, (c, LANES), 0).astype(F32)
    q_decay = jnp.exp(lg_lane * (idx + 1.0))
    k_decay = jnp.exp(lg_lane * (c - 1.0 - idx))
    si = lax.broadcasted_iota(jnp.int32, (LANES, LANES), 0)
    sj = lax.broadcasted_iota(jnp.int32, (LANES, LANES), 1)
    same_head = (si < HEAD_DIM) == (sj < HEAD_DIM)
    state_decay = jnp.where(same_head, jnp.exp(lg_lane * float(c)), 0.0)
    norm_g = ng_ref[...]

    chunk_rows = [slice(n * c, (n + 1) * c) for n in range(nchunks)]

    for n, rows in enumerate(chunk_rows):
        kd = (k_ref[rows, :].astype(F32) * k_decay).astype(BF16)
        kv_ref[n] = jnp.where(same_head, _dot_tn(kd, v_ref[rows, :]), 0.0)

    state = jnp.zeros((LANES, LANES), F32)
    for n in range(nchunks):
        state_ref[n] = state.astype(BF16)
        state = state * state_decay + kv_ref[n]

    for n, rows in enumerate(chunk_rows):
        qb = q_ref[rows, :]
        kb = k_ref[rows, :]
        v = v_ref[rows, :]
        out = _dot(qb, state_ref[n]) * q_decay
        for mask, decay in ((first, decay_a), (second, decay_b)):
            scores = _dot_nt(jnp.where(mask, qb, 0), kb) * decay
            out = out + _dot(scores.astype(BF16), jnp.where(mask, v, 0))
        inv = lax.rsqrt(_pair_mean_square(out, first) + EPS)
        y = out * inv * norm_g
        o_ref[rows, :] = (y * _silu(g_ref[rows, :].astype(F32))).astype(BF16)


def _retention(proj, ret_norm_g, b, s):
    seq = lambda col: pl.BlockSpec((s, LANES), lambda bi, p: (bi, col + p))
    gamma = 1.0 - 2.0 ** (-5.0 - jnp.arange(RET_HEADS, dtype=F32))
    log_gamma = jnp.repeat(jnp.log(gamma), HEAD_DIM).reshape(RET_PAIRS, 1, LANES)
    return pl.pallas_call(
        _ret_kernel,
        grid=(b, RET_PAIRS),
        in_specs=[seq(COL_RET_Q), seq(COL_RET_K), seq(COL_RET_V), seq(COL_RET_G),
                  pl.BlockSpec((1, LANES), lambda bi, p: (0, p)),
                  pl.BlockSpec((1, 1, LANES), lambda bi, p: (p, 0, 0))],
        out_specs=pl.BlockSpec((s, LANES), lambda bi, p: (bi, p)),
        out_shape=jax.ShapeDtypeStruct((b * s, RET_WIDTH), BF16),
        scratch_shapes=[pltpu.VMEM((s // RET_CHUNK, LANES, LANES), F32),
                        pltpu.VMEM((s // RET_CHUNK, LANES, LANES), BF16)],
        compiler_params=pltpu.CompilerParams(vmem_limit_bytes=VMEM_LIMIT),
        name="retention",
    )(proj, proj, proj, proj, ret_norm_g, log_gamma)


def _mem_kv_kernel(mem_ref, g_ref, w_ref, kg_ref, k_ref, v_ref):
    first, _ = _head_masks()
    x = mem_ref[...]
    ms = jnp.mean(x * x, axis=-1, keepdims=True)
    h = (x * lax.rsqrt(ms + EPS) * g_ref[0]).astype(BF16)
    kv = _dot(h, w_ref[0])
    kg = kg_ref[0]
    for pair in range(MEM_PAIRS):
        cols = slice(pair * LANES, (pair + 1) * LANES)
        kp = kv[:, cols]
        kn = kp * lax.rsqrt(_pair_mean_square(kp, first) + EPS) * kg
        k_ref[0, :, cols] = kn.astype(BF16)
    v_ref[0] = kv[:, MEM_WIDTH:].astype(BF16)


def _mem_kv(mem2, mem_norm_g, w_mem_kv, k_norm_g, b, tokens):
    depth = w_mem_kv.shape[0]
    out = jax.ShapeDtypeStruct((depth, b * tokens, MEM_WIDTH), BF16)
    return pl.pallas_call(
        _mem_kv_kernel,
        grid=(depth, b),
        in_specs=[pl.BlockSpec((tokens, D_MODEL), lambda l, bi: (bi, 0)),
                  pl.BlockSpec((1, 1, D_MODEL), lambda l, bi: (l, 0, 0)),
                  pl.BlockSpec((1, D_MODEL, 2 * MEM_WIDTH), lambda l, bi: (l, 0, 0)),
                  pl.BlockSpec((1, 1, LANES), lambda l, bi: (l, 0, 0))],
        out_specs=[pl.BlockSpec((1, tokens, MEM_WIDTH), lambda l, bi: (l, bi, 0)),
                   pl.BlockSpec((1, tokens, MEM_WIDTH), lambda l, bi: (l, bi, 0))],
        out_shape=[out, out],
        compiler_params=pltpu.CompilerParams(vmem_limit_bytes=VMEM_LIMIT),
        name="mem_kv",
    )(mem2, mem_norm_g, w_mem_kv, k_norm_g)


def _mem_attn_kernel(q_ref, g_ref, k_ref, v_ref, qg_ref, o_ref):
    first, second = _head_masks()
    q = q_ref[...].astype(F32)
    qn = (q * lax.rsqrt(_pair_mean_square(q, first) + EPS) * qg_ref[...]).astype(BF16)
    k = k_ref[...]
    v = v_ref[...]
    out = jnp.zeros(q.shape, F32)
    for mask in (first, second):
        sc = _dot_nt(jnp.where(mask, qn, 0), k) * QK_SCALE
        e = jnp.exp(sc - jnp.max(sc, axis=-1, keepdims=True))
        denom = jnp.sum(e, axis=-1, keepdims=True)
        out = out + _dot(e.astype(BF16), jnp.where(mask, v, 0)) / denom
    o_ref[...] = (out * _silu(g_ref[...].astype(F32))).astype(BF16)


def _mem_attention(proj, mk, mv, q_norm_g, b, s, tokens):
    nq = s // MEM_ROWS
    kv_spec = pl.BlockSpec((tokens, LANES), lambda bi, p, i: (bi, p))
    return pl.pallas_call(
        _mem_attn_kernel,
        grid=(b, MEM_PAIRS, nq),
        in_specs=[pl.BlockSpec((MEM_ROWS, LANES), lambda bi, p, i: (bi * nq + i, COL_MEM_Q + p)),
                  pl.BlockSpec((MEM_ROWS, LANES), lambda bi, p, i: (bi * nq + i, COL_MEM_G + p)),
                  kv_spec, kv_spec,
                  pl.BlockSpec((1, LANES), lambda bi, p, i: (0, 0))],
        out_specs=pl.BlockSpec((MEM_ROWS, LANES), lambda bi, p, i: (bi * nq + i, p)),
        out_shape=jax.ShapeDtypeStruct((b * s, MEM_WIDTH), BF16),
        compiler_params=pltpu.CompilerParams(vmem_limit_bytes=VMEM_LIMIT),
        name="mem_attention",
    )(proj, proj, mk, mv, q_norm_g)


def _out_proj_kernel(sb_ref, ret_ref, mem_ref, w_ref, x_ref, o_ref):
    y = _dot(sb_ref[...], w_ref[0:SB_WIDTH, :])
    y = y + _dot(ret_ref[...], w_ref[SB_WIDTH:SB_WIDTH + RET_WIDTH, :])
    y = y + _dot(mem_ref[...], w_ref[SB_WIDTH + RET_WIDTH:MIX_WIDTH, :])
    o_ref[...] = x_ref[...] + y


def _out_proj(sb_o, ret_o, mem_o, w, x2):
    m = x2.shape[0]
    rows = lambda width: pl.BlockSpec((PROJ_ROWS, width), lambda i: (i, 0))
    return pl.pallas_call(
        _out_proj_kernel,
        grid=(m // PROJ_ROWS,),
        in_specs=[rows(SB_WIDTH), rows(RET_WIDTH), rows(MEM_WIDTH),
                  pl.BlockSpec((MIX_WIDTH, D_MODEL), lambda i: (0, 0)),
                  rows(D_MODEL)],
        out_specs=rows(D_MODEL),
        out_shape=jax.ShapeDtypeStruct((m, D_MODEL), F32),
        compiler_params=pltpu.CompilerParams(vmem_limit_bytes=VMEM_LIMIT),
        name="out_proj",
    )(sb_o, ret_o, mem_o, w, x2)


def kernel(x, mem, positions, norm_g, w_in, w_out, mem_norm_g, w_mem_kv, mem_q_norm_g,
           mem_k_norm_g, ret_norm_g):
    b, s, d = x.shape
    tokens = mem.shape[1]
    depth = w_in.shape[0]
    assert d == D_MODEL and w_in.shape[2] == IN_WIDTH
    assert s % SB_BLOCK == 0 and s % RET_CHUNK == 0 and s % MEM_ROWS == 0
    assert (b * s) % PROJ_ROWS == 0

    w_in_b = w_in.astype(BF16)
    w_out_b = w_out.astype(BF16)
    w_kv_b = w_mem_kv.astype(BF16)
    q_norm_g2 = jnp.tile(mem_q_norm_g, (1, PAIR))
    k_norm_g3 = jnp.tile(mem_k_norm_g, (1, PAIR))[:, None, :]

    cos, sin = _rope_tables(positions)
    mk_all, mv_all = _mem_kv(mem.reshape(b * tokens, d), mem_norm_g[:, None, :], w_kv_b,
                             k_norm_g3, b, tokens)

    x2 = x.reshape(b * s, d)
    for l in range(depth):
        proj = _in_proj(x2, norm_g[l][None, :], w_in_b[l], cos, sin)
        sb_o = _sb_attention(proj, b, s)
        ret_o = _retention(proj, ret_norm_g[l][None, :], b, s)
        mem_o = _mem_attention(proj, mk_all[l], mv_all[l], q_norm_g2[l][None, :], b, s, tokens)
        x2 = _out_proj(sb_o, ret_o, mem_o, w_out_b[l], x2)
    return x2.reshape(b, s, d)
```

```python
import functools
import math

import jax
import jax.numpy as jnp
from jax import lax
from jax.experimental import pallas as pl
from jax.experimental.pallas import tpu as pltpu

D_MODEL = 1024
HEAD_DIM = 64
SB_HEADS = 6
RET_HEADS = 6
MEM_HEADS = 4
SB_WIDTH = SB_HEADS * HEAD_DIM
RET_WIDTH = RET_HEADS * HEAD_DIM
MEM_WIDTH = MEM_HEADS * HEAD_DIM
MIX_WIDTH = SB_WIDTH + RET_WIDTH + MEM_WIDTH
IN_WIDTH = 4 * SB_WIDTH + 4 * RET_WIDTH + 2 * MEM_WIDTH
ROPE_BASE = 10000.0
EPS = 1e-6
QK_SCALE = HEAD_DIM ** -0.5
SB_Q_SCALE = QK_SCALE * math.log2(math.e)

LANES = 128
PAIR = LANES // HEAD_DIM
SB_PAIRS = SB_HEADS // PAIR
RET_PAIRS = RET_HEADS // PAIR
MEM_PAIRS = MEM_HEADS // PAIR

COL_SB_Q = 0
COL_SB_K = COL_SB_Q + SB_PAIRS
COL_SB_V = COL_SB_K + SB_PAIRS
COL_SB_G = COL_SB_V + SB_PAIRS
COL_RET_Q = COL_SB_G + SB_PAIRS
COL_RET_K = COL_RET_Q + RET_PAIRS
COL_RET_V = COL_RET_K + RET_PAIRS
COL_RET_G = COL_RET_V + RET_PAIRS
COL_MEM_Q = COL_RET_G + RET_PAIRS
COL_MEM_G = COL_MEM_Q + MEM_PAIRS

PROJ_ROWS = 512
PROJ_COLS = 512
SB_BLOCK = 256
RET_CHUNK = 256
MEM_ROWS = 512
VMEM_LIMIT = 48 * 1024 * 1024

F32 = jnp.float32
BF16 = jnp.bfloat16


def _dot(a, b):
    return jnp.dot(a, b, preferred_element_type=F32)


def _dot_nt(a, b):
    return lax.dot_general(a, b, (((1,), (1,)), ((), ())), preferred_element_type=F32)


def _dot_tn(a, b):
    return lax.dot_general(a, b, (((0,), (0,)), ((), ())), preferred_element_type=F32)


def _silu(g):
    return g / (1.0 + jnp.exp(-g))


def _head_masks():
    lane = lax.broadcasted_iota(jnp.int32, (1, LANES), 1)
    first = lane < HEAD_DIM
    return first, jnp.logical_not(first)


def _pair_mean_square(t, first):
    sq = t * t
    s_a = jnp.sum(jnp.where(first, sq, 0.0), axis=-1, keepdims=True)
    s_b = jnp.sum(jnp.where(first, 0.0, sq), axis=-1, keepdims=True)
    return jnp.where(first, s_a, s_b) * (1.0 / HEAD_DIM)


def _rope_kernel(pos_ref, invf_ref, cos_ref, sin_ref):
    ang = pos_ref[...].astype(F32) * invf_ref[...]
    lane = lax.broadcasted_iota(jnp.int32, (1, LANES), 1)
    sign = jnp.where((lane % HEAD_DIM) < HEAD_DIM // 2, -1.0, 1.0)
    cos_ref[...] = jnp.cos(ang)
    sin_ref[...] = jnp.sin(ang) * sign


def _rope_tables(positions):
    b, s = positions.shape
    half = HEAD_DIM // 2
    inv_freq = ROPE_BASE ** (-jnp.arange(half, dtype=F32) / half)
    invf = jnp.tile(inv_freq, LANES // half)[None, :]
    pos = positions.reshape(b * s, 1)
    rows = s
    return pl.pallas_call(
        _rope_kernel,
        grid=(b * s // rows,),
        in_specs=[pl.BlockSpec((rows, 1), lambda i: (i, 0)),
                  pl.BlockSpec((1, LANES), lambda i: (0, 0))],
        out_specs=[pl.BlockSpec((rows, LANES), lambda i: (i, 0)),
                   pl.BlockSpec((rows, LANES), lambda i: (i, 0))],
        out_shape=[jax.ShapeDtypeStruct((b * s, LANES), F32)] * 2,
        name="rope_tables",
    )(pos, invf)


def _swap_halves(t):
    lane = lax.broadcasted_iota(jnp.int32, (1, LANES), 1)
    half = HEAD_DIM // 2
    lower = (lane % HEAD_DIM) < half
    return jnp.where(lower, pltpu.roll(t, LANES - half, 1), pltpu.roll(t, half, 1))


def _norm_project(x, g_ref, w_ref, cos_ref, sin_ref, o_ref):
    ms = jnp.mean(x * x, axis=-1, keepdims=True)
    h = (x * lax.rsqrt(ms + EPS) * g_ref[...]).astype(BF16)
    cos = cos_ref[...]
    sin = sin_ref[...]
    blocks_per_dot = PROJ_COLS // LANES
    for j in range(IN_WIDTH // PROJ_COLS):
        y = _dot(h, w_ref[:, j * PROJ_COLS:(j + 1) * PROJ_COLS])
        for b in range(blocks_per_dot):
            blk = j * blocks_per_dot + b
            piece = y[:, b * LANES:(b + 1) * LANES]
            if blk < COL_SB_K:
                piece = piece * SB_Q_SCALE
            elif COL_RET_Q <= blk < COL_RET_V:
                piece = piece * cos + _swap_halves(piece) * sin
                if blk >= COL_RET_K:
                    piece = piece * QK_SCALE
            o_ref[:, blk * LANES:(blk + 1) * LANES] = piece.astype(BF16)


def _in_proj_kernel(x_ref, g_ref, w_ref, cos_ref, sin_ref, o_ref):
    _norm_project(x_ref[...], g_ref, w_ref, cos_ref, sin_ref, o_ref)


def _mix(sb_ref, ret_ref, mem_ref, w_ref, x_ref):
    mixed = jnp.concatenate([sb_ref[...], ret_ref[...], mem_ref[...]], axis=1)
    return x_ref[...] + _dot(mixed, w_ref[...])


def _out_proj_kernel(sb_ref, ret_ref, mem_ref, w_ref, x_ref, o_ref):
    o_ref[...] = _mix(sb_ref, ret_ref, mem_ref, w_ref, x_ref)


def _out_in_proj_kernel(sb_ref, ret_ref, mem_ref, w_out_ref, x_ref, g_ref, w_in_ref, cos_ref,
                        sin_ref, x_out_ref, proj_ref):
    x_new = _mix(sb_ref, ret_ref, mem_ref, w_out_ref, x_ref)
    x_out_ref[...] = x_new
    _norm_project(x_new, g_ref, w_in_ref, cos_ref, sin_ref, proj_ref)


def _row_tiles(width):
    return pl.BlockSpec((PROJ_ROWS, width), lambda i: (i, 0))


def _resident(shape):
    return pl.BlockSpec(shape, lambda i: (0, 0), pipeline_mode=pl.Buffered(1))


def _in_proj(x2, g, w, cos, sin):
    m = x2.shape[0]
    return pl.pallas_call(
        _in_proj_kernel,
        grid=(m // PROJ_ROWS,),
        in_specs=[_row_tiles(D_MODEL), _resident((1, D_MODEL)), _resident((D_MODEL, IN_WIDTH)),
                  _row_tiles(LANES), _row_tiles(LANES)],
        out_specs=_row_tiles(IN_WIDTH),
        out_shape=jax.ShapeDtypeStruct((m, IN_WIDTH), BF16),
        compiler_params=pltpu.CompilerParams(vmem_limit_bytes=VMEM_LIMIT),
        name="in_proj",
    )(x2, g, w, cos, sin)


def _out_proj(sb_o, ret_o, mem_o, w, x2):
    m = x2.shape[0]
    return pl.pallas_call(
        _out_proj_kernel,
        grid=(m // PROJ_ROWS,),
        in_specs=[_row_tiles(SB_WIDTH), _row_tiles(RET_WIDTH), _row_tiles(MEM_WIDTH),
                  _resident((MIX_WIDTH, D_MODEL)), _row_tiles(D_MODEL)],
        out_specs=_row_tiles(D_MODEL),
        out_shape=jax.ShapeDtypeStruct((m, D_MODEL), F32),
        compiler_params=pltpu.CompilerParams(vmem_limit_bytes=VMEM_LIMIT),
        name="out_proj",
    )(sb_o, ret_o, mem_o, w, x2)


def _out_in_proj(sb_o, ret_o, mem_o, w_out, x2, g, w_in, cos, sin):
    m = x2.shape[0]
    return pl.pallas_call(
        _out_in_proj_kernel,
        grid=(m // PROJ_ROWS,),
        in_specs=[_row_tiles(SB_WIDTH), _row_tiles(RET_WIDTH), _row_tiles(MEM_WIDTH),
                  _resident((MIX_WIDTH, D_MODEL)), _row_tiles(D_MODEL),
                  _resident((1, D_MODEL)), _resident((D_MODEL, IN_WIDTH)),
                  _row_tiles(LANES), _row_tiles(LANES)],
        out_specs=[_row_tiles(D_MODEL), _row_tiles(IN_WIDTH)],
        out_shape=[jax.ShapeDtypeStruct((m, D_MODEL), F32),
                   jax.ShapeDtypeStruct((m, IN_WIDTH), BF16)],
        compiler_params=pltpu.CompilerParams(vmem_limit_bytes=VMEM_LIMIT),
        name="out_in_proj",
    )(sb_o, ret_o, mem_o, w_out, x2, g, w_in, cos, sin)


SB_LAG = 2
SB_STEP = 4
SB_MASKED = -1e30
SB_DEAD = 160.0
SB_BIAS_NONE, SB_BIAS_DIAG, SB_BIAS_ALL = 0, 1, 2


def _sb_schedule(nq):
    zero_slot, junk_slot = nq, nq + 1
    idle = (0, 0, SB_BIAS_ALL, junk_slot, junk_slot, 1)
    items = [idle] * SB_STEP
    waves = []
    for w in range(nq):
        start = len(items)
        for qb in range(nq - 1, w - 1, -1):
            src = zero_slot if w == 0 else qb
            bias = SB_BIAS_DIAG if w == 0 else SB_BIAS_NONE
            items.append((qb, qb - w, bias, src, qb, 0 if qb > w else 1))
        items += [idle] * (-(len(items) - start) % SB_STEP)
        waves.append((start, (len(items) - start) // SB_STEP))
    return jnp.asarray(items, jnp.int32).T, jnp.asarray(waves, jnp.int32).T


def _sb_kernel(tab_ref, wave_ref, q_ref, k_ref, v_ref, g_ref, o_ref,
               tri_ref, bias_ref, z_ref, sp_ref, w_ref, r_ref, m_ref, acc_ref, *, n_waves):
    t = SB_BLOCK
    first, second = _head_masks()
    n_carry = r_ref.shape[0]

    @pl.when(jnp.logical_and(pl.program_id(0) == 0, pl.program_id(1) == 0))
    def _constants():
        row = lax.broadcasted_iota(jnp.int32, (t, t), 0)
        col = lax.broadcasted_iota(jnp.int32, (t, t), 1)
        tri_ref[...] = (row >= col).astype(BF16)
        bias_ref[SB_BIAS_NONE] = jnp.zeros((t, t), F32)
        bias_ref[SB_BIAS_DIAG] = jnp.where(col < row, 0.0, SB_MASKED)
        bias_ref[SB_BIAS_ALL] = jnp.full((t, t), SB_MASKED, F32)
        r_ref[n_carry - 2] = jnp.zeros(r_ref.shape[1:], F32)

    z_ref[...] = jnp.full(z_ref.shape, SB_MASKED, F32)
    sp_ref[...] = jnp.zeros_like(sp_ref)
    w_ref[...] = jnp.zeros_like(w_ref)
    r_ref[n_carry - 1] = jnp.zeros(r_ref.shape[1:], F32)
    m_ref[1] = jnp.zeros(m_ref.shape[1:], F32)
    acc_ref[...] = jnp.zeros_like(acc_ref)

    def rows(blk):
        return pl.ds(pl.multiple_of(blk * t, t), t)

    def scores(j, slot):
        qb, kb = tab_ref[0, j], tab_ref[1, j]
        q = q_ref[rows(qb), :]
        q2 = jnp.concatenate([jnp.where(first, q, 0), jnp.where(second, q, 0)], axis=0)
        bias = bias_ref[tab_ref[2, j]]
        z = _dot_nt(q2, k_ref[rows(kb), :]) + jnp.concatenate([bias, bias], axis=0)
        sp = jnp.maximum(z, 0.0) + jnp.log2(1.0 + jnp.exp2(-jnp.abs(z)))
        z_ref[slot] = z
        sp_ref[slot] = sp.astype(BF16)

    def weights(j, slot):
        cum = _dot(sp_ref[slot], tri_ref[...])
        r = r_ref[tab_ref[3, j]]
        z = z_ref[slot]
        halves = [jnp.exp2(z[:, c:c + LANES] - cum[:, c:c + LANES] - r) for c in (0, LANES)]
        w_ref[slot] = jnp.concatenate(halves, axis=1).astype(BF16)
        r_new = r + cum[:, 0:1]
        r_ref[tab_ref[4, j]] = r_new
        m_slot = tab_ref[5, j]
        m_ref[m_slot] = jnp.minimum(m_ref[m_slot], r_new)

    def values(j, slot):
        qb, kb = tab_ref[0, j], tab_ref[1, j]
        pv = _dot(w_ref[slot], v_ref[rows(kb), :])
        acc_ref[rows(qb), :] += jnp.where(first, pv[:t], pv[t:])

    def wave(state):
        w = state[0]
        start, n_steps = wave_ref[0, w], wave_ref[1, w]
        m_ref[0] = jnp.full(m_ref.shape[1:], jnp.inf, F32)

        def step(n, _):
            for i in range(SB_STEP):
                c = start + SB_STEP * n + i
                values(c - 2, (i - 2) % SB_STEP)
                weights(c - 1, (i - 1) % SB_STEP)
                scores(c, i)
            return 0

        lax.fori_loop(0, n_steps, step, 0)
        return w + 1, start + SB_STEP * n_steps, jnp.min(m_ref[0])

    _, end, _ = lax.while_loop(lambda s: jnp.logical_and(s[0] < n_waves, s[2] < SB_DEAD), wave,
                               (jnp.int32(0), jnp.int32(SB_STEP), jnp.float32(0.0)))
    values(end - 2, SB_STEP - 2)
    weights(end - 1, SB_STEP - 1)
    values(end - 1, SB_STEP - 1)
    o_ref[...] = (acc_ref[...] * _silu(g_ref[...].astype(F32))).astype(BF16)


def _sb_attention(proj, b, s):
    nq = s // SB_BLOCK
    table, waves = _sb_schedule(nq)
    t = SB_BLOCK
    seq = lambda col: pl.BlockSpec((s, LANES), lambda bi, p, tab, wav: (bi, col + p))
    return pl.pallas_call(
        functools.partial(_sb_kernel, n_waves=nq),
        grid_spec=pltpu.PrefetchScalarGridSpec(
            num_scalar_prefetch=2,
            grid=(b, SB_PAIRS),
            in_specs=[seq(COL_SB_Q), seq(COL_SB_K), seq(COL_SB_V), seq(COL_SB_G)],
            out_specs=pl.BlockSpec((s, LANES), lambda bi, p, tab, wav: (bi, p)),
            scratch_shapes=[pltpu.VMEM((t, t), BF16),
                            pltpu.VMEM((3, t, t), F32),
                            pltpu.VMEM((SB_STEP, 2 * t, t), F32),
                            pltpu.VMEM((SB_STEP, 2 * t, t), BF16),
                            pltpu.VMEM((SB_STEP, 2 * t, t), BF16),
                            pltpu.VMEM((nq + 2, 2 * t, LANES), F32),
                            pltpu.VMEM((2, 2 * t, LANES), F32),
                            pltpu.VMEM((s, LANES), F32)]),
        out_shape=jax.ShapeDtypeStruct((b * s, SB_WIDTH), BF16),
        compiler_params=pltpu.CompilerParams(
            dimension_semantics=("arbitrary", "arbitrary"),
            vmem_limit_bytes=VMEM_LIMIT),
        name="sb_attention",
    )(table, waves, proj, proj, proj, proj)


def _ret_kernel(q_ref, k_ref, v_ref, g_ref, ng_ref, lg_ref, o_ref, kv_ref, state_ref):
    c = RET_CHUNK
    first, second = _head_masks()
    nchunks = q_ref.shape[0] // c

    lg_lane = lg_ref[0]
    lg_a = lg_lane[:, 0:1]
    lg_b = lg_lane[:, HEAD_DIM:HEAD_DIM + 1]

    ri = lax.broadcasted_iota(jnp.int32, (c, c), 0)
    ci = lax.broadcasted_iota(jnp.int32, (c, c), 1)
    diff = (ri - ci).astype(F32)
    keep = ri >= ci
    decay_a = jnp.where(keep, jnp.exp(lg_a * jnp.maximum(diff, 0.0)), 0.0)
    decay_b = jnp.where(keep, jnp.exp(lg_b * jnp.maximum(diff, 0.0)), 0.0)
    idx = lax.broadcasted_iota(jnp.int32, (c, LANES), 0).astype(F32)
    q_decay = jnp.exp(lg_lane * (idx + 1.0))
    k_decay = jnp.exp(lg_lane * (c - 1.0 - idx))
    si = lax.broadcasted_iota(jnp.int32, (LANES, LANES), 0)
    sj = lax.broadcasted_iota(jnp.int32, (LANES, LANES), 1)
    same_head = (si < HEAD_DIM) == (sj < HEAD_DIM)
    state_decay = jnp.where(same_head, jnp.exp(lg_lane * float(c)), 0.0)
    norm_g = ng_ref[...]

    chunk_rows = [slice(n * c, (n + 1) * c) for n in range(nchunks)]

    for n, rows in enumerate(chunk_rows):
        kd = (k_ref[rows, :].astype(F32) * k_decay).astype(BF16)
        kv_ref[n] = jnp.where(same_head, _dot_tn(kd, v_ref[rows, :]), 0.0)

    state = jnp.zeros((LANES, LANES), F32)
    for n in range(nchunks):
        state_ref[n] = state.astype(BF16)
        state = state * state_decay + kv_ref[n]

    for n, rows in enumerate(chunk_rows):
        qb = q_ref[rows, :]
        kb = k_ref[rows, :]
        v = v_ref[rows, :]
        out = _dot(qb, state_ref[n]) * q_decay
        for mask, decay in ((first, decay_a), (second, decay_b)):
            scores = _dot_nt(jnp.where(mask, qb, 0), kb) * decay
            out = out + _dot(scores.astype(BF16), jnp.where(mask, v, 0))
        inv = lax.rsqrt(_pair_mean_square(out, first) + EPS)
        y = out * inv * norm_g
        o_ref[rows, :] = (y * _silu(g_ref[rows, :].astype(F32))).astype(BF16)


def _retention(proj, ret_norm_g, b, s):
    seq = lambda col: pl.BlockSpec((s, LANES), lambda bi, p: (bi, col + p))
    gamma = 1.0 - 2.0 ** (-5.0 - jnp.arange(RET_HEADS, dtype=F32))
    log_gamma = jnp.repeat(jnp.log(gamma), HEAD_DIM).reshape(RET_PAIRS, 1, LANES)
    return pl.pallas_call(
        _ret_kernel,
        grid=(b, RET_PAIRS),
        in_specs=[seq(COL_RET_Q), seq(COL_RET_K), seq(COL_RET_V), seq(COL_RET_G),
                  pl.BlockSpec((1, LANES), lambda bi, p: (0, p)),
                  pl.BlockSpec((1, 1, LANES), lambda bi, p: (p, 0, 0))],
        out_specs=pl.BlockSpec((s, LANES), lambda bi, p: (bi, p)),
        out_shape=jax.ShapeDtypeStruct((b * s, RET_WIDTH), BF16),
        scratch_shapes=[pltpu.VMEM((s // RET_CHUNK, LANES, LANES), F32),
                        pltpu.VMEM((s // RET_CHUNK, LANES, LANES), BF16)],
        compiler_params=pltpu.CompilerParams(vmem_limit_bytes=VMEM_LIMIT),
        name="retention",
    )(proj, proj, proj, proj, ret_norm_g, log_gamma)


def _mem_kv_kernel(mem_ref, g_ref, w_ref, kg_ref, k_ref, v_ref):
    first, _ = _head_masks()
    x = mem_ref[...]
    ms = jnp.mean(x * x, axis=-1, keepdims=True)
    h = (x * lax.rsqrt(ms + EPS) * g_ref[0]).astype(BF16)
    kv = _dot(h, w_ref[0])
    kg = kg_ref[0]
    for pair in range(MEM_PAIRS):
        cols = slice(pair * LANES, (pair + 1) * LANES)
        kp = kv[:, cols]
        kn = kp * lax.rsqrt(_pair_mean_square(kp, first) + EPS) * kg
        k_ref[0, :, cols] = kn.astype(BF16)
    v_ref[0] = kv[:, MEM_WIDTH:].astype(BF16)


def _mem_kv(mem2, mem_norm_g, w_mem_kv, k_norm_g, b, tokens):
    depth = w_mem_kv.shape[0]
    out = jax.ShapeDtypeStruct((depth, b * tokens, MEM_WIDTH), BF16)
    return pl.pallas_call(
        _mem_kv_kernel,
        grid=(depth, b),
        in_specs=[pl.BlockSpec((tokens, D_MODEL), lambda l, bi: (bi, 0)),
                  pl.BlockSpec((1, 1, D_MODEL), lambda l, bi: (l, 0, 0)),
                  pl.BlockSpec((1, D_MODEL, 2 * MEM_WIDTH), lambda l, bi: (l, 0, 0)),
                  pl.BlockSpec((1, 1, LANES), lambda l, bi: (l, 0, 0))],
        out_specs=[pl.BlockSpec((1, tokens, MEM_WIDTH), lambda l, bi: (l, bi, 0)),
                   pl.BlockSpec((1, tokens, MEM_WIDTH), lambda l, bi: (l, bi, 0))],
        out_shape=[out, out],
        compiler_params=pltpu.CompilerParams(vmem_limit_bytes=VMEM_LIMIT),
        name="mem_kv",
    )(mem2, mem_norm_g, w_mem_kv, k_norm_g)


def _mem_attn_kernel(q_ref, g_ref, k_ref, v_ref, qg_ref, o_ref):
    first, second = _head_masks()
    q = q_ref[...].astype(F32)
    qn = (q * lax.rsqrt(_pair_mean_square(q, first) + EPS) * qg_ref[...]).astype(BF16)
    k = k_ref[...]
    v = v_ref[...]
    out = jnp.zeros(q.shape, F32)
    for mask in (first, second):
        sc = _dot_nt(jnp.where(mask, qn, 0), k) * QK_SCALE
        e = jnp.exp(sc - jnp.max(sc, axis=-1, keepdims=True))
        denom = jnp.sum(e, axis=-1, keepdims=True)
        out = out + _dot(e.astype(BF16), jnp.where(mask, v, 0)) / denom
    o_ref[...] = (out * _silu(g_ref[...].astype(F32))).astype(BF16)


def _mem_attention(proj, mk, mv, q_norm_g, b, s, tokens):
    nq = s // MEM_ROWS
    kv_spec = pl.BlockSpec((tokens, LANES), lambda bi, p, i: (bi, p))
    return pl.pallas_call(
        _mem_attn_kernel,
        grid=(b, MEM_PAIRS, nq),
        in_specs=[pl.BlockSpec((MEM_ROWS, LANES), lambda bi, p, i: (bi * nq + i, COL_MEM_Q + p)),
                  pl.BlockSpec((MEM_ROWS, LANES), lambda bi, p, i: (bi * nq + i, COL_MEM_G + p)),
                  kv_spec, kv_spec,
                  pl.BlockSpec((1, LANES), lambda bi, p, i: (0, 0))],
        out_specs=pl.BlockSpec((MEM_ROWS, LANES), lambda bi, p, i: (bi * nq + i, p)),
        out_shape=jax.ShapeDtypeStruct((b * s, MEM_WIDTH), BF16),
        compiler_params=pltpu.CompilerParams(vmem_limit_bytes=VMEM_LIMIT),
        name="mem_attention",
    )(proj, proj, mk, mv, q_norm_g)


def kernel(x, mem, positions, norm_g, w_in, w_out, mem_norm_g, w_mem_kv, mem_q_norm_g,
           mem_k_norm_g, ret_norm_g):
    b, s, d = x.shape
    tokens = mem.shape[1]
    depth = w_in.shape[0]
    assert d == D_MODEL and w_in.shape[2] == IN_WIDTH
    assert s % SB_BLOCK == 0 and s % RET_CHUNK == 0 and s % MEM_ROWS == 0
    assert (b * s) % PROJ_ROWS == 0

    w_in_b = w_in.astype(BF16)
    w_out_b = w_out.astype(BF16)
    w_kv_b = w_mem_kv.astype(BF16)
    q_norm_g2 = jnp.tile(mem_q_norm_g, (1, PAIR))
    k_norm_g3 = jnp.tile(mem_k_norm_g, (1, PAIR))[:, None, :]

    cos, sin = _rope_tables(positions)
    mk_all, mv_all = _mem_kv(mem.reshape(b * tokens, d), mem_norm_g[:, None, :], w_kv_b,
                             k_norm_g3, b, tokens)

    x2 = x.reshape(b * s, d)
    proj = _in_proj(x2, norm_g[0][None, :], w_in_b[0], cos, sin)
    for l in range(depth):
        sb_o = _sb_attention(proj, b, s)
        ret_o = _retention(proj, ret_norm_g[l][None, :], b, s)
        mem_o = _mem_attention(proj, mk_all[l], mv_all[l], q_norm_g2[l][None, :], b, s, tokens)
        if l + 1 < depth:
            x2, proj = _out_in_proj(sb_o, ret_o, mem_o, w_out_b[l], x2, norm_g[l + 1][None, :],
                                    w_in_b[l + 1], cos, sin)
        else:
            x2 = _out_proj(sb_o, ret_o, mem_o, w_out_b[l], x2)
    return x2.reshape(b, s, d)
```

```python
import functools
import math

import jax
import jax.numpy as jnp
from jax import lax
from jax.experimental import pallas as pl
from jax.experimental.pallas import tpu as pltpu

D_MODEL = 1024
HEAD_DIM = 64
SB_HEADS = 6
RET_HEADS = 6
MEM_HEADS = 4
SB_WIDTH = SB_HEADS * HEAD_DIM
RET_WIDTH = RET_HEADS * HEAD_DIM
MEM_WIDTH = MEM_HEADS * HEAD_DIM
MIX_WIDTH = SB_WIDTH + RET_WIDTH + MEM_WIDTH
IN_WIDTH = 4 * SB_WIDTH + 4 * RET_WIDTH + 2 * MEM_WIDTH
ROPE_BASE = 10000.0
EPS = 1e-6
QK_SCALE = HEAD_DIM ** -0.5
SB_Q_SCALE = QK_SCALE * math.log2(math.e)

LANES = 128
PAIR = LANES // HEAD_DIM
SB_PAIRS = SB_HEADS // PAIR
RET_PAIRS = RET_HEADS // PAIR
MEM_PAIRS = MEM_HEADS // PAIR

COL_SB_Q = 0
COL_SB_K = COL_SB_Q + SB_PAIRS
COL_SB_V = COL_SB_K + SB_PAIRS
COL_SB_G = COL_SB_V + SB_PAIRS
COL_RET_Q = COL_SB_G + SB_PAIRS
COL_RET_K = COL_RET_Q + RET_PAIRS
COL_RET_V = COL_RET_K + RET_PAIRS
COL_RET_G = COL_RET_V + RET_PAIRS
COL_MEM_Q = COL_RET_G + RET_PAIRS
COL_MEM_G = COL_MEM_Q + MEM_PAIRS

PROJ_ROWS = 512
PROJ_COLS = 512
SB_BLOCK = 256
RET_CHUNK = 256
MEM_ROWS = 512
VMEM_LIMIT = 48 * 1024 * 1024

F32 = jnp.float32
BF16 = jnp.bfloat16


def _dot(a, b):
    return jnp.dot(a, b, preferred_element_type=F32)


def _dot_nt(a, b):
    return lax.dot_general(a, b, (((1,), (1,)), ((), ())), preferred_element_type=F32)


def _dot_tn(a, b):
    return lax.dot_general(a, b, (((0,), (0,)), ((), ())), preferred_element_type=F32)


def _silu(g):
    return g / (1.0 + jnp.exp(-g))


def _head_masks():
    lane = lax.broadcasted_iota(jnp.int32, (1, LANES), 1)
    first = lane < HEAD_DIM
    return first, jnp.logical_not(first)


def _pair_mean_square(t, first):
    sq = t * t
    s_a = jnp.sum(jnp.where(first, sq, 0.0), axis=-1, keepdims=True)
    s_b = jnp.sum(jnp.where(first, 0.0, sq), axis=-1, keepdims=True)
    return jnp.where(first, s_a, s_b) * (1.0 / HEAD_DIM)


def _rope_kernel(pos_ref, invf_ref, cos_ref, sin_ref):
    ang = pos_ref[...].astype(F32) * invf_ref[...]
    lane = lax.broadcasted_iota(jnp.int32, (1, LANES), 1)
    sign = jnp.where((lane % HEAD_DIM) < HEAD_DIM // 2, -1.0, 1.0)
    cos_ref[...] = jnp.cos(ang)
    sin_ref[...] = jnp.sin(ang) * sign


def _rope_tables(positions):
    b, s = positions.shape
    half = HEAD_DIM // 2
    inv_freq = ROPE_BASE ** (-jnp.arange(half, dtype=F32) / half)
    invf = jnp.tile(inv_freq, LANES // half)[None, :]
    pos = positions.reshape(b * s, 1)
    rows = s
    return pl.pallas_call(
        _rope_kernel,
        grid=(b * s // rows,),
        in_specs=[pl.BlockSpec((rows, 1), lambda i: (i, 0)),
                  pl.BlockSpec((1, LANES), lambda i: (0, 0))],
        out_specs=[pl.BlockSpec((rows, LANES), lambda i: (i, 0)),
                   pl.BlockSpec((rows, LANES), lambda i: (i, 0))],
        out_shape=[jax.ShapeDtypeStruct((b * s, LANES), F32)] * 2,
        name="rope_tables",
    )(pos, invf)


def _swap_halves(t):
    lane = lax.broadcasted_iota(jnp.int32, (1, LANES), 1)
    half = HEAD_DIM // 2
    lower = (lane % HEAD_DIM) < half
    return jnp.where(lower, pltpu.roll(t, LANES - half, 1), pltpu.roll(t, half, 1))


def _norm_project(x, g_ref, w_ref, cos_ref, sin_ref, o_ref):
    ms = jnp.mean(x * x, axis=-1, keepdims=True)
    h = (x * lax.rsqrt(ms + EPS) * g_ref[...]).astype(BF16)
    cos = cos_ref[...]
    sin = sin_ref[...]
    blocks_per_dot = PROJ_COLS // LANES
    for j in range(IN_WIDTH // PROJ_COLS):
        y = _dot(h, w_ref[:, j * PROJ_COLS:(j + 1) * PROJ_COLS])
        for b in range(blocks_per_dot):
            blk = j * blocks_per_dot + b
            piece = y[:, b * LANES:(b + 1) * LANES]
            if blk < COL_SB_K:
                piece = piece * SB_Q_SCALE
            elif COL_RET_Q <= blk < COL_RET_V:
                piece = piece * cos + _swap_halves(piece) * sin
                if blk >= COL_RET_K:
                    piece = piece * QK_SCALE
            o_ref[:, blk * LANES:(blk + 1) * LANES] = piece.astype(BF16)


def _in_proj_kernel(x_ref, g_ref, w_ref, cos_ref, sin_ref, o_ref):
    _norm_project(x_ref[...], g_ref, w_ref, cos_ref, sin_ref, o_ref)


def _mix(sb_ref, ret_ref, mem_ref, w_ref, x_ref):
    mixed = jnp.concatenate([sb_ref[...], ret_ref[...], mem_ref[...]], axis=1)
    return x_ref[...] + _dot(mixed, w_ref[...])


def _out_proj_kernel(sb_ref, ret_ref, mem_ref, w_ref, x_ref, o_ref):
    o_ref[...] = _mix(sb_ref, ret_ref, mem_ref, w_ref, x_ref)


def _out_in_proj_kernel(sb_ref, ret_ref, mem_ref, w_out_ref, x_ref, g_ref, w_in_ref, cos_ref,
                        sin_ref, x_out_ref, proj_ref):
    x_new = _mix(sb_ref, ret_ref, mem_ref, w_out_ref, x_ref)
    x_out_ref[...] = x_new
    _norm_project(x_new, g_ref, w_in_ref, cos_ref, sin_ref, proj_ref)


def _row_tiles(width):
    return pl.BlockSpec((PROJ_ROWS, width), lambda i: (i, 0))


def _resident(shape):
    return pl.BlockSpec(shape, lambda i: (0, 0), pipeline_mode=pl.Buffered(1))


def _in_proj(x2, g, w, cos, sin):
    m = x2.shape[0]
    return pl.pallas_call(
        _in_proj_kernel,
        grid=(m // PROJ_ROWS,),
        in_specs=[_row_tiles(D_MODEL), _resident((1, D_MODEL)), _resident((D_MODEL, IN_WIDTH)),
                  _row_tiles(LANES), _row_tiles(LANES)],
        out_specs=_row_tiles(IN_WIDTH),
        out_shape=jax.ShapeDtypeStruct((m, IN_WIDTH), BF16),
        compiler_params=pltpu.CompilerParams(vmem_limit_bytes=VMEM_LIMIT),
        name="in_proj",
    )(x2, g, w, cos, sin)


def _out_proj(sb_o, ret_o, mem_o, w, x2):
    m = x2.shape[0]
    return pl.pallas_call(
        _out_proj_kernel,
        grid=(m // PROJ_ROWS,),
        in_specs=[_row_tiles(SB_WIDTH), _row_tiles(RET_WIDTH), _row_tiles(MEM_WIDTH),
                  _resident((MIX_WIDTH, D_MODEL)), _row_tiles(D_MODEL)],
        out_specs=_row_tiles(D_MODEL),
        out_shape=jax.ShapeDtypeStruct((m, D_MODEL), F32),
        compiler_params=pltpu.CompilerParams(vmem_limit_bytes=VMEM_LIMIT),
        name="out_proj",
    )(sb_o, ret_o, mem_o, w, x2)


def _out_in_proj(sb_o, ret_o, mem_o, w_out, x2, g, w_in, cos, sin):
    m = x2.shape[0]
    return pl.pallas_call(
        _out_in_proj_kernel,
        grid=(m // PROJ_ROWS,),
        in_specs=[_row_tiles(SB_WIDTH), _row_tiles(RET_WIDTH), _row_tiles(MEM_WIDTH),
                  _resident((MIX_WIDTH, D_MODEL)), _row_tiles(D_MODEL),
                  _resident((1, D_MODEL)), _resident((D_MODEL, IN_WIDTH)),
                  _row_tiles(LANES), _row_tiles(LANES)],
        out_specs=[_row_tiles(D_MODEL), _row_tiles(IN_WIDTH)],
        out_shape=[jax.ShapeDtypeStruct((m, D_MODEL), F32),
                   jax.ShapeDtypeStruct((m, IN_WIDTH), BF16)],
        compiler_params=pltpu.CompilerParams(vmem_limit_bytes=VMEM_LIMIT),
        name="out_in_proj",
    )(sb_o, ret_o, mem_o, w_out, x2, g, w_in, cos, sin)


SB_LAG = 2
SB_STEP = 4
SB_MASKED = -1e30
SB_DEAD = 160.0
SB_BIAS_NONE, SB_BIAS_DIAG, SB_BIAS_ALL = 0, 1, 2


def _sb_schedule(nq):
    zero_slot, junk_slot = nq, nq + 1
    idle = (0, 0, SB_BIAS_ALL, junk_slot, junk_slot, 1)
    items = [idle] * SB_STEP
    waves = []
    for w in range(nq):
        start = len(items)
        for qb in range(nq - 1, w - 1, -1):
            src = zero_slot if w == 0 else qb
            bias = SB_BIAS_DIAG if w == 0 else SB_BIAS_NONE
            items.append((qb, qb - w, bias, src, qb, 0 if qb > w else 1))
        items += [idle] * (-(len(items) - start) % SB_STEP)
        waves.append((start, (len(items) - start) // SB_STEP))
    return jnp.asarray(items, jnp.int32).T, jnp.asarray(waves, jnp.int32).T


def _sb_kernel(tab_ref, wave_ref, q_ref, k_ref, v_ref, g_ref, o_ref,
               tri_ref, bias_ref, z_ref, sp_ref, w_ref, r_ref, m_ref, acc_ref, *, n_waves):
    t = SB_BLOCK
    first, second = _head_masks()
    n_carry = r_ref.shape[0]

    @pl.when(jnp.logical_and(pl.program_id(0) == 0, pl.program_id(1) == 0))
    def _constants():
        row = lax.broadcasted_iota(jnp.int32, (t, t), 0)
        col = lax.broadcasted_iota(jnp.int32, (t, t), 1)
        tri_ref[...] = (row >= col).astype(BF16)
        bias_ref[SB_BIAS_NONE] = jnp.zeros((t, t), F32)
        bias_ref[SB_BIAS_DIAG] = jnp.where(col < row, 0.0, SB_MASKED)
        bias_ref[SB_BIAS_ALL] = jnp.full((t, t), SB_MASKED, F32)
        r_ref[n_carry - 2] = jnp.zeros(r_ref.shape[1:], F32)

    z_ref[SB_STEP - 1] = jnp.full(z_ref.shape[1:], SB_MASKED, F32)
    sp_ref[SB_STEP - 1] = jnp.zeros(sp_ref.shape[1:], BF16)
    for slot in range(SB_STEP - SB_LAG, SB_STEP):
        w_ref[slot] = jnp.zeros(w_ref.shape[1:], BF16)
    r_ref[n_carry - 1] = jnp.zeros(r_ref.shape[1:], F32)
    m_ref[1] = jnp.zeros(m_ref.shape[1:], F32)
    acc_ref[...] = jnp.zeros_like(acc_ref)

    def rows(blk):
        return pl.ds(pl.multiple_of(blk * t, t), t)

    def scores(j, slot):
        qb, kb = tab_ref[0, j], tab_ref[1, j]
        q = q_ref[rows(qb), :]
        q2 = jnp.concatenate([jnp.where(first, q, 0), jnp.where(second, q, 0)], axis=0)
        bias = bias_ref[tab_ref[2, j]]
        z = _dot_nt(q2, k_ref[rows(kb), :]) + jnp.concatenate([bias, bias], axis=0)
        sp = jnp.maximum(z, 0.0) + jnp.log2(1.0 + jnp.exp2(-jnp.abs(z)))
        z_ref[slot] = z
        sp_ref[slot] = sp.astype(BF16)

    def weights(j, slot):
        cum = _dot(sp_ref[slot], tri_ref[...])
        r = r_ref[tab_ref[3, j]]
        z = z_ref[slot]
        halves = [jnp.exp2(z[:, c:c + LANES] - cum[:, c:c + LANES] - r) for c in (0, LANES)]
        w_ref[slot] = jnp.concatenate(halves, axis=1).astype(BF16)
        r_new = r + cum[:, 0:1]
        r_ref[tab_ref[4, j]] = r_new
        m_slot = tab_ref[5, j]
        m_ref[m_slot] = jnp.minimum(m_ref[m_slot], r_new)

    def values(j, slot):
        qb, kb = tab_ref[0, j], tab_ref[1, j]
        pv = _dot(w_ref[slot], v_ref[rows(kb), :])
        acc_ref[rows(qb), :] += jnp.where(first, pv[:t], pv[t:])

    def wave(state):
        w = state[0]
        start, n_steps = wave_ref[0, w], wave_ref[1, w]
        m_ref[0] = jnp.full(m_ref.shape[1:], jnp.inf, F32)

        def step(n, _):
            for i in range(SB_STEP):
                c = start + SB_STEP * n + i
                values(c - 2, (i - 2) % SB_STEP)
                weights(c - 1, (i - 1) % SB_STEP)
                scores(c, i)
            return 0

        lax.fori_loop(0, n_steps, step, 0)
        return w + 1, start + SB_STEP * n_steps, jnp.min(m_ref[0])

    _, end, _ = lax.while_loop(lambda s: jnp.logical_and(s[0] < n_waves, s[2] < SB_DEAD), wave,
                               (jnp.int32(0), jnp.int32(SB_STEP), jnp.float32(0.0)))
    values(end - 2, SB_STEP - 2)
    weights(end - 1, SB_STEP - 1)
    values(end - 1, SB_STEP - 1)
    o_ref[...] = (acc_ref[...] * _silu(g_ref[...].astype(F32))).astype(BF16)


def _sb_attention(proj, b, s):
    nq = s // SB_BLOCK
    table, waves = _sb_schedule(nq)
    t = SB_BLOCK
    seq = lambda col: pl.BlockSpec((s, LANES), lambda bi, p, tab, wav: (bi, col + p))
    return pl.pallas_call(
        functools.partial(_sb_kernel, n_waves=nq),
        grid_spec=pltpu.PrefetchScalarGridSpec(
            num_scalar_prefetch=2,
            grid=(b, SB_PAIRS),
            in_specs=[seq(COL_SB_Q), seq(COL_SB_K), seq(COL_SB_V), seq(COL_SB_G)],
            out_specs=pl.BlockSpec((s, LANES), lambda bi, p, tab, wav: (bi, p)),
            scratch_shapes=[pltpu.VMEM((t, t), BF16),
                            pltpu.VMEM((3, t, t), F32),
                            pltpu.VMEM((SB_STEP, 2 * t, t), F32),
                            pltpu.VMEM((SB_STEP, 2 * t, t), BF16),
                            pltpu.VMEM((SB_STEP, 2 * t, t), BF16),
                            pltpu.VMEM((nq + 2, 2 * t, LANES), F32),
                            pltpu.VMEM((2, 2 * t, LANES), F32),
                            pltpu.VMEM((s, LANES), F32)]),
        out_shape=jax.ShapeDtypeStruct((b * s, SB_WIDTH), BF16),
        compiler_params=pltpu.CompilerParams(
            dimension_semantics=("arbitrary", "arbitrary"),
            vmem_limit_bytes=VMEM_LIMIT),
        name="sb_attention",
    )(table, waves, proj, proj, proj, proj)


def _ret_kernel(q_ref, k_ref, v_ref, g_ref, ng_ref, lg_ref, o_ref, kv_ref, state_ref):
    c = RET_CHUNK
    first, second = _head_masks()
    nchunks = q_ref.shape[0] // c

    lg_lane = lg_ref[0]
    lg_a = lg_lane[:, 0:1]
    lg_b = lg_lane[:, HEAD_DIM:HEAD_DIM + 1]

    ri = lax.broadcasted_iota(jnp.int32, (c, c), 0)
    ci = lax.broadcasted_iota(jnp.int32, (c, c), 1)
    diff = (ri - ci).astype(F32)
    keep = ri >= ci
    decay_a = jnp.where(keep, jnp.exp(lg_a * jnp.maximum(diff, 0.0)), 0.0)
    decay_b = jnp.where(keep, jnp.exp(lg_b * jnp.maximum(diff, 0.0)), 0.0)
    idx = lax.broadcasted_iota(jnp.int32, (c, LANES), 0).astype(F32)
    q_decay = jnp.exp(lg_lane * (idx + 1.0))
    k_decay = jnp.exp(lg_lane * (c - 1.0 - idx))
    si = lax.broadcasted_iota(jnp.int32, (LANES, LANES), 0)
    sj = lax.broadcasted_iota(jnp.int32, (LANES, LANES), 1)
    same_head = (si < HEAD_DIM) == (sj < HEAD_DIM)
    state_decay = jnp.where(same_head, jnp.exp(lg_lane * float(c)), 0.0)
    norm_g = ng_ref[...]

    chunk_rows = [slice(n * c, (n + 1) * c) for n in range(nchunks)]

    for n, rows in enumerate(chunk_rows):
        kd = (k_ref[rows, :].astype(F32) * k_decay).astype(BF16)
        kv_ref[n] = jnp.where(same_head, _dot_tn(kd, v_ref[rows, :]), 0.0)

    state = jnp.zeros((LANES, LANES), F32)
    for n in range(nchunks):
        state_ref[n] = state.astype(BF16)
        state = state * state_decay + kv_ref[n]

    for n, rows in enumerate(chunk_rows):
        qb = q_ref[rows, :]
        kb = k_ref[rows, :]
        v = v_ref[rows, :]
        out = _dot(qb, state_ref[n]) * q_decay
        for mask, decay in ((first, decay_a), (second, decay_b)):
            scores = _dot_nt(jnp.where(mask, qb, 0), kb) * decay
            out = out + _dot(scores.astype(BF16), jnp.where(mask, v, 0))
        inv = lax.rsqrt(_pair_mean_square(out, first) + EPS)
        y = out * inv * norm_g
        o_ref[rows, :] = (y * _silu(g_ref[rows, :].astype(F32))).astype(BF16)


def _retention(proj, ret_norm_g, b, s):
    seq = lambda col: pl.BlockSpec((s, LANES), lambda bi, p: (bi, col + p))
    gamma = 1.0 - 2.0 ** (-5.0 - jnp.arange(RET_HEADS, dtype=F32))
    log_gamma = jnp.repeat(jnp.log(gamma), HEAD_DIM).reshape(RET_PAIRS, 1, LANES)
    return pl.pallas_call(
        _ret_kernel,
        grid=(b, RET_PAIRS),
        in_specs=[seq(COL_RET_Q), seq(COL_RET_K), seq(COL_RET_V), seq(COL_RET_G),
                  pl.BlockSpec((1, LANES), lambda bi, p: (0, p)),
                  pl.BlockSpec((1, 1, LANES), lambda bi, p: (p, 0, 0))],
        out_specs=pl.BlockSpec((s, LANES), lambda bi, p: (bi, p)),
        out_shape=jax.ShapeDtypeStruct((b * s, RET_WIDTH), BF16),
        scratch_shapes=[pltpu.VMEM((s // RET_CHUNK, LANES, LANES), F32),
                        pltpu.VMEM((s // RET_CHUNK, LANES, LANES), BF16)],
        compiler_params=pltpu.CompilerParams(vmem_limit_bytes=VMEM_LIMIT),
        name="retention",
    )(proj, proj, proj, proj, ret_norm_g, log_gamma)


def _mem_kv_kernel(mem_ref, g_ref, w_ref, kg_ref, k_ref, v_ref):
    first, _ = _head_masks()
    x = mem_ref[...]
    ms = jnp.mean(x * x, axis=-1, keepdims=True)
    h = (x * lax.rsqrt(ms + EPS) * g_ref[0]).astype(BF16)
    kv = _dot(h, w_ref[0])
    kg = kg_ref[0]
    for pair in range(MEM_PAIRS):
        cols = slice(pair * LANES, (pair + 1) * LANES)
        kp = kv[:, cols]
        kn = kp * lax.rsqrt(_pair_mean_square(kp, first) + EPS) * kg
        k_ref[0, :, cols] = kn.astype(BF16)
    v_ref[0] = kv[:, MEM_WIDTH:].astype(BF16)


def _mem_kv(mem2, mem_norm_g, w_mem_kv, k_norm_g, b, tokens):
    depth = w_mem_kv.shape[0]
    out = jax.ShapeDtypeStruct((depth, b * tokens, MEM_WIDTH), BF16)
    return pl.pallas_call(
        _mem_kv_kernel,
        grid=(depth, b),
        in_specs=[pl.BlockSpec((tokens, D_MODEL), lambda l, bi: (bi, 0)),
                  pl.BlockSpec((1, 1, D_MODEL), lambda l, bi: (l, 0, 0)),
                  pl.BlockSpec((1, D_MODEL, 2 * MEM_WIDTH), lambda l, bi: (l, 0, 0)),
                  pl.BlockSpec((1, 1, LANES), lambda l, bi: (l, 0, 0))],
        out_specs=[pl.BlockSpec((1, tokens, MEM_WIDTH), lambda l, bi: (l, bi, 0)),
                   pl.BlockSpec((1, tokens, MEM_WIDTH), lambda l, bi: (l, bi, 0))],
        out_shape=[out, out],
        compiler_params=pltpu.CompilerParams(vmem_limit_bytes=VMEM_LIMIT),
        name="mem_kv",
    )(mem2, mem_norm_g, w_mem_kv, k_norm_g)


def _mem_attn_kernel(q_ref, g_ref, k_ref, v_ref, qg_ref, o_ref, e_ref, den_ref):
    first, second = _head_masks()
    tile = MEM_ROWS
    n_tiles = q_ref.shape[0] // tile
    k = k_ref[...]
    v = v_ref[...]

    def probabilities(i, slot):
        rows = slice(i * tile, (i + 1) * tile)
        q = q_ref[rows, :].astype(F32)
        inv = lax.rsqrt(_pair_mean_square(q, first) + EPS) * QK_SCALE
        qn = (q * inv * qg_ref[...]).astype(BF16)
        q2 = jnp.concatenate([jnp.where(first, qn, 0), jnp.where(second, qn, 0)], axis=0)
        sc = _dot_nt(q2, k)
        e = jnp.exp(sc - jnp.max(sc, axis=-1, keepdims=True))
        den_ref[slot] = jnp.broadcast_to(jnp.sum(e, axis=-1, keepdims=True), (2 * tile, LANES))
        e_ref[slot] = e.astype(BF16)

    def output(i, slot):
        rows = slice(i * tile, (i + 1) * tile)
        pv = _dot(e_ref[slot], v) / den_ref[slot]
        out = jnp.where(first, pv[:tile], pv[tile:])
        o_ref[rows, :] = (out * _silu(g_ref[rows, :].astype(F32))).astype(BF16)

    for i in range(n_tiles + 1):
        if i >= 1:
            output(i - 1, (i - 1) % 2)
        if i < n_tiles:
            probabilities(i, i % 2)


def _mem_attention(proj, mk, mv, q_norm_g, b, s, tokens):
    seq = lambda col: pl.BlockSpec((s, LANES), lambda bi, p: (bi, col + p))
    kv_spec = pl.BlockSpec((tokens, LANES), lambda bi, p: (bi, p))
    return pl.pallas_call(
        _mem_attn_kernel,
        grid=(b, MEM_PAIRS),
        in_specs=[seq(COL_MEM_Q), seq(COL_MEM_G), kv_spec, kv_spec,
                  pl.BlockSpec((1, LANES), lambda bi, p: (0, 0))],
        out_specs=pl.BlockSpec((s, LANES), lambda bi, p: (bi, p)),
        out_shape=jax.ShapeDtypeStruct((b * s, MEM_WIDTH), BF16),
        scratch_shapes=[pltpu.VMEM((2, 2 * MEM_ROWS, tokens), BF16),
                        pltpu.VMEM((2, 2 * MEM_ROWS, LANES), F32)],
        compiler_params=pltpu.CompilerParams(vmem_limit_bytes=VMEM_LIMIT),
        name="mem_attention",
    )(proj, proj, mk, mv, q_norm_g)


def kernel(x, mem, positions, norm_g, w_in, w_out, mem_norm_g, w_mem_kv, mem_q_norm_g,
           mem_k_norm_g, ret_norm_g):
    b, s, d = x.shape
    tokens = mem.shape[1]
    depth = w_in.shape[0]
    assert d == D_MODEL and w_in.shape[2] == IN_WIDTH
    assert s % SB_BLOCK == 0 and s % RET_CHUNK == 0 and s % MEM_ROWS == 0
    assert (b * s) % PROJ_ROWS == 0

    w_in_b = w_in.astype(BF16)
    w_out_b = w_out.astype(BF16)
    w_kv_b = w_mem_kv.astype(BF16)
    q_norm_g2 = jnp.tile(mem_q_norm_g, (1, PAIR))
    k_norm_g3 = jnp.tile(mem_k_norm_g, (1, PAIR))[:, None, :]

    cos, sin = _rope_tables(positions)
    mk_all, mv_all = _mem_kv(mem.reshape(b * tokens, d), mem_norm_g[:, None, :], w_kv_b,
                             k_norm_g3, b, tokens)

    x2 = x.reshape(b * s, d)
    proj = _in_proj(x2, norm_g[0][None, :], w_in_b[0], cos, sin)
    for l in range(depth):
        sb_o = _sb_attention(proj, b, s)
        ret_o = _retention(proj, ret_norm_g[l][None, :], b, s)
        mem_o = _mem_attention(proj, mk_all[l], mv_all[l], q_norm_g2[l][None, :], b, s, tokens)
        if l + 1 < depth:
            x2, proj = _out_in_proj(sb_o, ret_o, mem_o, w_out_b[l], x2, norm_g[l + 1][None, :],
                                    w_in_b[l + 1], cos, sin)
        else:
            x2 = _out_proj(sb_o, ret_o, mem_o, w_out_b[l], x2)
    return x2.reshape(b, s, d)
```

```python
import functools
import math

import jax
import jax.numpy as jnp
from jax import lax
from jax.experimental import pallas as pl
from jax.experimental.pallas import tpu as pltpu

D_MODEL = 1024
HEAD_DIM = 64
SB_HEADS = 6
RET_HEADS = 6
MEM_HEADS = 4
SB_WIDTH = SB_HEADS * HEAD_DIM
RET_WIDTH = RET_HEADS * HEAD_DIM
MEM_WIDTH = MEM_HEADS * HEAD_DIM
MIX_WIDTH = SB_WIDTH + RET_WIDTH + MEM_WIDTH
IN_WIDTH = 4 * SB_WIDTH + 4 * RET_WIDTH + 2 * MEM_WIDTH
ROPE_BASE = 10000.0
EPS = 1e-6
QK_SCALE = HEAD_DIM ** -0.5
SB_Q_SCALE = QK_SCALE * math.log2(math.e)

LANES = 128
PAIR = LANES // HEAD_DIM
SB_PAIRS = SB_HEADS // PAIR
RET_PAIRS = RET_HEADS // PAIR
MEM_PAIRS = MEM_HEADS // PAIR

COL_SB_Q = 0
COL_SB_K = COL_SB_Q + SB_PAIRS
COL_SB_V = COL_SB_K + SB_PAIRS
COL_SB_G = COL_SB_V + SB_PAIRS
COL_RET_Q = COL_SB_G + SB_PAIRS
COL_RET_K = COL_RET_Q + RET_PAIRS
COL_RET_V = COL_RET_K + RET_PAIRS
COL_RET_G = COL_RET_V + RET_PAIRS
COL_MEM_Q = COL_RET_G + RET_PAIRS
COL_MEM_G = COL_MEM_Q + MEM_PAIRS

PROJ_ROWS = 512
PROJ_COLS = 512
SB_BLOCK = 256
RET_CHUNK = 256
MEM_ROWS = 512
VMEM_LIMIT = 48 * 1024 * 1024

F32 = jnp.float32
BF16 = jnp.bfloat16


def _dot(a, b):
    return jnp.dot(a, b, preferred_element_type=F32)


def _dot_nt(a, b):
    return lax.dot_general(a, b, (((1,), (1,)), ((), ())), preferred_element_type=F32)


def _dot_tn(a, b):
    return lax.dot_general(a, b, (((0,), (0,)), ((), ())), preferred_element_type=F32)


def _silu(g):
    return g / (1.0 + jnp.exp(-g))


def _head_masks():
    lane = lax.broadcasted_iota(jnp.int32, (1, LANES), 1)
    first = lane < HEAD_DIM
    return first, jnp.logical_not(first)


def _pair_mean_square(t, first):
    sq = t * t
    s_a = jnp.sum(jnp.where(first, sq, 0.0), axis=-1, keepdims=True)
    s_b = jnp.sum(jnp.where(first, 0.0, sq), axis=-1, keepdims=True)
    return jnp.where(first, s_a, s_b) * (1.0 / HEAD_DIM)


def _rope_kernel(pos_ref, invf_ref, cos_ref, sin_ref):
    ang = pos_ref[...].astype(F32) * invf_ref[...]
    lane = lax.broadcasted_iota(jnp.int32, (1, LANES), 1)
    sign = jnp.where((lane % HEAD_DIM) < HEAD_DIM // 2, -1.0, 1.0)
    cos_ref[...] = jnp.cos(ang)
    sin_ref[...] = jnp.sin(ang) * sign


def _rope_tables(positions):
    b, s = positions.shape
    half = HEAD_DIM // 2
    inv_freq = ROPE_BASE ** (-jnp.arange(half, dtype=F32) / half)
    invf = jnp.tile(inv_freq, LANES // half)[None, :]
    pos = positions.reshape(b * s, 1)
    rows = s
    return pl.pallas_call(
        _rope_kernel,
        grid=(b * s // rows,),
        in_specs=[pl.BlockSpec((rows, 1), lambda i: (i, 0)),
                  pl.BlockSpec((1, LANES), lambda i: (0, 0))],
        out_specs=[pl.BlockSpec((rows, LANES), lambda i: (i, 0)),
                   pl.BlockSpec((rows, LANES), lambda i: (i, 0))],
        out_shape=[jax.ShapeDtypeStruct((b * s, LANES), F32)] * 2,
        name="rope_tables",
    )(pos, invf)


def _swap_halves(t):
    lane = lax.broadcasted_iota(jnp.int32, (1, LANES), 1)
    half = HEAD_DIM // 2
    lower = (lane % HEAD_DIM) < half
    return jnp.where(lower, pltpu.roll(t, LANES - half, 1), pltpu.roll(t, half, 1))


def _norm_project(x, g_ref, w_ref, cos_ref, sin_ref, o_ref):
    ms = jnp.mean(x * x, axis=-1, keepdims=True)
    h = (x * lax.rsqrt(ms + EPS) * g_ref[0]).astype(BF16)
    cos = cos_ref[...]
    sin = sin_ref[...]
    blocks_per_dot = PROJ_COLS // LANES
    for j in range(IN_WIDTH // PROJ_COLS):
        y = _dot(h, w_ref[0, :, j * PROJ_COLS:(j + 1) * PROJ_COLS])
        for b in range(blocks_per_dot):
            blk = j * blocks_per_dot + b
            piece = y[:, b * LANES:(b + 1) * LANES]
            if blk < COL_SB_K:
                piece = piece * SB_Q_SCALE
            elif COL_RET_Q <= blk < COL_RET_V:
                piece = piece * cos + _swap_halves(piece) * sin
                if blk >= COL_RET_K:
                    piece = piece * QK_SCALE
            o_ref[:, blk * LANES:(blk + 1) * LANES] = piece.astype(BF16)


def _in_proj_kernel(x_ref, g_ref, w_ref, cos_ref, sin_ref, o_ref):
    _norm_project(x_ref[...], g_ref, w_ref, cos_ref, sin_ref, o_ref)


def _mix(sb_ref, ret_ref, mem_ref, w_ref, x_ref):
    mixed = jnp.concatenate([sb_ref[...], ret_ref[...], mem_ref[...]], axis=1)
    return x_ref[...] + _dot(mixed, w_ref[0])


def _out_proj_kernel(sb_ref, ret_ref, mem_ref, w_ref, x_ref, o_ref):
    o_ref[...] = _mix(sb_ref, ret_ref, mem_ref, w_ref, x_ref)


def _out_in_proj_kernel(sb_ref, ret_ref, mem_ref, w_out_ref, x_ref, g_ref, w_in_ref, cos_ref,
                        sin_ref, x_out_ref, proj_ref):
    x_new = _mix(sb_ref, ret_ref, mem_ref, w_out_ref, x_ref)
    x_out_ref[...] = x_new
    _norm_project(x_new, g_ref, w_in_ref, cos_ref, sin_ref, proj_ref)


def _row_tiles(width):
    return pl.BlockSpec((PROJ_ROWS, width), lambda i: (i, 0))


def _layer_param(shape, layer):
    return pl.BlockSpec((1,) + shape, lambda i: (layer, 0, 0), pipeline_mode=pl.Buffered(1))


def _in_proj(x2, g, w, layer, cos, sin):
    m = x2.shape[0]
    return pl.pallas_call(
        _in_proj_kernel,
        grid=(m // PROJ_ROWS,),
        in_specs=[_row_tiles(D_MODEL), _layer_param((1, D_MODEL), layer),
                  _layer_param((D_MODEL, IN_WIDTH), layer), _row_tiles(LANES), _row_tiles(LANES)],
        out_specs=_row_tiles(IN_WIDTH),
        out_shape=jax.ShapeDtypeStruct((m, IN_WIDTH), BF16),
        compiler_params=pltpu.CompilerParams(vmem_limit_bytes=VMEM_LIMIT),
        name="in_proj",
    )(x2, g, w, cos, sin)


def _out_proj(sb_o, ret_o, mem_o, w, layer, x2):
    m = x2.shape[0]
    return pl.pallas_call(
        _out_proj_kernel,
        grid=(m // PROJ_ROWS,),
        in_specs=[_row_tiles(SB_WIDTH), _row_tiles(RET_WIDTH), _row_tiles(MEM_WIDTH),
                  _layer_param((MIX_WIDTH, D_MODEL), layer), _row_tiles(D_MODEL)],
        out_specs=_row_tiles(D_MODEL),
        out_shape=jax.ShapeDtypeStruct((m, D_MODEL), F32),
        compiler_params=pltpu.CompilerParams(vmem_limit_bytes=VMEM_LIMIT),
        name="out_proj",
    )(sb_o, ret_o, mem_o, w, x2)


def _out_in_proj(sb_o, ret_o, mem_o, w_out, layer, x2, g, w_in, cos, sin):
    m = x2.shape[0]
    return pl.pallas_call(
        _out_in_proj_kernel,
        grid=(m // PROJ_ROWS,),
        in_specs=[_row_tiles(SB_WIDTH), _row_tiles(RET_WIDTH), _row_tiles(MEM_WIDTH),
                  _layer_param((MIX_WIDTH, D_MODEL), layer), _row_tiles(D_MODEL),
                  _layer_param((1, D_MODEL), layer + 1), _layer_param((D_MODEL, IN_WIDTH), layer + 1),
                  _row_tiles(LANES), _row_tiles(LANES)],
        out_specs=[_row_tiles(D_MODEL), _row_tiles(IN_WIDTH)],
        out_shape=[jax.ShapeDtypeStruct((m, D_MODEL), F32),
                   jax.ShapeDtypeStruct((m, IN_WIDTH), BF16)],
        compiler_params=pltpu.CompilerParams(vmem_limit_bytes=VMEM_LIMIT),
        name="out_in_proj",
    )(sb_o, ret_o, mem_o, w_out, x2, g, w_in, cos, sin)


SB_LAG = 2
SB_STEP = 4
SB_MASKED = -1e30
SB_DEAD = 160.0
SB_BIAS_NONE, SB_BIAS_DIAG, SB_BIAS_ALL = 0, 1, 2


def _sb_schedule(nq):
    zero_slot, junk_slot = nq, nq + 1
    idle = (0, 0, SB_BIAS_ALL, junk_slot, junk_slot, 1)
    items = [idle] * SB_STEP
    waves = []
    for w in range(nq):
        start = len(items)
        for qb in range(nq - 1, w - 1, -1):
            src = zero_slot if w == 0 else qb
            bias = SB_BIAS_DIAG if w == 0 else SB_BIAS_NONE
            items.append((qb, qb - w, bias, src, qb, 0 if qb > w else 1))
        items += [idle] * (-(len(items) - start) % SB_STEP)
        waves.append((start, (len(items) - start) // SB_STEP))
    return jnp.asarray(items, jnp.int32).T, jnp.asarray(waves, jnp.int32).T


def _sb_kernel(tab_ref, wave_ref, q_ref, k_ref, v_ref, g_ref, o_ref,
               tri_ref, bias_ref, z_ref, sp_ref, w_ref, r_ref, m_ref, acc_ref, *, n_waves):
    t = SB_BLOCK
    first, second = _head_masks()
    n_carry = r_ref.shape[0]

    @pl.when(jnp.logical_and(pl.program_id(0) == 0, pl.program_id(1) == 0))
    def _constants():
        row = lax.broadcasted_iota(jnp.int32, (t, t), 0)
        col = lax.broadcasted_iota(jnp.int32, (t, t), 1)
        tri_ref[...] = (row >= col).astype(BF16)
        bias_ref[SB_BIAS_NONE] = jnp.zeros((t, t), F32)
        bias_ref[SB_BIAS_DIAG] = jnp.where(col < row, 0.0, SB_MASKED)
        bias_ref[SB_BIAS_ALL] = jnp.full((t, t), SB_MASKED, F32)
        r_ref[n_carry - 2] = jnp.zeros(r_ref.shape[1:], F32)

    z_ref[SB_STEP - 1] = jnp.full(z_ref.shape[1:], SB_MASKED, F32)
    sp_ref[SB_STEP - 1] = jnp.zeros(sp_ref.shape[1:], BF16)
    for slot in range(SB_STEP - SB_LAG, SB_STEP):
        w_ref[slot] = jnp.zeros(w_ref.shape[1:], BF16)
    r_ref[n_carry - 1] = jnp.zeros(r_ref.shape[1:], F32)
    m_ref[1] = jnp.zeros(m_ref.shape[1:], F32)
    acc_ref[...] = jnp.zeros_like(acc_ref)

    def rows(blk):
        return pl.ds(pl.multiple_of(blk * t, t), t)

    def scores(j, slot):
        qb, kb = tab_ref[0, j], tab_ref[1, j]
        q = q_ref[rows(qb), :]
        q2 = jnp.concatenate([jnp.where(first, q, 0), jnp.where(second, q, 0)], axis=0)
        bias = bias_ref[tab_ref[2, j]]
        z = _dot_nt(q2, k_ref[rows(kb), :]) + jnp.concatenate([bias, bias], axis=0)
        sp = jnp.maximum(z, 0.0) + jnp.log2(1.0 + jnp.exp2(-jnp.abs(z)))
        z_ref[slot] = z
        sp_ref[slot] = sp.astype(BF16)

    def weights(j, slot):
        cum = _dot(sp_ref[slot], tri_ref[...])
        r = r_ref[tab_ref[3, j]]
        z = z_ref[slot]
        halves = [jnp.exp2(z[:, c:c + LANES] - cum[:, c:c + LANES] - r) for c in (0, LANES)]
        w_ref[slot] = jnp.concatenate(halves, axis=1).astype(BF16)
        r_new = r + cum[:, 0:1]
        r_ref[tab_ref[4, j]] = r_new
        m_slot = tab_ref[5, j]
        m_ref[m_slot] = jnp.minimum(m_ref[m_slot], r_new)

    def values(j, slot):
        qb, kb = tab_ref[0, j], tab_ref[1, j]
        pv = _dot(w_ref[slot], v_ref[rows(kb), :])
        acc_ref[rows(qb), :] += jnp.where(first, pv[:t], pv[t:])

    def wave(state):
        w = state[0]
        start, n_steps = wave_ref[0, w], wave_ref[1, w]
        m_ref[0] = jnp.full(m_ref.shape[1:], jnp.inf, F32)

        def step(n, _):
            for i in range(SB_STEP):
                c = start + SB_STEP * n + i
                values(c - 2, (i - 2) % SB_STEP)
                weights(c - 1, (i - 1) % SB_STEP)
                scores(c, i)
            return 0

        lax.fori_loop(0, n_steps, step, 0)
        return w + 1, start + SB_STEP * n_steps, jnp.min(m_ref[0])

    _, end, _ = lax.while_loop(lambda s: jnp.logical_and(s[0] < n_waves, s[2] < SB_DEAD), wave,
                               (jnp.int32(0), jnp.int32(SB_STEP), jnp.float32(0.0)))
    values(end - 2, SB_STEP - 2)
    weights(end - 1, SB_STEP - 1)
    values(end - 1, SB_STEP - 1)
    o_ref[...] = (acc_ref[...] * _silu(g_ref[...].astype(F32))).astype(BF16)


def _sb_attention(proj, b, s):
    nq = s // SB_BLOCK
    table, waves = _sb_schedule(nq)
    t = SB_BLOCK
    seq = lambda col: pl.BlockSpec((s, LANES), lambda bi, p, tab, wav: (bi, col + p))
    return pl.pallas_call(
        functools.partial(_sb_kernel, n_waves=nq),
        grid_spec=pltpu.PrefetchScalarGridSpec(
            num_scalar_prefetch=2,
            grid=(b, SB_PAIRS),
            in_specs=[seq(COL_SB_Q), seq(COL_SB_K), seq(COL_SB_V), seq(COL_SB_G)],
            out_specs=pl.BlockSpec((s, LANES), lambda bi, p, tab, wav: (bi, p)),
            scratch_shapes=[pltpu.VMEM((t, t), BF16),
                            pltpu.VMEM((3, t, t), F32),
                            pltpu.VMEM((SB_STEP, 2 * t, t), F32),
                            pltpu.VMEM((SB_STEP, 2 * t, t), BF16),
                            pltpu.VMEM((SB_STEP, 2 * t, t), BF16),
                            pltpu.VMEM((nq + 2, 2 * t, LANES), F32),
                            pltpu.VMEM((2, 2 * t, LANES), F32),
                            pltpu.VMEM((s, LANES), F32)]),
        out_shape=jax.ShapeDtypeStruct((b * s, SB_WIDTH), BF16),
        compiler_params=pltpu.CompilerParams(
            dimension_semantics=("arbitrary", "arbitrary"),
            vmem_limit_bytes=VMEM_LIMIT),
        name="sb_attention",
    )(table, waves, proj, proj, proj, proj)


def _ret_kernel(q_ref, k_ref, v_ref, g_ref, ng_ref, lg_ref, o_ref, kv_ref, state_ref):
    c = RET_CHUNK
    first, second = _head_masks()
    nchunks = q_ref.shape[0] // c

    lg_lane = lg_ref[0]
    lg_a = lg_lane[:, 0:1]
    lg_b = lg_lane[:, HEAD_DIM:HEAD_DIM + 1]

    ri = lax.broadcasted_iota(jnp.int32, (c, c), 0)
    ci = lax.broadcasted_iota(jnp.int32, (c, c), 1)
    diff = (ri - ci).astype(F32)
    keep = ri >= ci
    decay_a = jnp.where(keep, jnp.exp(lg_a * jnp.maximum(diff, 0.0)), 0.0)
    decay_b = jnp.where(keep, jnp.exp(lg_b * jnp.maximum(diff, 0.0)), 0.0)
    idx = lax.broadcasted_iota(jnp.int32, (c, LANES), 0).astype(F32)
    q_decay = jnp.exp(lg_lane * (idx + 1.0))
    k_decay = jnp.exp(lg_lane * (c - 1.0 - idx))
    si = lax.broadcasted_iota(jnp.int32, (LANES, LANES), 0)
    sj = lax.broadcasted_iota(jnp.int32, (LANES, LANES), 1)
    same_head = (si < HEAD_DIM) == (sj < HEAD_DIM)
    state_decay = jnp.where(same_head, jnp.exp(lg_lane * float(c)), 0.0)
    norm_g = ng_ref[0]

    chunk_rows = [slice(n * c, (n + 1) * c) for n in range(nchunks)]

    for n, rows in enumerate(chunk_rows):
        kd = (k_ref[rows, :].astype(F32) * k_decay).astype(BF16)
        kv_ref[n] = jnp.where(same_head, _dot_tn(kd, v_ref[rows, :]), 0.0)

    state = jnp.zeros((LANES, LANES), F32)
    for n in range(nchunks):
        state_ref[n] = state.astype(BF16)
        state = state * state_decay + kv_ref[n]

    for n, rows in enumerate(chunk_rows):
        qb = q_ref[rows, :]
        kb = k_ref[rows, :]
        v = v_ref[rows, :]
        out = _dot(qb, state_ref[n]) * q_decay
        for mask, decay in ((first, decay_a), (second, decay_b)):
            scores = _dot_nt(jnp.where(mask, qb, 0), kb) * decay
            out = out + _dot(scores.astype(BF16), jnp.where(mask, v, 0))
        inv = lax.rsqrt(_pair_mean_square(out, first) + EPS)
        y = out * inv * norm_g
        o_ref[rows, :] = (y * _silu(g_ref[rows, :].astype(F32))).astype(BF16)


def _retention(proj, ret_norm_g, layer, b, s):
    seq = lambda col: pl.BlockSpec((s, LANES), lambda bi, p: (bi, col + p))
    gamma = 1.0 - 2.0 ** (-5.0 - jnp.arange(RET_HEADS, dtype=F32))
    log_gamma = jnp.repeat(jnp.log(gamma), HEAD_DIM).reshape(RET_PAIRS, 1, LANES)
    return pl.pallas_call(
        _ret_kernel,
        grid=(b, RET_PAIRS),
        in_specs=[seq(COL_RET_Q), seq(COL_RET_K), seq(COL_RET_V), seq(COL_RET_G),
                  pl.BlockSpec((1, 1, LANES), lambda bi, p: (layer, 0, p)),
                  pl.BlockSpec((1, 1, LANES), lambda bi, p: (p, 0, 0))],
        out_specs=pl.BlockSpec((s, LANES), lambda bi, p: (bi, p)),
        out_shape=jax.ShapeDtypeStruct((b * s, RET_WIDTH), BF16),
        scratch_shapes=[pltpu.VMEM((s // RET_CHUNK, LANES, LANES), F32),
                        pltpu.VMEM((s // RET_CHUNK, LANES, LANES), BF16)],
        compiler_params=pltpu.CompilerParams(vmem_limit_bytes=VMEM_LIMIT),
        name="retention",
    )(proj, proj, proj, proj, ret_norm_g, log_gamma)


def _mem_kv_kernel(mem_ref, g_ref, w_ref, kg_ref, k_ref, v_ref):
    first, _ = _head_masks()
    x = mem_ref[...]
    ms = jnp.mean(x * x, axis=-1, keepdims=True)
    h = (x * lax.rsqrt(ms + EPS) * g_ref[0]).astype(BF16)
    kv = _dot(h, w_ref[0])
    kg = kg_ref[0]
    for pair in range(MEM_PAIRS):
        cols = slice(pair * LANES, (pair + 1) * LANES)
        kp = kv[:, cols]
        kn = kp * lax.rsqrt(_pair_mean_square(kp, first) + EPS) * kg
        k_ref[0, :, cols] = kn.astype(BF16)
    v_ref[0] = kv[:, MEM_WIDTH:].astype(BF16)


def _mem_kv(mem2, mem_norm_g, w_mem_kv, k_norm_g):
    depth = w_mem_kv.shape[0]
    rows = mem2.shape[0]
    out = jax.ShapeDtypeStruct((depth, rows, MEM_WIDTH), BF16)
    return pl.pallas_call(
        _mem_kv_kernel,
        grid=(depth,),
        in_specs=[pl.BlockSpec((rows, D_MODEL), lambda l: (0, 0)),
                  pl.BlockSpec((1, 1, D_MODEL), lambda l: (l, 0, 0)),
                  pl.BlockSpec((1, D_MODEL, 2 * MEM_WIDTH), lambda l: (l, 0, 0)),
                  pl.BlockSpec((1, 1, LANES), lambda l: (l, 0, 0))],
        out_specs=[pl.BlockSpec((1, rows, MEM_WIDTH), lambda l: (l, 0, 0)),
                   pl.BlockSpec((1, rows, MEM_WIDTH), lambda l: (l, 0, 0))],
        out_shape=[out, out],
        compiler_params=pltpu.CompilerParams(vmem_limit_bytes=VMEM_LIMIT),
        name="mem_kv",
    )(mem2, mem_norm_g, w_mem_kv, k_norm_g)


def _mem_attn_kernel(q_ref, g_ref, k_ref, v_ref, qg_ref, o_ref, e_ref, den_ref):
    first, second = _head_masks()
    tile = MEM_ROWS
    n_tiles = q_ref.shape[0] // tile
    k = k_ref[0]
    v = v_ref[0]

    def probabilities(i, slot):
        rows = slice(i * tile, (i + 1) * tile)
        q = q_ref[rows, :].astype(F32)
        inv = lax.rsqrt(_pair_mean_square(q, first) + EPS) * QK_SCALE
        qn = (q * inv * qg_ref[0]).astype(BF16)
        q2 = jnp.concatenate([jnp.where(first, qn, 0), jnp.where(second, qn, 0)], axis=0)
        sc = _dot_nt(q2, k)
        e = jnp.exp(sc - jnp.max(sc, axis=-1, keepdims=True))
        den_ref[slot] = jnp.broadcast_to(jnp.sum(e, axis=-1, keepdims=True), (2 * tile, LANES))
        e_ref[slot] = e.astype(BF16)

    def output(i, slot):
        rows = slice(i * tile, (i + 1) * tile)
        pv = _dot(e_ref[slot], v) / den_ref[slot]
        out = jnp.where(first, pv[:tile], pv[tile:])
        o_ref[rows, :] = (out * _silu(g_ref[rows, :].astype(F32))).astype(BF16)

    for i in range(n_tiles + 1):
        if i >= 1:
            output(i - 1, (i - 1) % 2)
        if i < n_tiles:
            probabilities(i, i % 2)


def _mem_attention(proj, mk, mv, q_norm_g, layer, b, s, tokens):
    seq = lambda col: pl.BlockSpec((s, LANES), lambda bi, p: (bi, col + p))
    kv_spec = pl.BlockSpec((1, tokens, LANES), lambda bi, p: (layer, bi, p))
    return pl.pallas_call(
        _mem_attn_kernel,
        grid=(b, MEM_PAIRS),
        in_specs=[seq(COL_MEM_Q), seq(COL_MEM_G), kv_spec, kv_spec,
                  pl.BlockSpec((1, 1, LANES), lambda bi, p: (layer, 0, 0))],
        out_specs=pl.BlockSpec((s, LANES), lambda bi, p: (bi, p)),
        out_shape=jax.ShapeDtypeStruct((b * s, MEM_WIDTH), BF16),
        scratch_shapes=[pltpu.VMEM((2, 2 * MEM_ROWS, tokens), BF16),
                        pltpu.VMEM((2, 2 * MEM_ROWS, LANES), F32)],
        compiler_params=pltpu.CompilerParams(vmem_limit_bytes=VMEM_LIMIT),
        name="mem_attention",
    )(proj, proj, mk, mv, q_norm_g)


def kernel(x, mem, positions, norm_g, w_in, w_out, mem_norm_g, w_mem_kv, mem_q_norm_g,
           mem_k_norm_g, ret_norm_g):
    b, s, d = x.shape
    tokens = mem.shape[1]
    depth = w_in.shape[0]
    assert d == D_MODEL and w_in.shape[2] == IN_WIDTH
    assert s % SB_BLOCK == 0 and s % RET_CHUNK == 0 and s % MEM_ROWS == 0
    assert (b * s) % PROJ_ROWS == 0

    w_in_b = w_in.astype(BF16)
    w_out_b = w_out.astype(BF16)
    w_kv_b = w_mem_kv.astype(BF16)
    norm_g3 = norm_g[:, None, :]
    ret_norm_g3 = ret_norm_g[:, None, :]
    q_norm_g3 = jnp.tile(mem_q_norm_g, (1, PAIR))[:, None, :]
    k_norm_g3 = jnp.tile(mem_k_norm_g, (1, PAIR))[:, None, :]

    cos, sin = _rope_tables(positions)
    mk_all, mv_all = _mem_kv(mem.reshape(b * tokens, d), mem_norm_g[:, None, :], w_kv_b, k_norm_g3)

    x2 = x.reshape(b * s, d)
    proj = _in_proj(x2, norm_g3, w_in_b, 0, cos, sin)
    for l in range(depth):
        sb_o = _sb_attention(proj, b, s)
        ret_o = _retention(proj, ret_norm_g3, l, b, s)
        mem_o = _mem_attention(proj, mk_all, mv_all, q_norm_g3, l, b, s, tokens)
        if l + 1 < depth:
            x2, proj = _out_in_proj(sb_o, ret_o, mem_o, w_out_b, l, x2, norm_g3, w_in_b, cos, sin)
        else:
            x2 = _out_proj(sb_o, ret_o, mem_o, w_out_b, l, x2)
    return x2.reshape(b, s, d)
```

```python
import functools
import math

import jax
import jax.numpy as jnp
from jax import lax
from jax.experimental import pallas as pl
from jax.experimental.pallas import tpu as pltpu

D_MODEL = 1024
HEAD_DIM = 64
SB_HEADS = 6
RET_HEADS = 6
MEM_HEADS = 4
SB_WIDTH = SB_HEADS * HEAD_DIM
RET_WIDTH = RET_HEADS * HEAD_DIM
MEM_WIDTH = MEM_HEADS * HEAD_DIM
MIX_WIDTH = SB_WIDTH + RET_WIDTH + MEM_WIDTH
IN_WIDTH = 4 * SB_WIDTH + 4 * RET_WIDTH + 2 * MEM_WIDTH
ROPE_BASE = 10000.0
EPS = 1e-6
QK_SCALE = HEAD_DIM ** -0.5
SB_Q_SCALE = QK_SCALE * math.log2(math.e)

LANES = 128
PAIR = LANES // HEAD_DIM
SB_PAIRS = SB_HEADS // PAIR
RET_PAIRS = RET_HEADS // PAIR
MEM_PAIRS = MEM_HEADS // PAIR

COL_SB_Q = 0
COL_SB_K = COL_SB_Q + SB_PAIRS
COL_SB_V = COL_SB_K + SB_PAIRS
COL_SB_G = COL_SB_V + SB_PAIRS
COL_RET_Q = COL_SB_G + SB_PAIRS
COL_RET_K = COL_RET_Q + RET_PAIRS
COL_RET_V = COL_RET_K + RET_PAIRS
COL_RET_G = COL_RET_V + RET_PAIRS
COL_MEM_Q = COL_RET_G + RET_PAIRS
COL_MEM_G = COL_MEM_Q + MEM_PAIRS

PROJ_ROWS = 512
OUT_ROWS = 1024
PROJ_COLS = 512
SB_BLOCK = 256
RET_CHUNK = 128
MEM_ROWS = 512
VMEM_LIMIT = 48 * 1024 * 1024

F32 = jnp.float32
BF16 = jnp.bfloat16


def _dot(a, b):
    return jnp.dot(a, b, preferred_element_type=F32)


def _dot_nt(a, b):
    return lax.dot_general(a, b, (((1,), (1,)), ((), ())), preferred_element_type=F32)


def _dot_tn(a, b):
    return lax.dot_general(a, b, (((0,), (0,)), ((), ())), preferred_element_type=F32)


def _silu(g):
    return g / (1.0 + jnp.exp(-g))


def _head_masks():
    lane = lax.broadcasted_iota(jnp.int32, (1, LANES), 1)
    first = lane < HEAD_DIM
    return first, jnp.logical_not(first)


def _pair_mean_square(t, first):
    sq = t * t
    s_a = jnp.sum(jnp.where(first, sq, 0.0), axis=-1, keepdims=True)
    s_b = jnp.sum(jnp.where(first, 0.0, sq), axis=-1, keepdims=True)
    return jnp.where(first, s_a, s_b) * (1.0 / HEAD_DIM)


def _rope_kernel(pos_ref, invf_ref, cos_ref, sin_ref):
    ang = pos_ref[...].astype(F32) * invf_ref[...]
    lane = lax.broadcasted_iota(jnp.int32, (1, LANES), 1)
    sign = jnp.where((lane % HEAD_DIM) < HEAD_DIM // 2, -1.0, 1.0)
    cos_ref[...] = jnp.cos(ang)
    sin_ref[...] = jnp.sin(ang) * sign


def _rope_tables(positions):
    b, s = positions.shape
    half = HEAD_DIM // 2
    inv_freq = ROPE_BASE ** (-jnp.arange(half, dtype=F32) / half)
    invf = jnp.tile(inv_freq, LANES // half)[None, :]
    pos = positions.reshape(b * s, 1)
    rows = s
    return pl.pallas_call(
        _rope_kernel,
        grid=(b * s // rows,),
        in_specs=[pl.BlockSpec((rows, 1), lambda i: (i, 0)),
                  pl.BlockSpec((1, LANES), lambda i: (0, 0))],
        out_specs=[pl.BlockSpec((rows, LANES), lambda i: (i, 0)),
                   pl.BlockSpec((rows, LANES), lambda i: (i, 0))],
        out_shape=[jax.ShapeDtypeStruct((b * s, LANES), F32)] * 2,
        name="rope_tables",
    )(pos, invf)


def _swap_halves(t):
    lane = lax.broadcasted_iota(jnp.int32, (1, LANES), 1)
    half = HEAD_DIM // 2
    lower = (lane % HEAD_DIM) < half
    return jnp.where(lower, pltpu.roll(t, LANES - half, 1), pltpu.roll(t, half, 1))


def _norm_project(x, g_ref, w_ref, cos_ref, sin_ref, o_ref):
    ms = jnp.mean(x * x, axis=-1, keepdims=True)
    h = (x * lax.rsqrt(ms + EPS) * g_ref[0]).astype(BF16)
    cos = cos_ref[...]
    sin = sin_ref[...]
    blocks_per_dot = PROJ_COLS // LANES
    for j in range(IN_WIDTH // PROJ_COLS):
        y = _dot(h, w_ref[0, :, j * PROJ_COLS:(j + 1) * PROJ_COLS])
        for b in range(blocks_per_dot):
            blk = j * blocks_per_dot + b
            piece = y[:, b * LANES:(b + 1) * LANES]
            if blk < COL_SB_K:
                piece = piece * SB_Q_SCALE
            elif COL_RET_Q <= blk < COL_RET_V:
                piece = piece * cos + _swap_halves(piece) * sin
                if blk >= COL_RET_K:
                    piece = piece * QK_SCALE
            o_ref[:, blk * LANES:(blk + 1) * LANES] = piece.astype(BF16)


def _in_proj_kernel(x_ref, g_ref, w_ref, cos_ref, sin_ref, o_ref):
    _norm_project(x_ref[...], g_ref, w_ref, cos_ref, sin_ref, o_ref)


def _mix(sb_ref, ret_ref, mem_ref, w_ref, x_ref):
    mixed = jnp.concatenate([sb_ref[...], ret_ref[...], mem_ref[...]], axis=1)
    return x_ref[...] + _dot(mixed, w_ref[0])


def _out_proj_kernel(sb_ref, ret_ref, mem_ref, w_ref, x_ref, o_ref):
    o_ref[...] = _mix(sb_ref, ret_ref, mem_ref, w_ref, x_ref)


def _out_in_proj_kernel(sb_ref, ret_ref, mem_ref, w_out_ref, x_ref, g_ref, w_in_ref, cos_ref,
                        sin_ref, x_out_ref, proj_ref):
    x_new = _mix(sb_ref, ret_ref, mem_ref, w_out_ref, x_ref)
    x_out_ref[...] = x_new
    _norm_project(x_new, g_ref, w_in_ref, cos_ref, sin_ref, proj_ref)


def _row_tiles(width, rows=PROJ_ROWS):
    return pl.BlockSpec((rows, width), lambda i: (i, 0))


def _layer_param(shape, layer):
    return pl.BlockSpec((1,) + shape, lambda i: (layer, 0, 0), pipeline_mode=pl.Buffered(1))


def _in_proj(x2, g, w, layer, cos, sin):
    m = x2.shape[0]
    return pl.pallas_call(
        _in_proj_kernel,
        grid=(m // PROJ_ROWS,),
        in_specs=[_row_tiles(D_MODEL), _layer_param((1, D_MODEL), layer),
                  _layer_param((D_MODEL, IN_WIDTH), layer), _row_tiles(LANES), _row_tiles(LANES)],
        out_specs=_row_tiles(IN_WIDTH),
        out_shape=jax.ShapeDtypeStruct((m, IN_WIDTH), BF16),
        compiler_params=pltpu.CompilerParams(vmem_limit_bytes=VMEM_LIMIT),
        name="in_proj",
    )(x2, g, w, cos, sin)


def _out_proj(sb_o, ret_o, mem_o, w, layer, x2):
    m = x2.shape[0]
    return pl.pallas_call(
        _out_proj_kernel,
        grid=(m // OUT_ROWS,),
        in_specs=[_row_tiles(SB_WIDTH, OUT_ROWS), _row_tiles(RET_WIDTH, OUT_ROWS),
                  _row_tiles(MEM_WIDTH, OUT_ROWS), _layer_param((MIX_WIDTH, D_MODEL), layer),
                  _row_tiles(D_MODEL, OUT_ROWS)],
        out_specs=_row_tiles(D_MODEL, OUT_ROWS),
        out_shape=jax.ShapeDtypeStruct((m, D_MODEL), F32),
        compiler_params=pltpu.CompilerParams(vmem_limit_bytes=VMEM_LIMIT),
        name="out_proj",
    )(sb_o, ret_o, mem_o, w, x2)


def _out_in_proj(sb_o, ret_o, mem_o, w_out, layer, x2, g, w_in, cos, sin):
    m = x2.shape[0]
    return pl.pallas_call(
        _out_in_proj_kernel,
        grid=(m // PROJ_ROWS,),
        in_specs=[_row_tiles(SB_WIDTH), _row_tiles(RET_WIDTH), _row_tiles(MEM_WIDTH),
                  _layer_param((MIX_WIDTH, D_MODEL), layer), _row_tiles(D_MODEL),
                  _layer_param((1, D_MODEL), layer + 1), _layer_param((D_MODEL, IN_WIDTH), layer + 1),
                  _row_tiles(LANES), _row_tiles(LANES)],
        out_specs=[_row_tiles(D_MODEL), _row_tiles(IN_WIDTH)],
        out_shape=[jax.ShapeDtypeStruct((m, D_MODEL), F32),
                   jax.ShapeDtypeStruct((m, IN_WIDTH), BF16)],
        compiler_params=pltpu.CompilerParams(vmem_limit_bytes=VMEM_LIMIT),
        name="out_in_proj",
    )(sb_o, ret_o, mem_o, w_out, x2, g, w_in, cos, sin)


SB_LAG = 2
SB_STEP = 4
SB_MASKED = -1e30
SB_DEAD = 160.0
SB_BIAS_NONE, SB_BIAS_DIAG, SB_BIAS_ALL = 0, 1, 2


def _sb_schedule(nq):
    zero_slot, junk_slot = nq, nq + 1
    idle = (0, 0, SB_BIAS_ALL, junk_slot, junk_slot, 1)
    items = [idle] * SB_STEP
    waves = []
    for w in range(nq):
        start = len(items)
        for qb in range(nq - 1, w - 1, -1):
            src = zero_slot if w == 0 else qb
            bias = SB_BIAS_DIAG if w == 0 else SB_BIAS_NONE
            items.append((qb, qb - w, bias, src, qb, 0 if qb > w else 1))
        items += [idle] * (-(len(items) - start) % SB_STEP)
        waves.append((start, (len(items) - start) // SB_STEP))
    return jnp.asarray(items, jnp.int32).T, jnp.asarray(waves, jnp.int32).T


def _sb_kernel(tab_ref, wave_ref, q_ref, k_ref, v_ref, g_ref, o_ref,
               tri_ref, bias_ref, z_ref, sp_ref, w_ref, r_ref, m_ref, acc_ref, *, n_waves):
    t = SB_BLOCK
    first, second = _head_masks()
    n_carry = r_ref.shape[0]

    @pl.when(jnp.logical_and(pl.program_id(0) == 0, pl.program_id(1) == 0))
    def _constants():
        row = lax.broadcasted_iota(jnp.int32, (t, t), 0)
        col = lax.broadcasted_iota(jnp.int32, (t, t), 1)
        tri_ref[...] = (row >= col).astype(BF16)
        bias_ref[SB_BIAS_NONE] = jnp.zeros((t, t), F32)
        bias_ref[SB_BIAS_DIAG] = jnp.where(col < row, 0.0, SB_MASKED)
        bias_ref[SB_BIAS_ALL] = jnp.full((t, t), SB_MASKED, F32)
        r_ref[n_carry - 2] = jnp.zeros(r_ref.shape[1:], F32)

    z_ref[SB_STEP - 1] = jnp.full(z_ref.shape[1:], SB_MASKED, F32)
    sp_ref[SB_STEP - 1] = jnp.zeros(sp_ref.shape[1:], BF16)
    for slot in range(SB_STEP - SB_LAG, SB_STEP):
        w_ref[slot] = jnp.zeros(w_ref.shape[1:], BF16)
    r_ref[n_carry - 1] = jnp.zeros(r_ref.shape[1:], F32)
    m_ref[1] = jnp.zeros(m_ref.shape[1:], F32)
    acc_ref[...] = jnp.zeros_like(acc_ref)

    def rows(blk):
        return pl.ds(pl.multiple_of(blk * t, t), t)

    def scores(j, slot):
        qb, kb = tab_ref[0, j], tab_ref[1, j]
        q = q_ref[rows(qb), :]
        q2 = jnp.concatenate([jnp.where(first, q, 0), jnp.where(second, q, 0)], axis=0)
        bias = bias_ref[tab_ref[2, j]]
        z = _dot_nt(q2, k_ref[rows(kb), :]) + jnp.concatenate([bias, bias], axis=0)
        sp = jnp.maximum(z, 0.0) + jnp.log2(1.0 + jnp.exp2(-jnp.abs(z)))
        z_ref[slot] = z
        sp_ref[slot] = sp.astype(BF16)

    def weights(j, slot):
        cum = _dot(sp_ref[slot], tri_ref[...])
        r = r_ref[tab_ref[3, j]]
        z = z_ref[slot]
        halves = [jnp.exp2(z[:, c:c + LANES] - cum[:, c:c + LANES] - r) for c in (0, LANES)]
        w_ref[slot] = jnp.concatenate(halves, axis=1).astype(BF16)
        r_new = r + cum[:, 0:1]
        r_ref[tab_ref[4, j]] = r_new
        m_slot = tab_ref[5, j]
        m_ref[m_slot] = jnp.minimum(m_ref[m_slot], r_new)

    def values(j, slot):
        qb, kb = tab_ref[0, j], tab_ref[1, j]
        pv = _dot(w_ref[slot], v_ref[rows(kb), :])
        acc_ref[rows(qb), :] += jnp.where(first, pv[:t], pv[t:])

    def wave(state):
        w = state[0]
        start, n_steps = wave_ref[0, w], wave_ref[1, w]
        m_ref[0] = jnp.full(m_ref.shape[1:], jnp.inf, F32)

        def step(n, _):
            for i in range(SB_STEP):
                c = start + SB_STEP * n + i
                values(c - 2, (i - 2) % SB_STEP)
                weights(c - 1, (i - 1) % SB_STEP)
                scores(c, i)
            return 0

        lax.fori_loop(0, n_steps, step, 0)
        return w + 1, start + SB_STEP * n_steps, jnp.min(m_ref[0])

    _, end, _ = lax.while_loop(lambda s: jnp.logical_and(s[0] < n_waves, s[2] < SB_DEAD), wave,
                               (jnp.int32(0), jnp.int32(SB_STEP), jnp.float32(0.0)))
    values(end - 2, SB_STEP - 2)
    weights(end - 1, SB_STEP - 1)
    values(end - 1, SB_STEP - 1)
    o_ref[...] = (acc_ref[...] * _silu(g_ref[...].astype(F32))).astype(BF16)


def _sb_attention(proj, b, s):
    nq = s // SB_BLOCK
    table, waves = _sb_schedule(nq)
    t = SB_BLOCK
    seq = lambda col: pl.BlockSpec((s, LANES), lambda bi, p, tab, wav: (bi, col + p))
    return pl.pallas_call(
        functools.partial(_sb_kernel, n_waves=nq),
        grid_spec=pltpu.PrefetchScalarGridSpec(
            num_scalar_prefetch=2,
            grid=(b, SB_PAIRS),
            in_specs=[seq(COL_SB_Q), seq(COL_SB_K), seq(COL_SB_V), seq(COL_SB_G)],
            out_specs=pl.BlockSpec((s, LANES), lambda bi, p, tab, wav: (bi, p)),
            scratch_shapes=[pltpu.VMEM((t, t), BF16),
                            pltpu.VMEM((3, t, t), F32),
                            pltpu.VMEM((SB_STEP, 2 * t, t), F32),
                            pltpu.VMEM((SB_STEP, 2 * t, t), BF16),
                            pltpu.VMEM((SB_STEP, 2 * t, t), BF16),
                            pltpu.VMEM((nq + 2, 2 * t, LANES), F32),
                            pltpu.VMEM((2, 2 * t, LANES), F32),
                            pltpu.VMEM((s, LANES), F32)]),
        out_shape=jax.ShapeDtypeStruct((b * s, SB_WIDTH), BF16),
        compiler_params=pltpu.CompilerParams(
            dimension_semantics=("arbitrary", "arbitrary"),
            vmem_limit_bytes=VMEM_LIMIT),
        name="sb_attention",
    )(table, waves, proj, proj, proj, proj)


def _ret_kernel(q_ref, k_ref, v_ref, g_ref, ng_ref, lg_ref, o_ref, kv_ref, state_ref):
    c = RET_CHUNK
    first, second = _head_masks()
    nchunks = q_ref.shape[0] // c

    lg_lane = lg_ref[0]
    lg_a = lg_lane[:, 0:1]
    lg_b = lg_lane[:, HEAD_DIM:HEAD_DIM + 1]

    ri = lax.broadcasted_iota(jnp.int32, (c, c), 0)
    ci = lax.broadcasted_iota(jnp.int32, (c, c), 1)
    diff = (ri - ci).astype(F32)
    keep = ri >= ci
    decay_a = jnp.where(keep, jnp.exp(lg_a * jnp.maximum(diff, 0.0)), 0.0)
    decay_b = jnp.where(keep, jnp.exp(lg_b * jnp.maximum(diff, 0.0)), 0.0)
    idx = lax.broadcasted_iota(jnp.int32, (c, LANES), 0).astype(F32)
    q_decay = jnp.exp(lg_lane * (idx + 1.0))
    k_decay = jnp.exp(lg_lane * (c - 1.0 - idx))
    si = lax.broadcasted_iota(jnp.int32, (LANES, LANES), 0)
    sj = lax.broadcasted_iota(jnp.int32, (LANES, LANES), 1)
    same_head = (si < HEAD_DIM) == (sj < HEAD_DIM)
    state_decay = jnp.where(same_head, jnp.exp(lg_lane * float(c)), 0.0)
    norm_g = ng_ref[0]

    chunk_rows = [slice(n * c, (n + 1) * c) for n in range(nchunks)]

    for n, rows in enumerate(chunk_rows):
        kd = (k_ref[rows, :].astype(F32) * k_decay).astype(BF16)
        kv_ref[n] = jnp.where(same_head, _dot_tn(kd, v_ref[rows, :]), 0.0)

    state = jnp.zeros((LANES, LANES), F32)
    for n in range(nchunks):
        state_ref[n] = state.astype(BF16)
        state = state * state_decay + kv_ref[n]

    decay2 = jnp.concatenate([decay_a, decay_b], axis=1)
    for n, rows in enumerate(chunk_rows):
        qb = q_ref[rows, :]
        kb = k_ref[rows, :]
        v = v_ref[rows, :]
        k2 = jnp.concatenate([jnp.where(first, kb, 0), jnp.where(second, kb, 0)], axis=0)
        v2 = jnp.concatenate([jnp.where(first, v, 0), jnp.where(second, v, 0)], axis=0)
        scores = (_dot_nt(qb, k2) * decay2).astype(BF16)
        out = _dot(scores, v2) + _dot(qb, state_ref[n]) * q_decay
        inv = lax.rsqrt(_pair_mean_square(out, first) + EPS)
        y = out * inv * norm_g
        o_ref[rows, :] = (y * _silu(g_ref[rows, :].astype(F32))).astype(BF16)


def _retention(proj, ret_norm_g, layer, b, s):
    seq = lambda col: pl.BlockSpec((s, LANES), lambda bi, p: (bi, col + p))
    gamma = 1.0 - 2.0 ** (-5.0 - jnp.arange(RET_HEADS, dtype=F32))
    log_gamma = jnp.repeat(jnp.log(gamma), HEAD_DIM).reshape(RET_PAIRS, 1, LANES)
    return pl.pallas_call(
        _ret_kernel,
        grid=(b, RET_PAIRS),
        in_specs=[seq(COL_RET_Q), seq(COL_RET_K), seq(COL_RET_V), seq(COL_RET_G),
                  pl.BlockSpec((1, 1, LANES), lambda bi, p: (layer, 0, p)),
                  pl.BlockSpec((1, 1, LANES), lambda bi, p: (p, 0, 0))],
        out_specs=pl.BlockSpec((s, LANES), lambda bi, p: (bi, p)),
        out_shape=jax.ShapeDtypeStruct((b * s, RET_WIDTH), BF16),
        scratch_shapes=[pltpu.VMEM((s // RET_CHUNK, LANES, LANES), F32),
                        pltpu.VMEM((s // RET_CHUNK, LANES, LANES), BF16)],
        compiler_params=pltpu.CompilerParams(vmem_limit_bytes=VMEM_LIMIT),
        name="retention",
    )(proj, proj, proj, proj, ret_norm_g, log_gamma)


def _mem_kv_kernel(mem_ref, g_ref, w_ref, kg_ref, k_ref, v_ref):
    first, _ = _head_masks()
    x = mem_ref[...]
    ms = jnp.mean(x * x, axis=-1, keepdims=True)
    h = (x * lax.rsqrt(ms + EPS) * g_ref[0]).astype(BF16)
    kv = _dot(h, w_ref[0])
    kg = kg_ref[0]
    for pair in range(MEM_PAIRS):
        cols = slice(pair * LANES, (pair + 1) * LANES)
        kp = kv[:, cols]
        kn = kp * lax.rsqrt(_pair_mean_square(kp, first) + EPS) * kg
        k_ref[0, :, cols] = kn.astype(BF16)
    v_ref[0] = kv[:, MEM_WIDTH:].astype(BF16)


def _mem_kv(mem2, mem_norm_g, w_mem_kv, k_norm_g):
    depth = w_mem_kv.shape[0]
    rows = mem2.shape[0]
    out = jax.ShapeDtypeStruct((depth, rows, MEM_WIDTH), BF16)
    return pl.pallas_call(
        _mem_kv_kernel,
        grid=(depth,),
        in_specs=[pl.BlockSpec((rows, D_MODEL), lambda l: (0, 0)),
                  pl.BlockSpec((1, 1, D_MODEL), lambda l: (l, 0, 0)),
                  pl.BlockSpec((1, D_MODEL, 2 * MEM_WIDTH), lambda l: (l, 0, 0)),
                  pl.BlockSpec((1, 1, LANES), lambda l: (l, 0, 0))],
        out_specs=[pl.BlockSpec((1, rows, MEM_WIDTH), lambda l: (l, 0, 0)),
                   pl.BlockSpec((1, rows, MEM_WIDTH), lambda l: (l, 0, 0))],
        out_shape=[out, out],
        compiler_params=pltpu.CompilerParams(vmem_limit_bytes=VMEM_LIMIT),
        name="mem_kv",
    )(mem2, mem_norm_g, w_mem_kv, k_norm_g)


def _mem_attn_kernel(q_ref, g_ref, k_ref, v_ref, qg_ref, o_ref, e_ref, den_ref):
    first, second = _head_masks()
    tile = MEM_ROWS
    n_tiles = q_ref.shape[0] // tile
    k = k_ref[0]
    v = v_ref[0]

    def probabilities(i, slot):
        rows = slice(i * tile, (i + 1) * tile)
        q = q_ref[rows, :].astype(F32)
        inv = lax.rsqrt(_pair_mean_square(q, first) + EPS) * QK_SCALE
        qn = (q * inv * qg_ref[0]).astype(BF16)
        q2 = jnp.concatenate([jnp.where(first, qn, 0), jnp.where(second, qn, 0)], axis=0)
        sc = _dot_nt(q2, k)
        e = jnp.exp(sc - jnp.max(sc, axis=-1, keepdims=True))
        den_ref[slot] = jnp.broadcast_to(jnp.sum(e, axis=-1, keepdims=True), (2 * tile, LANES))
        e_ref[slot] = e.astype(BF16)

    def output(i, slot):
        rows = slice(i * tile, (i + 1) * tile)
        pv = _dot(e_ref[slot], v) / den_ref[slot]
        out = jnp.where(first, pv[:tile], pv[tile:])
        o_ref[rows, :] = (out * _silu(g_ref[rows, :].astype(F32))).astype(BF16)

    for i in range(n_tiles + 1):
        if i >= 1:
            output(i - 1, (i - 1) % 2)
        if i < n_tiles:
            probabilities(i, i % 2)


def _mem_attention(proj, mk, mv, q_norm_g, layer, b, s, tokens):
    seq = lambda col: pl.BlockSpec((s, LANES), lambda bi, p: (bi, col + p))
    kv_spec = pl.BlockSpec((1, tokens, LANES), lambda bi, p: (layer, bi, p))
    return pl.pallas_call(
        _mem_attn_kernel,
        grid=(b, MEM_PAIRS),
        in_specs=[seq(COL_MEM_Q), seq(COL_MEM_G), kv_spec, kv_spec,
                  pl.BlockSpec((1, 1, LANES), lambda bi, p: (layer, 0, 0))],
        out_specs=pl.BlockSpec((s, LANES), lambda bi, p: (bi, p)),
        out_shape=jax.ShapeDtypeStruct((b * s, MEM_WIDTH), BF16),
        scratch_shapes=[pltpu.VMEM((2, 2 * MEM_ROWS, tokens), BF16),
                        pltpu.VMEM((2, 2 * MEM_ROWS, LANES), F32)],
        compiler_params=pltpu.CompilerParams(vmem_limit_bytes=VMEM_LIMIT),
        name="mem_attention",
    )(proj, proj, mk, mv, q_norm_g)


def kernel(x, mem, positions, norm_g, w_in, w_out, mem_norm_g, w_mem_kv, mem_q_norm_g,
           mem_k_norm_g, ret_norm_g):
    b, s, d = x.shape
    tokens = mem.shape[1]
    depth = w_in.shape[0]
    assert d == D_MODEL and w_in.shape[2] == IN_WIDTH
    assert s % SB_BLOCK == 0 and s % RET_CHUNK == 0 and s % MEM_ROWS == 0
    assert (b * s) % PROJ_ROWS == 0 and (b * s) % OUT_ROWS == 0

    w_in_b = w_in.astype(BF16)
    w_out_b = w_out.astype(BF16)
    w_kv_b = w_mem_kv.astype(BF16)
    norm_g3 = norm_g[:, None, :]
    ret_norm_g3 = ret_norm_g[:, None, :]
    q_norm_g3 = jnp.tile(mem_q_norm_g, (1, PAIR))[:, None, :]
    k_norm_g3 = jnp.tile(mem_k_norm_g, (1, PAIR))[:, None, :]

    cos, sin = _rope_tables(positions)
    mk_all, mv_all = _mem_kv(mem.reshape(b * tokens, d), mem_norm_g[:, None, :], w_kv_b, k_norm_g3)

    x2 = x.reshape(b * s, d)
    proj = _in_proj(x2, norm_g3, w_in_b, 0, cos, sin)
    for l in range(depth):
        sb_o = _sb_attention(proj, b, s)
        ret_o = _retention(proj, ret_norm_g3, l, b, s)
        mem_o = _mem_attention(proj, mk_all, mv_all, q_norm_g3, l, b, s, tokens)
        if l + 1 < depth:
            x2, proj = _out_in_proj(sb_o, ret_o, mem_o, w_out_b, l, x2, norm_g3, w_in_b, cos, sin)
        else:
            x2 = _out_proj(sb_o, ret_o, mem_o, w_out_b, l, x2)
    return x2.reshape(b, s, d)
```

```python
import functools
import math

import jax
import jax.numpy as jnp
from jax import lax
from jax.experimental import pallas as pl
from jax.experimental.pallas import tpu as pltpu

D_MODEL = 1024
HEAD_DIM = 64
SB_HEADS = 6
RET_HEADS = 6
MEM_HEADS = 4
SB_WIDTH = SB_HEADS * HEAD_DIM
RET_WIDTH = RET_HEADS * HEAD_DIM
MEM_WIDTH = MEM_HEADS * HEAD_DIM
MIX_WIDTH = SB_WIDTH + RET_WIDTH + MEM_WIDTH
IN_WIDTH = 4 * SB_WIDTH + 4 * RET_WIDTH + 2 * MEM_WIDTH
ROPE_BASE = 10000.0
EPS = 1e-6
QK_SCALE = HEAD_DIM ** -0.5
SB_Q_SCALE = QK_SCALE * math.log2(math.e)

LANES = 128
PAIR = LANES // HEAD_DIM
SB_PAIRS = SB_HEADS // PAIR
RET_PAIRS = RET_HEADS // PAIR
MEM_PAIRS = MEM_HEADS // PAIR

COL_SB_Q = 0
COL_SB_K = COL_SB_Q + SB_PAIRS
COL_SB_V = COL_SB_K + SB_PAIRS
COL_SB_G = COL_SB_V + SB_PAIRS
COL_RET_Q = COL_SB_G + SB_PAIRS
COL_RET_K = COL_RET_Q + RET_PAIRS
COL_RET_V = COL_RET_K + RET_PAIRS
COL_RET_G = COL_RET_V + RET_PAIRS
COL_MEM_Q = COL_RET_G + RET_PAIRS
COL_MEM_G = COL_MEM_Q + MEM_PAIRS

PROJ_ROWS = 512
OUT_ROWS = 1024
PROJ_COLS = 512
SB_BLOCK = 256
RET_CHUNK = 128
MEM_ROWS = 512
VMEM_LIMIT = 48 * 1024 * 1024

F32 = jnp.float32
BF16 = jnp.bfloat16


def _dot(a, b):
    return jnp.dot(a, b, preferred_element_type=F32)


def _dot_nt(a, b):
    return lax.dot_general(a, b, (((1,), (1,)), ((), ())), preferred_element_type=F32)


def _dot_tn(a, b):
    return lax.dot_general(a, b, (((0,), (0,)), ((), ())), preferred_element_type=F32)


def _silu(g):
    return g / (1.0 + jnp.exp(-g))


def _head_masks():
    lane = lax.broadcasted_iota(jnp.int32, (1, LANES), 1)
    first = lane < HEAD_DIM
    return first, jnp.logical_not(first)


def _pair_mean_square(t, first):
    sq = t * t
    s_a = jnp.sum(jnp.where(first, sq, 0.0), axis=-1, keepdims=True)
    s_b = jnp.sum(jnp.where(first, 0.0, sq), axis=-1, keepdims=True)
    return jnp.where(first, s_a, s_b) * (1.0 / HEAD_DIM)


def _rope_kernel(pos_ref, invf_ref, cos_ref, sin_ref):
    ang = pos_ref[...].astype(F32) * invf_ref[...]
    lane = lax.broadcasted_iota(jnp.int32, (1, LANES), 1)
    sign = jnp.where((lane % HEAD_DIM) < HEAD_DIM // 2, -1.0, 1.0)
    cos_ref[...] = jnp.cos(ang)
    sin_ref[...] = jnp.sin(ang) * sign


def _rope_tables(positions):
    b, s = positions.shape
    half = HEAD_DIM // 2
    inv_freq = ROPE_BASE ** (-jnp.arange(half, dtype=F32) / half)
    invf = jnp.tile(inv_freq, LANES // half)[None, :]
    pos = positions.reshape(b * s, 1)
    rows = s
    return pl.pallas_call(
        _rope_kernel,
        grid=(b * s // rows,),
        in_specs=[pl.BlockSpec((rows, 1), lambda i: (i, 0)),
                  pl.BlockSpec((1, LANES), lambda i: (0, 0))],
        out_specs=[pl.BlockSpec((rows, LANES), lambda i: (i, 0)),
                   pl.BlockSpec((rows, LANES), lambda i: (i, 0))],
        out_shape=[jax.ShapeDtypeStruct((b * s, LANES), F32)] * 2,
        name="rope_tables",
    )(pos, invf)


def _swap_halves(t):
    lane = lax.broadcasted_iota(jnp.int32, (1, LANES), 1)
    half = HEAD_DIM // 2
    lower = (lane % HEAD_DIM) < half
    return jnp.where(lower, pltpu.roll(t, LANES - half, 1), pltpu.roll(t, half, 1))


def _norm_project(x, g_ref, w_ref, cos_ref, sin_ref, o_ref):
    ms = jnp.mean(x * x, axis=-1, keepdims=True)
    h = (x * lax.rsqrt(ms + EPS) * g_ref[0]).astype(BF16)
    cos = cos_ref[...]
    sin = sin_ref[...]
    blocks_per_dot = PROJ_COLS // LANES
    for j in range(IN_WIDTH // PROJ_COLS):
        y = _dot(h, w_ref[0, :, j * PROJ_COLS:(j + 1) * PROJ_COLS])
        for b in range(blocks_per_dot):
            blk = j * blocks_per_dot + b
            piece = y[:, b * LANES:(b + 1) * LANES]
            if blk < COL_SB_K:
                piece = piece * SB_Q_SCALE
            elif COL_RET_Q <= blk < COL_RET_V:
                piece = piece * cos + _swap_halves(piece) * sin
                if blk >= COL_RET_K:
                    piece = piece * QK_SCALE
            o_ref[:, blk * LANES:(blk + 1) * LANES] = piece.astype(BF16)


def _in_proj_kernel(x_ref, g_ref, w_ref, cos_ref, sin_ref, o_ref):
    _norm_project(x_ref[...], g_ref, w_ref, cos_ref, sin_ref, o_ref)


def _mix(sb_ref, ret_ref, mem_ref, w_ref, x_ref):
    mixed = jnp.concatenate([sb_ref[...], ret_ref[...], mem_ref[...]], axis=1)
    return x_ref[...] + _dot(mixed, w_ref[0])


def _out_proj_kernel(sb_ref, ret_ref, mem_ref, w_ref, x_ref, o_ref):
    o_ref[...] = _mix(sb_ref, ret_ref, mem_ref, w_ref, x_ref)


def _out_in_proj_kernel(sb_ref, ret_ref, mem_ref, w_out_ref, x_ref, g_ref, w_in_ref, cos_ref,
                        sin_ref, x_out_ref, proj_ref):
    x_new = _mix(sb_ref, ret_ref, mem_ref, w_out_ref, x_ref)
    x_out_ref[...] = x_new
    _norm_project(x_new, g_ref, w_in_ref, cos_ref, sin_ref, proj_ref)


def _row_tiles(width, rows=PROJ_ROWS):
    return pl.BlockSpec((rows, width), lambda i: (i, 0))


def _layer_param(shape, layer):
    return pl.BlockSpec((1,) + shape, lambda i: (layer, 0, 0), pipeline_mode=pl.Buffered(1))


def _in_proj(x2, g, w, layer, cos, sin):
    m = x2.shape[0]
    return pl.pallas_call(
        _in_proj_kernel,
        grid=(m // PROJ_ROWS,),
        in_specs=[_row_tiles(D_MODEL), _layer_param((1, D_MODEL), layer),
                  _layer_param((D_MODEL, IN_WIDTH), layer), _row_tiles(LANES), _row_tiles(LANES)],
        out_specs=_row_tiles(IN_WIDTH),
        out_shape=jax.ShapeDtypeStruct((m, IN_WIDTH), BF16),
        compiler_params=pltpu.CompilerParams(vmem_limit_bytes=VMEM_LIMIT),
        name="in_proj",
    )(x2, g, w, cos, sin)


def _out_proj(sb_o, ret_o, mem_o, w, layer, x2):
    m = x2.shape[0]
    return pl.pallas_call(
        _out_proj_kernel,
        grid=(m // OUT_ROWS,),
        in_specs=[_row_tiles(SB_WIDTH, OUT_ROWS), _row_tiles(RET_WIDTH, OUT_ROWS),
                  _row_tiles(MEM_WIDTH, OUT_ROWS), _layer_param((MIX_WIDTH, D_MODEL), layer),
                  _row_tiles(D_MODEL, OUT_ROWS)],
        out_specs=_row_tiles(D_MODEL, OUT_ROWS),
        out_shape=jax.ShapeDtypeStruct((m, D_MODEL), F32),
        compiler_params=pltpu.CompilerParams(vmem_limit_bytes=VMEM_LIMIT),
        name="out_proj",
    )(sb_o, ret_o, mem_o, w, x2)


def _out_in_proj(sb_o, ret_o, mem_o, w_out, layer, x2, g, w_in, cos, sin):
    m = x2.shape[0]
    return pl.pallas_call(
        _out_in_proj_kernel,
        grid=(m // PROJ_ROWS,),
        in_specs=[_row_tiles(SB_WIDTH), _row_tiles(RET_WIDTH), _row_tiles(MEM_WIDTH),
                  _layer_param((MIX_WIDTH, D_MODEL), layer), _row_tiles(D_MODEL),
                  _layer_param((1, D_MODEL), layer + 1), _layer_param((D_MODEL, IN_WIDTH), layer + 1),
                  _row_tiles(LANES), _row_tiles(LANES)],
        out_specs=[_row_tiles(D_MODEL), _row_tiles(IN_WIDTH)],
        out_shape=[jax.ShapeDtypeStruct((m, D_MODEL), F32),
                   jax.ShapeDtypeStruct((m, IN_WIDTH), BF16)],
        compiler_params=pltpu.CompilerParams(vmem_limit_bytes=VMEM_LIMIT),
        name="out_in_proj",
    )(sb_o, ret_o, mem_o, w_out, x2, g, w_in, cos, sin)


SB_LAG = 2
SB_STEP = 4
SB_MASKED = -1e30
SB_DEAD = 160.0
SB_IDLE_CARRY = 1e30


def _sb_schedule(nq):
    zero_slot, junk_slot = nq, nq + 1
    idle = (0, 0, junk_slot, junk_slot, 1)
    items = [idle] * SB_STEP
    waves = []
    for w in range(nq):
        start = len(items)
        for qb in range(nq - 1, w - 1, -1):
            src = zero_slot if w == 0 else qb
            items.append((qb, qb - w, src, qb, 0 if qb > w else 1))
        items += [idle] * (-(len(items) - start) % SB_STEP)
        waves.append((start, (len(items) - start) // SB_STEP))
    return jnp.asarray(items, jnp.int32).T, jnp.asarray(waves, jnp.int32).T, waves[0][1]


def _sb_kernel(tab_ref, wave_ref, q_ref, k_ref, v_ref, g_ref, o_ref,
               tri_ref, bias_ref, z_ref, sp_ref, w_ref, r_ref, m_ref, acc_ref, *, n_waves,
               first_wave_steps):
    t = SB_BLOCK
    first, second = _head_masks()
    n_carry = r_ref.shape[0]

    @pl.when(jnp.logical_and(pl.program_id(0) == 0, pl.program_id(1) == 0))
    def _constants():
        row = lax.broadcasted_iota(jnp.int32, (t, t), 0)
        col = lax.broadcasted_iota(jnp.int32, (t, t), 1)
        tri_ref[...] = (row >= col).astype(BF16)
        bias_ref[...] = jnp.where(col < row, 0.0, SB_MASKED)
        r_ref[n_carry - 2] = jnp.zeros(r_ref.shape[1:], F32)

    z_ref[SB_STEP - 1] = jnp.full(z_ref.shape[1:], SB_MASKED, F32)
    sp_ref[SB_STEP - 1] = jnp.zeros(sp_ref.shape[1:], BF16)
    for slot in range(SB_STEP - SB_LAG, SB_STEP):
        w_ref[slot] = jnp.zeros(w_ref.shape[1:], BF16)
    r_ref[n_carry - 1] = jnp.full(r_ref.shape[1:], SB_IDLE_CARRY, F32)
    m_ref[1] = jnp.zeros(m_ref.shape[1:], F32)
    acc_ref[...] = jnp.zeros_like(acc_ref)

    def rows(blk):
        return pl.ds(pl.multiple_of(blk * t, t), t)

    def scores(j, slot, diagonal):
        qb, kb = tab_ref[0, j], tab_ref[1, j]
        q = q_ref[rows(qb), :]
        q2 = jnp.concatenate([jnp.where(first, q, 0), jnp.where(second, q, 0)], axis=0)
        z = _dot_nt(q2, k_ref[rows(kb), :])
        if diagonal:
            bias = bias_ref[...]
            z = z + jnp.concatenate([bias, bias], axis=0)
        sp = jnp.maximum(z, 0.0) + jnp.log2(1.0 + jnp.exp2(-jnp.abs(z)))
        z_ref[slot] = z
        sp_ref[slot] = sp.astype(BF16)

    def weights(j, slot, track_minimum):
        cum = _dot(sp_ref[slot], tri_ref[...])
        r = r_ref[tab_ref[2, j]]
        z = z_ref[slot]
        halves = [jnp.exp2(z[:, c:c + LANES] - cum[:, c:c + LANES] - r) for c in (0, LANES)]
        w_ref[slot] = jnp.concatenate(halves, axis=1).astype(BF16)
        r_new = r + cum[:, 0:1]
        r_ref[tab_ref[3, j]] = r_new
        if track_minimum:
            m_slot = tab_ref[4, j]
            m_ref[m_slot] = jnp.minimum(m_ref[m_slot], r_new)

    def values(j, slot):
        qb, kb = tab_ref[0, j], tab_ref[1, j]
        pv = _dot(w_ref[slot], v_ref[rows(kb), :])
        acc_ref[rows(qb), :] += jnp.where(first, pv[:t], pv[t:])

    def run_wave(start, n_steps, diagonal):
        def step(n, _):
            for i in range(SB_STEP):
                c = start + SB_STEP * n + i
                values(c - 2, (i - 2) % SB_STEP)
                weights(c - 1, (i - 1) % SB_STEP, not diagonal)
                scores(c, i, diagonal)
            return 0

        lax.fori_loop(0, n_steps, step, 0)
        return start + SB_STEP * n_steps

    def later_wave(state):
        w = state[0]
        m_ref[0] = jnp.full(m_ref.shape[1:], jnp.inf, F32)
        end = run_wave(wave_ref[0, w], wave_ref[1, w], False)
        return w + 1, end, jnp.min(m_ref[0])

    end = run_wave(SB_STEP, first_wave_steps, True)
    _, end, _ = lax.while_loop(lambda s: jnp.logical_and(s[0] < n_waves, s[2] < SB_DEAD),
                               later_wave, (jnp.int32(1), jnp.int32(end), jnp.float32(0.0)))
    values(end - 2, SB_STEP - 2)
    weights(end - 1, SB_STEP - 1, False)
    values(end - 1, SB_STEP - 1)
    o_ref[...] = (acc_ref[...] * _silu(g_ref[...].astype(F32))).astype(BF16)


def _sb_attention(proj, b, s):
    nq = s // SB_BLOCK
    table, waves, first_wave_steps = _sb_schedule(nq)
    t = SB_BLOCK
    seq = lambda col: pl.BlockSpec((s, LANES), lambda bi, p, tab, wav: (bi, col + p))
    return pl.pallas_call(
        functools.partial(_sb_kernel, n_waves=nq, first_wave_steps=first_wave_steps),
        grid_spec=pltpu.PrefetchScalarGridSpec(
            num_scalar_prefetch=2,
            grid=(b, SB_PAIRS),
            in_specs=[seq(COL_SB_Q), seq(COL_SB_K), seq(COL_SB_V), seq(COL_SB_G)],
            out_specs=pl.BlockSpec((s, LANES), lambda bi, p, tab, wav: (bi, p)),
            scratch_shapes=[pltpu.VMEM((t, t), BF16),
                            pltpu.VMEM((t, t), F32),
                            pltpu.VMEM((SB_STEP, 2 * t, t), F32),
                            pltpu.VMEM((SB_STEP, 2 * t, t), BF16),
                            pltpu.VMEM((SB_STEP, 2 * t, t), BF16),
                            pltpu.VMEM((nq + 2, 2 * t, LANES), F32),
                            pltpu.VMEM((2, 2 * t, LANES), F32),
                            pltpu.VMEM((s, LANES), F32)]),
        out_shape=jax.ShapeDtypeStruct((b * s, SB_WIDTH), BF16),
        compiler_params=pltpu.CompilerParams(
            dimension_semantics=("arbitrary", "arbitrary"),
            vmem_limit_bytes=VMEM_LIMIT),
        name="sb_attention",
    )(table, waves, proj, proj, proj, proj)


def _ret_kernel(q_ref, k_ref, v_ref, g_ref, ng_ref, lg_ref, o_ref, kv_ref, state_ref):
    c = RET_CHUNK
    first, second = _head_masks()
    nchunks = q_ref.shape[0] // c

    lg_lane = lg_ref[0]
    lg_a = lg_lane[:, 0:1]
    lg_b = lg_lane[:, HEAD_DIM:HEAD_DIM + 1]

    ri = lax.broadcasted_iota(jnp.int32, (c, c), 0)
    ci = lax.broadcasted_iota(jnp.int32, (c, c), 1)
    diff = (ri - ci).astype(F32)
    keep = ri >= ci
    decay_a = jnp.where(keep, jnp.exp(lg_a * jnp.maximum(diff, 0.0)), 0.0)
    decay_b = jnp.where(keep, jnp.exp(lg_b * jnp.maximum(diff, 0.0)), 0.0)
    idx = lax.broadcasted_iota(jnp.int32, (c, LANES), 0).astype(F32)
    q_decay = jnp.exp(lg_lane * (idx + 1.0))
    k_decay = jnp.exp(lg_lane * (c - 1.0 - idx))
    si = lax.broadcasted_iota(jnp.int32, (LANES, LANES), 0)
    sj = lax.broadcasted_iota(jnp.int32, (LANES, LANES), 1)
    same_head = (si < HEAD_DIM) == (sj < HEAD_DIM)
    state_decay = jnp.where(same_head, jnp.exp(lg_lane * float(c)), 0.0)
    norm_g = ng_ref[0]

    chunk_rows = [slice(n * c, (n + 1) * c) for n in range(nchunks)]

    for n, rows in enumerate(chunk_rows):
        kd = (k_ref[rows, :].astype(F32) * k_decay).astype(BF16)
        kv_ref[n] = jnp.where(same_head, _dot_tn(kd, v_ref[rows, :]), 0.0)

    state = jnp.zeros((LANES, LANES), F32)
    for n in range(nchunks):
        state_ref[n] = state.astype(BF16)
        state = state * state_decay + kv_ref[n]

    decay2 = jnp.concatenate([decay_a, decay_b], axis=1)
    for n, rows in enumerate(chunk_rows):
        qb = q_ref[rows, :]
        kb = k_ref[rows, :]
        v = v_ref[rows, :]
        k2 = jnp.concatenate([jnp.where(first, kb, 0), jnp.where(second, kb, 0)], axis=0)
        v2 = jnp.concatenate([jnp.where(first, v, 0), jnp.where(second, v, 0)], axis=0)
        scores = (_dot_nt(qb, k2) * decay2).astype(BF16)
        out = _dot(scores, v2) + _dot(qb, state_ref[n]) * q_decay
        inv = lax.rsqrt(_pair_mean_square(out, first) + EPS)
        y = out * inv * norm_g
        o_ref[rows, :] = (y * _silu(g_ref[rows, :].astype(F32))).astype(BF16)


def _retention(proj, ret_norm_g, layer, b, s):
    seq = lambda col: pl.BlockSpec((s, LANES), lambda bi, p: (bi, col + p))
    gamma = 1.0 - 2.0 ** (-5.0 - jnp.arange(RET_HEADS, dtype=F32))
    log_gamma = jnp.repeat(jnp.log(gamma), HEAD_DIM).reshape(RET_PAIRS, 1, LANES)
    return pl.pallas_call(
        _ret_kernel,
        grid=(b, RET_PAIRS),
        in_specs=[seq(COL_RET_Q), seq(COL_RET_K), seq(COL_RET_V), seq(COL_RET_G),
                  pl.BlockSpec((1, 1, LANES), lambda bi, p: (layer, 0, p)),
                  pl.BlockSpec((1, 1, LANES), lambda bi, p: (p, 0, 0))],
        out_specs=pl.BlockSpec((s, LANES), lambda bi, p: (bi, p)),
        out_shape=jax.ShapeDtypeStruct((b * s, RET_WIDTH), BF16),
        scratch_shapes=[pltpu.VMEM((s // RET_CHUNK, LANES, LANES), F32),
                        pltpu.VMEM((s // RET_CHUNK, LANES, LANES), BF16)],
        compiler_params=pltpu.CompilerParams(vmem_limit_bytes=VMEM_LIMIT),
        name="retention",
    )(proj, proj, proj, proj, ret_norm_g, log_gamma)


def _mem_kv_kernel(mem_ref, g_ref, w_ref, kg_ref, k_ref, v_ref):
    first, _ = _head_masks()
    x = mem_ref[...]
    ms = jnp.mean(x * x, axis=-1, keepdims=True)
    h = (x * lax.rsqrt(ms + EPS) * g_ref[0]).astype(BF16)
    kv = _dot(h, w_ref[0])
    kg = kg_ref[0]
    for pair in range(MEM_PAIRS):
        cols = slice(pair * LANES, (pair + 1) * LANES)
        kp = kv[:, cols]
        kn = kp * lax.rsqrt(_pair_mean_square(kp, first) + EPS) * kg
        k_ref[0, :, cols] = kn.astype(BF16)
    v_ref[0] = kv[:, MEM_WIDTH:].astype(BF16)


def _mem_kv(mem2, mem_norm_g, w_mem_kv, k_norm_g):
    depth = w_mem_kv.shape[0]
    rows = mem2.shape[0]
    out = jax.ShapeDtypeStruct((depth, rows, MEM_WIDTH), BF16)
    return pl.pallas_call(
        _mem_kv_kernel,
        grid=(depth,),
        in_specs=[pl.BlockSpec((rows, D_MODEL), lambda l: (0, 0)),
                  pl.BlockSpec((1, 1, D_MODEL), lambda l: (l, 0, 0)),
                  pl.BlockSpec((1, D_MODEL, 2 * MEM_WIDTH), lambda l: (l, 0, 0)),
                  pl.BlockSpec((1, 1, LANES), lambda l: (l, 0, 0))],
        out_specs=[pl.BlockSpec((1, rows, MEM_WIDTH), lambda l: (l, 0, 0)),
                   pl.BlockSpec((1, rows, MEM_WIDTH), lambda l: (l, 0, 0))],
        out_shape=[out, out],
        compiler_params=pltpu.CompilerParams(vmem_limit_bytes=VMEM_LIMIT),
        name="mem_kv",
    )(mem2, mem_norm_g, w_mem_kv, k_norm_g)


def _mem_attn_kernel(q_ref, g_ref, k_ref, v_ref, qg_ref, o_ref, e_ref, den_ref):
    first, second = _head_masks()
    tile = MEM_ROWS
    n_tiles = q_ref.shape[0] // tile
    k = k_ref[0]
    v = v_ref[0]

    def probabilities(i, slot):
        rows = slice(i * tile, (i + 1) * tile)
        q = q_ref[rows, :].astype(F32)
        inv = lax.rsqrt(_pair_mean_square(q, first) + EPS) * QK_SCALE
        qn = (q * inv * qg_ref[0]).astype(BF16)
        q2 = jnp.concatenate([jnp.where(first, qn, 0), jnp.where(second, qn, 0)], axis=0)
        sc = _dot_nt(q2, k)
        e = jnp.exp(sc - jnp.max(sc, axis=-1, keepdims=True))
        den_ref[slot] = jnp.broadcast_to(jnp.sum(e, axis=-1, keepdims=True), (2 * tile, LANES))
        e_ref[slot] = e.astype(BF16)

    def output(i, slot):
        rows = slice(i * tile, (i + 1) * tile)
        pv = _dot(e_ref[slot], v) / den_ref[slot]
        out = jnp.where(first, pv[:tile], pv[tile:])
        o_ref[rows, :] = (out * _silu(g_ref[rows, :].astype(F32))).astype(BF16)

    for i in range(n_tiles + 1):
        if i >= 1:
            output(i - 1, (i - 1) % 2)
        if i < n_tiles:
            probabilities(i, i % 2)


def _mem_attention(proj, mk, mv, q_norm_g, layer, b, s, tokens):
    seq = lambda col: pl.BlockSpec((s, LANES), lambda bi, p: (bi, col + p))
    kv_spec = pl.BlockSpec((1, tokens, LANES), lambda bi, p: (layer, bi, p))
    return pl.pallas_call(
        _mem_attn_kernel,
        grid=(b, MEM_PAIRS),
        in_specs=[seq(COL_MEM_Q), seq(COL_MEM_G), kv_spec, kv_spec,
                  pl.BlockSpec((1, 1, LANES), lambda bi, p: (layer, 0, 0))],
        out_specs=pl.BlockSpec((s, LANES), lambda bi, p: (bi, p)),
        out_shape=jax.ShapeDtypeStruct((b * s, MEM_WIDTH), BF16),
        scratch_shapes=[pltpu.VMEM((2, 2 * MEM_ROWS, tokens), BF16),
                        pltpu.VMEM((2, 2 * MEM_ROWS, LANES), F32)],
        compiler_params=pltpu.CompilerParams(vmem_limit_bytes=VMEM_LIMIT),
        name="mem_attention",
    )(proj, proj, mk, mv, q_norm_g)


def kernel(x, mem, positions, norm_g, w_in, w_out, mem_norm_g, w_mem_kv, mem_q_norm_g,
           mem_k_norm_g, ret_norm_g):
    b, s, d = x.shape
    tokens = mem.shape[1]
    depth = w_in.shape[0]
    assert d == D_MODEL and w_in.shape[2] == IN_WIDTH
    assert s % SB_BLOCK == 0 and s % RET_CHUNK == 0 and s % MEM_ROWS == 0
    assert (b * s) % PROJ_ROWS == 0 and (b * s) % OUT_ROWS == 0

    w_in_b = w_in.astype(BF16)
    w_out_b = w_out.astype(BF16)
    w_kv_b = w_mem_kv.astype(BF16)
    norm_g3 = norm_g[:, None, :]
    ret_norm_g3 = ret_norm_g[:, None, :]
    q_norm_g3 = jnp.tile(mem_q_norm_g, (1, PAIR))[:, None, :]
    k_norm_g3 = jnp.tile(mem_k_norm_g, (1, PAIR))[:, None, :]

    cos, sin = _rope_tables(positions)
    mk_all, mv_all = _mem_kv(mem.reshape(b * tokens, d), mem_norm_g[:, None, :], w_kv_b, k_norm_g3)

    x2 = x.reshape(b * s, d)
    proj = _in_proj(x2, norm_g3, w_in_b, 0, cos, sin)
    for l in range(depth):
        sb_o = _sb_attention(proj, b, s)
        ret_o = _retention(proj, ret_norm_g3, l, b, s)
        mem_o = _mem_attention(proj, mk_all, mv_all, q_norm_g3, l, b, s, tokens)
        if l + 1 < depth:
            x2, proj = _out_in_proj(sb_o, ret_o, mem_o, w_out_b, l, x2, norm_g3, w_in_b, cos, sin)
        else:
            x2 = _out_proj(sb_o, ret_o, mem_o, w_out_b, l, x2)
    return x2.reshape(b, s, d)
```

```python
import functools
import math

import jax
import jax.numpy as jnp
from jax import lax
from jax.experimental import pallas as pl
from jax.experimental.pallas import tpu as pltpu

D_MODEL = 1024
HEAD_DIM = 64
SB_HEADS = 6
RET_HEADS = 6
MEM_HEADS = 4
SB_WIDTH = SB_HEADS * HEAD_DIM
RET_WIDTH = RET_HEADS * HEAD_DIM
MEM_WIDTH = MEM_HEADS * HEAD_DIM
MIX_WIDTH = SB_WIDTH + RET_WIDTH + MEM_WIDTH
IN_WIDTH = 4 * SB_WIDTH + 4 * RET_WIDTH + 2 * MEM_WIDTH
ROPE_BASE = 10000.0
EPS = 1e-6
QK_SCALE = HEAD_DIM ** -0.5
SB_Q_SCALE = QK_SCALE * math.log2(math.e)

LANES = 128
PAIR = LANES // HEAD_DIM
SB_PAIRS = SB_HEADS // PAIR
RET_PAIRS = RET_HEADS // PAIR
MEM_PAIRS = MEM_HEADS // PAIR

COL_SB_Q = 0
COL_SB_K = COL_SB_Q + SB_PAIRS
COL_SB_V = COL_SB_K + SB_PAIRS
COL_SB_G = COL_SB_V + SB_PAIRS
COL_RET_Q = COL_SB_G + SB_PAIRS
COL_RET_K = COL_RET_Q + RET_PAIRS
COL_RET_V = COL_RET_K + RET_PAIRS
COL_RET_G = COL_RET_V + RET_PAIRS
COL_MEM_Q = COL_RET_G + RET_PAIRS
COL_MEM_G = COL_MEM_Q + MEM_PAIRS

PROJ_ROWS = 512
OUT_ROWS = 1024
PROJ_COLS = 512
SB_BLOCK = 256
RET_CHUNK = 128
MEM_ROWS = 512
VMEM_LIMIT = 48 * 1024 * 1024

F32 = jnp.float32
BF16 = jnp.bfloat16


def _dot(a, b):
    return jnp.dot(a, b, preferred_element_type=F32)


def _dot_nt(a, b):
    return lax.dot_general(a, b, (((1,), (1,)), ((), ())), preferred_element_type=F32)


def _dot_tn(a, b):
    return lax.dot_general(a, b, (((0,), (0,)), ((), ())), preferred_element_type=F32)


def _silu(g):
    return g / (1.0 + jnp.exp(-g))


def _head_masks():
    lane = lax.broadcasted_iota(jnp.int32, (1, LANES), 1)
    first = lane < HEAD_DIM
    return first, jnp.logical_not(first)


def _pair_mean_square(t, first):
    sq = t * t
    s_a = jnp.sum(jnp.where(first, sq, 0.0), axis=-1, keepdims=True)
    s_b = jnp.sum(jnp.where(first, 0.0, sq), axis=-1, keepdims=True)
    return jnp.where(first, s_a, s_b) * (1.0 / HEAD_DIM)


def _rope_kernel(pos_ref, invf_ref, cos_ref, sin_ref):
    ang = pos_ref[...].astype(F32) * invf_ref[...]
    lane = lax.broadcasted_iota(jnp.int32, (1, LANES), 1)
    sign = jnp.where((lane % HEAD_DIM) < HEAD_DIM // 2, -1.0, 1.0)
    cos_ref[...] = jnp.cos(ang)
    sin_ref[...] = jnp.sin(ang) * sign


def _rope_tables(positions):
    b, s = positions.shape
    half = HEAD_DIM // 2
    inv_freq = ROPE_BASE ** (-jnp.arange(half, dtype=F32) / half)
    invf = jnp.tile(inv_freq, LANES // half)[None, :]
    pos = positions.reshape(b * s, 1)
    rows = s
    return pl.pallas_call(
        _rope_kernel,
        grid=(b * s // rows,),
        in_specs=[pl.BlockSpec((rows, 1), lambda i: (i, 0)),
                  pl.BlockSpec((1, LANES), lambda i: (0, 0))],
        out_specs=[pl.BlockSpec((rows, LANES), lambda i: (i, 0)),
                   pl.BlockSpec((rows, LANES), lambda i: (i, 0))],
        out_shape=[jax.ShapeDtypeStruct((b * s, LANES), F32)] * 2,
        name="rope_tables",
    )(pos, invf)


def _swap_halves(t):
    lane = lax.broadcasted_iota(jnp.int32, (1, LANES), 1)
    half = HEAD_DIM // 2
    lower = (lane % HEAD_DIM) < half
    return jnp.where(lower, pltpu.roll(t, LANES - half, 1), pltpu.roll(t, half, 1))


def _norm_project(x, g_ref, w_ref, cos_ref, sin_ref, o_ref):
    ms = jnp.mean(x * x, axis=-1, keepdims=True)
    h = (x * lax.rsqrt(ms + EPS) * g_ref[0]).astype(BF16)
    cos = cos_ref[...]
    sin = sin_ref[...]
    blocks_per_dot = PROJ_COLS // LANES
    for j in range(IN_WIDTH // PROJ_COLS):
        y = _dot(h, w_ref[0, :, j * PROJ_COLS:(j + 1) * PROJ_COLS])
        for b in range(blocks_per_dot):
            blk = j * blocks_per_dot + b
            piece = y[:, b * LANES:(b + 1) * LANES]
            if blk < COL_SB_K:
                piece = piece * SB_Q_SCALE
            elif COL_RET_Q <= blk < COL_RET_V:
                piece = piece * cos + _swap_halves(piece) * sin
                if blk >= COL_RET_K:
                    piece = piece * QK_SCALE
            o_ref[:, blk * LANES:(blk + 1) * LANES] = piece.astype(BF16)


def _in_proj_kernel(x_ref, g_ref, w_ref, cos_ref, sin_ref, o_ref):
    _norm_project(x_ref[...], g_ref, w_ref, cos_ref, sin_ref, o_ref)


def _mix(sb_ref, ret_ref, mem_ref, w_ref, x_ref):
    mixed = jnp.concatenate([sb_ref[...], ret_ref[...], mem_ref[...]], axis=1)
    return x_ref[...] + _dot(mixed, w_ref[0])


def _out_proj_kernel(sb_ref, ret_ref, mem_ref, w_ref, x_ref, o_ref):
    o_ref[...] = _mix(sb_ref, ret_ref, mem_ref, w_ref, x_ref)


def _out_in_proj_kernel(sb_ref, ret_ref, mem_ref, w_out_ref, x_ref, g_ref, w_in_ref, cos_ref,
                        sin_ref, x_out_ref, proj_ref):
    x_new = _mix(sb_ref, ret_ref, mem_ref, w_out_ref, x_ref)
    x_out_ref[...] = x_new
    _norm_project(x_new, g_ref, w_in_ref, cos_ref, sin_ref, proj_ref)


def _row_tiles(width, rows=PROJ_ROWS):
    return pl.BlockSpec((rows, width), lambda i: (i, 0))


def _layer_param(shape, layer):
    return pl.BlockSpec((1,) + shape, lambda i: (layer, 0, 0), pipeline_mode=pl.Buffered(1))


def _in_proj(x2, g, w, layer, cos, sin):
    m = x2.shape[0]
    return pl.pallas_call(
        _in_proj_kernel,
        grid=(m // PROJ_ROWS,),
        in_specs=[_row_tiles(D_MODEL), _layer_param((1, D_MODEL), layer),
                  _layer_param((D_MODEL, IN_WIDTH), layer), _row_tiles(LANES), _row_tiles(LANES)],
        out_specs=_row_tiles(IN_WIDTH),
        out_shape=jax.ShapeDtypeStruct((m, IN_WIDTH), BF16),
        compiler_params=pltpu.CompilerParams(vmem_limit_bytes=VMEM_LIMIT),
        name="in_proj",
    )(x2, g, w, cos, sin)


def _out_proj(sb_o, ret_o, mem_o, w, layer, x2):
    m = x2.shape[0]
    return pl.pallas_call(
        _out_proj_kernel,
        grid=(m // OUT_ROWS,),
        in_specs=[_row_tiles(SB_WIDTH, OUT_ROWS), _row_tiles(RET_WIDTH, OUT_ROWS),
                  _row_tiles(MEM_WIDTH, OUT_ROWS), _layer_param((MIX_WIDTH, D_MODEL), layer),
                  _row_tiles(D_MODEL, OUT_ROWS)],
        out_specs=_row_tiles(D_MODEL, OUT_ROWS),
        out_shape=jax.ShapeDtypeStruct((m, D_MODEL), F32),
        compiler_params=pltpu.CompilerParams(vmem_limit_bytes=VMEM_LIMIT),
        name="out_proj",
    )(sb_o, ret_o, mem_o, w, x2)


def _out_in_proj(sb_o, ret_o, mem_o, w_out, layer, x2, g, w_in, cos, sin):
    m = x2.shape[0]
    return pl.pallas_call(
        _out_in_proj_kernel,
        grid=(m // PROJ_ROWS,),
        in_specs=[_row_tiles(SB_WIDTH), _row_tiles(RET_WIDTH), _row_tiles(MEM_WIDTH),
                  _layer_param((MIX_WIDTH, D_MODEL), layer), _row_tiles(D_MODEL),
                  _layer_param((1, D_MODEL), layer + 1), _layer_param((D_MODEL, IN_WIDTH), layer + 1),
                  _row_tiles(LANES), _row_tiles(LANES)],
        out_specs=[_row_tiles(D_MODEL), _row_tiles(IN_WIDTH)],
        out_shape=[jax.ShapeDtypeStruct((m, D_MODEL), F32),
                   jax.ShapeDtypeStruct((m, IN_WIDTH), BF16)],
        compiler_params=pltpu.CompilerParams(vmem_limit_bytes=VMEM_LIMIT),
        name="out_in_proj",
    )(sb_o, ret_o, mem_o, w_out, x2, g, w_in, cos, sin)


SB_LAG = 2
SB_STEP = 4
SB_MASKED = -1e30
SB_DEAD = 160.0
SB_IDLE_CARRY = 1e30


def _sb_schedule(nq):
    zero_slot, junk_slot = nq, nq + 1
    idle = (0, 0, junk_slot, junk_slot, 1)
    items = [idle] * SB_STEP
    waves = []
    for w in range(nq):
        start = len(items)
        for qb in range(nq - 1, w - 1, -1):
            src = zero_slot if w == 0 else qb
            items.append((qb, qb - w, src, qb, 0 if qb > w else 1))
        items += [idle] * (-(len(items) - start) % SB_STEP)
        waves.append((start, (len(items) - start) // SB_STEP))
    return jnp.asarray(items, jnp.int32).T, jnp.asarray(waves, jnp.int32).T, waves[0][1]


def _sb_kernel(tab_ref, wave_ref, q_ref, k_ref, v_ref, g_ref, o_ref,
               tri_ref, bias_ref, z_ref, sp_ref, w_ref, r_ref, m_ref, acc_ref, *, n_waves,
               first_wave_steps):
    t = SB_BLOCK
    first, second = _head_masks()
    n_carry = r_ref.shape[0]

    @pl.when(jnp.logical_and(pl.program_id(0) == 0, pl.program_id(1) == 0))
    def _constants():
        row = lax.broadcasted_iota(jnp.int32, (t, t), 0)
        col = lax.broadcasted_iota(jnp.int32, (t, t), 1)
        tri_ref[...] = (row >= col).astype(BF16)
        bias_ref[...] = jnp.where(col < row, 0.0, SB_MASKED)
        r_ref[n_carry - 2] = jnp.zeros(r_ref.shape[1:], F32)

    z_ref[SB_STEP - 1] = jnp.full(z_ref.shape[1:], SB_MASKED, F32)
    sp_ref[SB_STEP - 1] = jnp.zeros(sp_ref.shape[1:], BF16)
    for slot in range(SB_STEP - SB_LAG, SB_STEP):
        w_ref[slot] = jnp.zeros(w_ref.shape[1:], BF16)
    r_ref[n_carry - 1] = jnp.full(r_ref.shape[1:], SB_IDLE_CARRY, F32)
    m_ref[1] = jnp.zeros(m_ref.shape[1:], F32)
    acc_ref[...] = jnp.zeros_like(acc_ref)

    def rows(blk):
        return pl.ds(pl.multiple_of(blk * t, t), t)

    def softplus2(z):
        return jnp.maximum(z, 0.0) + jnp.log2(1.0 + jnp.exp2(-jnp.abs(z)))

    half = t // 2

    def lower_rows(a):
        return [a[r0:r0 + half] for r0 in (half, t + half)]

    def zero_upper_rows(lower):
        zeros = jnp.zeros((half, half), lower[0].dtype)
        return jnp.concatenate([zeros, lower[0], zeros, lower[1]], axis=0)

    def scores(j, slot, diagonal):
        qb, kb = tab_ref[0, j], tab_ref[1, j]
        q = q_ref[rows(qb), :]
        q2 = jnp.concatenate([jnp.where(first, q, 0), jnp.where(second, q, 0)], axis=0)
        z = _dot_nt(q2, k_ref[rows(kb), :])
        if diagonal:
            bias = bias_ref[...]
            z = z + jnp.concatenate([bias, bias], axis=0)
            sp = jnp.concatenate(
                [softplus2(z[:, :half]),
                 zero_upper_rows([softplus2(p) for p in lower_rows(z[:, half:])])], axis=1)
        else:
            sp = softplus2(z)
        z_ref[slot] = z
        sp_ref[slot] = sp.astype(BF16)

    def weights(j, slot, track_minimum, diagonal):
        cum = _dot(sp_ref[slot], tri_ref[...])
        r = r_ref[tab_ref[2, j]]
        z = z_ref[slot]
        if diagonal:
            pieces = zip(lower_rows(z[:, half:]), lower_rows(cum[:, half:]), lower_rows(r))
            halves = [jnp.exp2(z[:, :half] - cum[:, :half] - r),
                      zero_upper_rows([jnp.exp2(a - b - c) for a, b, c in pieces])]
        else:
            halves = [jnp.exp2(z[:, c:c + LANES] - cum[:, c:c + LANES] - r) for c in (0, LANES)]
        w_ref[slot] = jnp.concatenate(halves, axis=1).astype(BF16)
        r_new = r + cum[:, 0:1]
        r_ref[tab_ref[3, j]] = r_new
        if track_minimum:
            m_slot = tab_ref[4, j]
            m_ref[m_slot] = jnp.minimum(m_ref[m_slot], r_new)

    def values(j, slot):
        qb, kb = tab_ref[0, j], tab_ref[1, j]
        pv = _dot(w_ref[slot], v_ref[rows(kb), :])
        acc_ref[rows(qb), :] += jnp.where(first, pv[:t], pv[t:])

    def run_wave(start, n_steps, diagonal):
        def step(n, _):
            for i in range(SB_STEP):
                c = start + SB_STEP * n + i
                values(c - 2, (i - 2) % SB_STEP)
                weights(c - 1, (i - 1) % SB_STEP, not diagonal, diagonal)
                scores(c, i, diagonal)
            return 0

        lax.fori_loop(0, n_steps, step, 0)
        return start + SB_STEP * n_steps

    def later_wave(state):
        w = state[0]
        m_ref[0] = jnp.full(m_ref.shape[1:], jnp.inf, F32)
        end = run_wave(wave_ref[0, w], wave_ref[1, w], False)
        return w + 1, end, jnp.min(m_ref[0])

    end = run_wave(SB_STEP, first_wave_steps, True)
    _, end, _ = lax.while_loop(lambda s: jnp.logical_and(s[0] < n_waves, s[2] < SB_DEAD),
                               later_wave, (jnp.int32(1), jnp.int32(end), jnp.float32(0.0)))
    values(end - 2, SB_STEP - 2)
    weights(end - 1, SB_STEP - 1, False, False)
    values(end - 1, SB_STEP - 1)
    o_ref[...] = (acc_ref[...] * _silu(g_ref[...].astype(F32))).astype(BF16)


def _sb_attention(proj, b, s):
    nq = s // SB_BLOCK
    assert SB_BLOCK == 2 * LANES
    table, waves, first_wave_steps = _sb_schedule(nq)
    t = SB_BLOCK
    seq = lambda col: pl.BlockSpec((s, LANES), lambda bi, p, tab, wav: (bi, col + p))
    return pl.pallas_call(
        functools.partial(_sb_kernel, n_waves=nq, first_wave_steps=first_wave_steps),
        grid_spec=pltpu.PrefetchScalarGridSpec(
            num_scalar_prefetch=2,
            grid=(b, SB_PAIRS),
            in_specs=[seq(COL_SB_Q), seq(COL_SB_K), seq(COL_SB_V), seq(COL_SB_G)],
            out_specs=pl.BlockSpec((s, LANES), lambda bi, p, tab, wav: (bi, p)),
            scratch_shapes=[pltpu.VMEM((t, t), BF16),
                            pltpu.VMEM((t, t), F32),
                            pltpu.VMEM((SB_STEP, 2 * t, t), F32),
                            pltpu.VMEM((SB_STEP, 2 * t, t), BF16),
                            pltpu.VMEM((SB_STEP, 2 * t, t), BF16),
                            pltpu.VMEM((nq + 2, 2 * t, LANES), F32),
                            pltpu.VMEM((2, 2 * t, LANES), F32),
                            pltpu.VMEM((s, LANES), F32)]),
        out_shape=jax.ShapeDtypeStruct((b * s, SB_WIDTH), BF16),
        compiler_params=pltpu.CompilerParams(
            dimension_semantics=("arbitrary", "arbitrary"),
            vmem_limit_bytes=VMEM_LIMIT),
        name="sb_attention",
    )(table, waves, proj, proj, proj, proj)


def _ret_kernel(q_ref, k_ref, v_ref, g_ref, ng_ref, lg_ref, o_ref, kv_ref, state_ref):
    c = RET_CHUNK
    first, second = _head_masks()
    nchunks = q_ref.shape[0] // c

    lg_lane = lg_ref[0]
    lg_a = lg_lane[:, 0:1]
    lg_b = lg_lane[:, HEAD_DIM:HEAD_DIM + 1]

    ri = lax.broadcasted_iota(jnp.int32, (c, c), 0)
    ci = lax.broadcasted_iota(jnp.int32, (c, c), 1)
    diff = (ri - ci).astype(F32)
    keep = ri >= ci
    decay_a = jnp.where(keep, jnp.exp(lg_a * jnp.maximum(diff, 0.0)), 0.0)
    decay_b = jnp.where(keep, jnp.exp(lg_b * jnp.maximum(diff, 0.0)), 0.0)
    idx = lax.broadcasted_iota(jnp.int32, (c, LANES), 0).astype(F32)
    q_decay = jnp.exp(lg_lane * (idx + 1.0))
    k_decay = jnp.exp(lg_lane * (c - 1.0 - idx))
    si = lax.broadcasted_iota(jnp.int32, (LANES, LANES), 0)
    sj = lax.broadcasted_iota(jnp.int32, (LANES, LANES), 1)
    same_head = (si < HEAD_DIM) == (sj < HEAD_DIM)
    state_decay = jnp.where(same_head, jnp.exp(lg_lane * float(c)), 0.0)
    norm_g = ng_ref[0]

    chunk_rows = [slice(n * c, (n + 1) * c) for n in range(nchunks)]

    for n, rows in enumerate(chunk_rows):
        kd = (k_ref[rows, :].astype(F32) * k_decay).astype(BF16)
        kv_ref[n] = jnp.where(same_head, _dot_tn(kd, v_ref[rows, :]), 0.0)

    state = jnp.zeros((LANES, LANES), F32)
    for n in range(nchunks):
        state_ref[n] = state.astype(BF16)
        state = state * state_decay + kv_ref[n]

    decay2 = jnp.concatenate([decay_a, decay_b], axis=1)
    for n, rows in enumerate(chunk_rows):
        qb = q_ref[rows, :]
        kb = k_ref[rows, :]
        v = v_ref[rows, :]
        k2 = jnp.concatenate([jnp.where(first, kb, 0), jnp.where(second, kb, 0)], axis=0)
        v2 = jnp.concatenate([jnp.where(first, v, 0), jnp.where(second, v, 0)], axis=0)
        scores = (_dot_nt(qb, k2) * decay2).astype(BF16)
        out = _dot(scores, v2) + _dot(qb, state_ref[n]) * q_decay
        inv = lax.rsqrt(_pair_mean_square(out, first) + EPS)
        y = out * inv * norm_g
        o_ref[rows, :] = (y * _silu(g_ref[rows, :].astype(F32))).astype(BF16)


def _retention(proj, ret_norm_g, layer, b, s):
    seq = lambda col: pl.BlockSpec((s, LANES), lambda bi, p: (bi, col + p))
    gamma = 1.0 - 2.0 ** (-5.0 - jnp.arange(RET_HEADS, dtype=F32))
    log_gamma = jnp.repeat(jnp.log(gamma), HEAD_DIM).reshape(RET_PAIRS, 1, LANES)
    return pl.pallas_call(
        _ret_kernel,
        grid=(b, RET_PAIRS),
        in_specs=[seq(COL_RET_Q), seq(COL_RET_K), seq(COL_RET_V), seq(COL_RET_G),
                  pl.BlockSpec((1, 1, LANES), lambda bi, p: (layer, 0, p)),
                  pl.BlockSpec((1, 1, LANES), lambda bi, p: (p, 0, 0))],
        out_specs=pl.BlockSpec((s, LANES), lambda bi, p: (bi, p)),
        out_shape=jax.ShapeDtypeStruct((b * s, RET_WIDTH), BF16),
        scratch_shapes=[pltpu.VMEM((s // RET_CHUNK, LANES, LANES), F32),
                        pltpu.VMEM((s // RET_CHUNK, LANES, LANES), BF16)],
        compiler_params=pltpu.CompilerParams(vmem_limit_bytes=VMEM_LIMIT),
        name="retention",
    )(proj, proj, proj, proj, ret_norm_g, log_gamma)


def _mem_kv_kernel(mem_ref, g_ref, w_ref, kg_ref, k_ref, v_ref):
    first, _ = _head_masks()
    x = mem_ref[...]
    ms = jnp.mean(x * x, axis=-1, keepdims=True)
    h = (x * lax.rsqrt(ms + EPS) * g_ref[0]).astype(BF16)
    kv = _dot(h, w_ref[0])
    kg = kg_ref[0]
    for pair in range(MEM_PAIRS):
        cols = slice(pair * LANES, (pair + 1) * LANES)
        kp = kv[:, cols]
        kn = kp * lax.rsqrt(_pair_mean_square(kp, first) + EPS) * kg
        k_ref[0, :, cols] = kn.astype(BF16)
    v_ref[0] = kv[:, MEM_WIDTH:].astype(BF16)


def _mem_kv(mem2, mem_norm_g, w_mem_kv, k_norm_g):
    depth = w_mem_kv.shape[0]
    rows = mem2.shape[0]
    out = jax.ShapeDtypeStruct((depth, rows, MEM_WIDTH), BF16)
    return pl.pallas_call(
        _mem_kv_kernel,
        grid=(depth,),
        in_specs=[pl.BlockSpec((rows, D_MODEL), lambda l: (0, 0)),
                  pl.BlockSpec((1, 1, D_MODEL), lambda l: (l, 0, 0)),
                  pl.BlockSpec((1, D_MODEL, 2 * MEM_WIDTH), lambda l: (l, 0, 0)),
                  pl.BlockSpec((1, 1, LANES), lambda l: (l, 0, 0))],
        out_specs=[pl.BlockSpec((1, rows, MEM_WIDTH), lambda l: (l, 0, 0)),
                   pl.BlockSpec((1, rows, MEM_WIDTH), lambda l: (l, 0, 0))],
        out_shape=[out, out],
        compiler_params=pltpu.CompilerParams(vmem_limit_bytes=VMEM_LIMIT),
        name="mem_kv",
    )(mem2, mem_norm_g, w_mem_kv, k_norm_g)


def _mem_attn_kernel(q_ref, g_ref, k_ref, v_ref, qg_ref, o_ref, e_ref, den_ref):
    first, second = _head_masks()
    tile = MEM_ROWS
    n_tiles = q_ref.shape[0] // tile
    k = k_ref[0]
    v = v_ref[0]

    def probabilities(i, slot):
        rows = slice(i * tile, (i + 1) * tile)
        q = q_ref[rows, :].astype(F32)
        inv = lax.rsqrt(_pair_mean_square(q, first) + EPS) * QK_SCALE
        qn = (q * inv * qg_ref[0]).astype(BF16)
        q2 = jnp.concatenate([jnp.where(first, qn, 0), jnp.where(second, qn, 0)], axis=0)
        sc = _dot_nt(q2, k)
        e = jnp.exp(sc - jnp.max(sc, axis=-1, keepdims=True))
        den_ref[slot] = jnp.broadcast_to(jnp.sum(e, axis=-1, keepdims=True), (2 * tile, LANES))
        e_ref[slot] = e.astype(BF16)

    def output(i, slot):
        rows = slice(i * tile, (i + 1) * tile)
        pv = _dot(e_ref[slot], v) / den_ref[slot]
        out = jnp.where(first, pv[:tile], pv[tile:])
        o_ref[rows, :] = (out * _silu(g_ref[rows, :].astype(F32))).astype(BF16)

    for i in range(n_tiles + 1):
        if i >= 1:
            output(i - 1, (i - 1) % 2)
        if i < n_tiles:
            probabilities(i, i % 2)


def _mem_attention(proj, mk, mv, q_norm_g, layer, b, s, tokens):
    seq = lambda col: pl.BlockSpec((s, LANES), lambda bi, p: (bi, col + p))
    kv_spec = pl.BlockSpec((1, tokens, LANES), lambda bi, p: (layer, bi, p))
    return pl.pallas_call(
        _mem_attn_kernel,
        grid=(b, MEM_PAIRS),
        in_specs=[seq(COL_MEM_Q), seq(COL_MEM_G), kv_spec, kv_spec,
                  pl.BlockSpec((1, 1, LANES), lambda bi, p: (layer, 0, 0))],
        out_specs=pl.BlockSpec((s, LANES), lambda bi, p: (bi, p)),
        out_shape=jax.ShapeDtypeStruct((b * s, MEM_WIDTH), BF16),
        scratch_shapes=[pltpu.VMEM((2, 2 * MEM_ROWS, tokens), BF16),
                        pltpu.VMEM((2, 2 * MEM_ROWS, LANES), F32)],
        compiler_params=pltpu.CompilerParams(vmem_limit_bytes=VMEM_LIMIT),
        name="mem_attention",
    )(proj, proj, mk, mv, q_norm_g)


def kernel(x, mem, positions, norm_g, w_in, w_out, mem_norm_g, w_mem_kv, mem_q_norm_g,
           mem_k_norm_g, ret_norm_g):
    b, s, d = x.shape
    tokens = mem.shape[1]
    depth = w_in.shape[0]
    assert d == D_MODEL and w_in.shape[2] == IN_WIDTH
    assert s % SB_BLOCK == 0 and s % RET_CHUNK == 0 and s % MEM_ROWS == 0
    assert (b * s) % PROJ_ROWS == 0 and (b * s) % OUT_ROWS == 0

    w_in_b = w_in.astype(BF16)
    w_out_b = w_out.astype(BF16)
    w_kv_b = w_mem_kv.astype(BF16)
    norm_g3 = norm_g[:, None, :]
    ret_norm_g3 = ret_norm_g[:, None, :]
    q_norm_g3 = jnp.tile(mem_q_norm_g, (1, PAIR))[:, None, :]
    k_norm_g3 = jnp.tile(mem_k_norm_g, (1, PAIR))[:, None, :]

    cos, sin = _rope_tables(positions)
    mk_all, mv_all = _mem_kv(mem.reshape(b * tokens, d), mem_norm_g[:, None, :], w_kv_b, k_norm_g3)

    x2 = x.reshape(b * s, d)
    proj = _in_proj(x2, norm_g3, w_in_b, 0, cos, sin)
    for l in range(depth):
        sb_o = _sb_attention(proj, b, s)
        ret_o = _retention(proj, ret_norm_g3, l, b, s)
        mem_o = _mem_attention(proj, mk_all, mv_all, q_norm_g3, l, b, s, tokens)
        if l + 1 < depth:
            x2, proj = _out_in_proj(sb_o, ret_o, mem_o, w_out_b, l, x2, norm_g3, w_in_b, cos, sin)
        else:
            x2 = _out_proj(sb_o, ret_o, mem_o, w_out_b, l, x2)
    return x2.reshape(b, s, d)
```

```python
import functools
import math

import jax
import jax.numpy as jnp
from jax import lax
from jax.experimental import pallas as pl
from jax.experimental.pallas import tpu as pltpu

D_MODEL = 1024
HEAD_DIM = 64
SB_HEADS = 6
RET_HEADS = 6
MEM_HEADS = 4
SB_WIDTH = SB_HEADS * HEAD_DIM
RET_WIDTH = RET_HEADS * HEAD_DIM
MEM_WIDTH = MEM_HEADS * HEAD_DIM
MIX_WIDTH = SB_WIDTH + RET_WIDTH + MEM_WIDTH
IN_WIDTH = 4 * SB_WIDTH + 4 * RET_WIDTH + 2 * MEM_WIDTH
ROPE_BASE = 10000.0
EPS = 1e-6
QK_SCALE = HEAD_DIM ** -0.5
SB_Q_SCALE = QK_SCALE * math.log2(math.e)

LANES = 128
PAIR = LANES // HEAD_DIM
SB_PAIRS = SB_HEADS // PAIR
RET_PAIRS = RET_HEADS // PAIR
MEM_PAIRS = MEM_HEADS // PAIR

COL_SB_Q = 0
COL_SB_K = COL_SB_Q + SB_PAIRS
COL_SB_V = COL_SB_K + SB_PAIRS
COL_SB_G = COL_SB_V + SB_PAIRS
COL_RET_Q = COL_SB_G + SB_PAIRS
COL_RET_K = COL_RET_Q + RET_PAIRS
COL_RET_V = COL_RET_K + RET_PAIRS
COL_RET_G = COL_RET_V + RET_PAIRS
COL_MEM_Q = COL_RET_G + RET_PAIRS
COL_MEM_G = COL_MEM_Q + MEM_PAIRS

PROJ_ROWS = 512
OUT_ROWS = 1024
PROJ_COLS = 512
SB_BLOCK = 256
RET_CHUNK = 128
MEM_ROWS = 512
VMEM_LIMIT = 48 * 1024 * 1024

F32 = jnp.float32
BF16 = jnp.bfloat16


def _dot(a, b):
    return jnp.dot(a, b, preferred_element_type=F32)


def _dot_nt(a, b):
    return lax.dot_general(a, b, (((1,), (1,)), ((), ())), preferred_element_type=F32)


def _dot_tn(a, b):
    return lax.dot_general(a, b, (((0,), (0,)), ((), ())), preferred_element_type=F32)


def _silu(g):
    return g / (1.0 + jnp.exp(-g))


def _head_masks():
    lane = lax.broadcasted_iota(jnp.int32, (1, LANES), 1)
    first = lane < HEAD_DIM
    return first, jnp.logical_not(first)


def _pair_mean_square(t, first):
    sq = t * t
    s_a = jnp.sum(jnp.where(first, sq, 0.0), axis=-1, keepdims=True)
    s_b = jnp.sum(jnp.where(first, 0.0, sq), axis=-1, keepdims=True)
    return jnp.where(first, s_a, s_b) * (1.0 / HEAD_DIM)


def _rope_kernel(pos_ref, invf_ref, cos_ref, sin_ref):
    ang = pos_ref[...].astype(F32) * invf_ref[...]
    lane = lax.broadcasted_iota(jnp.int32, (1, LANES), 1)
    sign = jnp.where((lane % HEAD_DIM) < HEAD_DIM // 2, -1.0, 1.0)
    cos_ref[...] = jnp.cos(ang)
    sin_ref[...] = jnp.sin(ang) * sign


def _rope_tables(positions):
    b, s = positions.shape
    half = HEAD_DIM // 2
    inv_freq = ROPE_BASE ** (-jnp.arange(half, dtype=F32) / half)
    invf = jnp.tile(inv_freq, LANES // half)[None, :]
    pos = positions.reshape(b * s, 1)
    rows = s
    return pl.pallas_call(
        _rope_kernel,
        grid=(b * s // rows,),
        in_specs=[pl.BlockSpec((rows, 1), lambda i: (i, 0)),
                  pl.BlockSpec((1, LANES), lambda i: (0, 0))],
        out_specs=[pl.BlockSpec((rows, LANES), lambda i: (i, 0)),
                   pl.BlockSpec((rows, LANES), lambda i: (i, 0))],
        out_shape=[jax.ShapeDtypeStruct((b * s, LANES), F32)] * 2,
        name="rope_tables",
    )(pos, invf)


def _swap_halves(t):
    lane = lax.broadcasted_iota(jnp.int32, (1, LANES), 1)
    half = HEAD_DIM // 2
    lower = (lane % HEAD_DIM) < half
    return jnp.where(lower, pltpu.roll(t, LANES - half, 1), pltpu.roll(t, half, 1))


def _norm_project(x, g_ref, w_ref, cos_ref, sin_ref, o_ref):
    ms = jnp.mean(x * x, axis=-1, keepdims=True)
    h = (x * lax.rsqrt(ms + EPS) * g_ref[0]).astype(BF16)
    cos = cos_ref[...]
    sin = sin_ref[...]
    blocks_per_dot = PROJ_COLS // LANES
    for j in range(IN_WIDTH // PROJ_COLS):
        y = _dot(h, w_ref[0, :, j * PROJ_COLS:(j + 1) * PROJ_COLS].astype(BF16))
        for b in range(blocks_per_dot):
            blk = j * blocks_per_dot + b
            piece = y[:, b * LANES:(b + 1) * LANES]
            if blk < COL_SB_K:
                piece = piece * SB_Q_SCALE
            elif COL_RET_Q <= blk < COL_RET_V:
                piece = piece * cos + _swap_halves(piece) * sin
                if blk >= COL_RET_K:
                    piece = piece * QK_SCALE
            o_ref[:, blk * LANES:(blk + 1) * LANES] = piece.astype(BF16)


def _in_proj_kernel(x_ref, g_ref, w_ref, cos_ref, sin_ref, o_ref):
    _norm_project(x_ref[...], g_ref, w_ref, cos_ref, sin_ref, o_ref)


def _mix(sb_ref, ret_ref, mem_ref, w_ref, x_ref):
    mixed = jnp.concatenate([sb_ref[...], ret_ref[...], mem_ref[...]], axis=1)
    return x_ref[...] + _dot(mixed, w_ref[0].astype(BF16))


def _out_proj_kernel(sb_ref, ret_ref, mem_ref, w_ref, x_ref, o_ref):
    o_ref[...] = _mix(sb_ref, ret_ref, mem_ref, w_ref, x_ref)


def _out_in_proj_kernel(sb_ref, ret_ref, mem_ref, w_out_ref, x_ref, g_ref, w_in_ref, cos_ref,
                        sin_ref, x_out_ref, proj_ref):
    x_new = _mix(sb_ref, ret_ref, mem_ref, w_out_ref, x_ref)
    x_out_ref[...] = x_new
    _norm_project(x_new, g_ref, w_in_ref, cos_ref, sin_ref, proj_ref)


def _row_tiles(width, rows=PROJ_ROWS):
    return pl.BlockSpec((rows, width), lambda i: (i, 0))


def _layer_param(shape, layer):
    return pl.BlockSpec((1,) + shape, lambda i: (layer, 0, 0), pipeline_mode=pl.Buffered(1))


def _in_proj(x2, g, w, layer, cos, sin):
    m = x2.shape[0]
    return pl.pallas_call(
        _in_proj_kernel,
        grid=(m // PROJ_ROWS,),
        in_specs=[_row_tiles(D_MODEL), _layer_param((1, D_MODEL), layer),
                  _layer_param((D_MODEL, IN_WIDTH), layer), _row_tiles(LANES), _row_tiles(LANES)],
        out_specs=_row_tiles(IN_WIDTH),
        out_shape=jax.ShapeDtypeStruct((m, IN_WIDTH), BF16),
        compiler_params=pltpu.CompilerParams(vmem_limit_bytes=VMEM_LIMIT),
        name="in_proj",
    )(x2, g, w, cos, sin)


def _out_proj(sb_o, ret_o, mem_o, w, layer, x2):
    m = x2.shape[0]
    return pl.pallas_call(
        _out_proj_kernel,
        grid=(m // OUT_ROWS,),
        in_specs=[_row_tiles(SB_WIDTH, OUT_ROWS), _row_tiles(RET_WIDTH, OUT_ROWS),
                  _row_tiles(MEM_WIDTH, OUT_ROWS), _layer_param((MIX_WIDTH, D_MODEL), layer),
                  _row_tiles(D_MODEL, OUT_ROWS)],
        out_specs=_row_tiles(D_MODEL, OUT_ROWS),
        out_shape=jax.ShapeDtypeStruct((m, D_MODEL), F32),
        compiler_params=pltpu.CompilerParams(vmem_limit_bytes=VMEM_LIMIT),
        name="out_proj",
    )(sb_o, ret_o, mem_o, w, x2)


def _out_in_proj(sb_o, ret_o, mem_o, w_out, layer, x2, g, w_in, cos, sin):
    m = x2.shape[0]
    return pl.pallas_call(
        _out_in_proj_kernel,
        grid=(m // PROJ_ROWS,),
        in_specs=[_row_tiles(SB_WIDTH), _row_tiles(RET_WIDTH), _row_tiles(MEM_WIDTH),
                  _layer_param((MIX_WIDTH, D_MODEL), layer), _row_tiles(D_MODEL),
                  _layer_param((1, D_MODEL), layer + 1), _layer_param((D_MODEL, IN_WIDTH), layer + 1),
                  _row_tiles(LANES), _row_tiles(LANES)],
        out_specs=[_row_tiles(D_MODEL), _row_tiles(IN_WIDTH)],
        out_shape=[jax.ShapeDtypeStruct((m, D_MODEL), F32),
                   jax.ShapeDtypeStruct((m, IN_WIDTH), BF16)],
        compiler_params=pltpu.CompilerParams(vmem_limit_bytes=VMEM_LIMIT),
        name="out_in_proj",
    )(sb_o, ret_o, mem_o, w_out, x2, g, w_in, cos, sin)


SB_LAG = 2
SB_STEP = 4
SB_MASKED = -1e30
SB_DEAD = 160.0
SB_IDLE_CARRY = 1e30


def _sb_schedule(nq):
    zero_slot, junk_slot = nq, nq + 1
    idle = (0, 0, junk_slot, junk_slot, 1)
    items = [idle] * SB_STEP
    waves = []
    for w in range(nq):
        start = len(items)
        for qb in range(nq - 1, w - 1, -1):
            src = zero_slot if w == 0 else qb
            items.append((qb, qb - w, src, qb, 0 if qb > w else 1))
        items += [idle] * (-(len(items) - start) % SB_STEP)
        waves.append((start, (len(items) - start) // SB_STEP))
    return jnp.asarray(items, jnp.int32).T, jnp.asarray(waves, jnp.int32).T, waves[0][1]


def _sb_kernel(tab_ref, wave_ref, q_ref, k_ref, v_ref, g_ref, o_ref,
               tri_ref, bias_ref, z_ref, sp_ref, w_ref, r_ref, m_ref, acc_ref, *, n_waves,
               first_wave_steps):
    t = SB_BLOCK
    first, second = _head_masks()
    n_carry = r_ref.shape[0]

    @pl.when(jnp.logical_and(pl.program_id(0) == 0, pl.program_id(1) == 0))
    def _constants():
        row = lax.broadcasted_iota(jnp.int32, (t, t), 0)
        col = lax.broadcasted_iota(jnp.int32, (t, t), 1)
        tri_ref[...] = (row >= col).astype(BF16)
        bias_ref[...] = jnp.where(col < row, 0.0, SB_MASKED)
        r_ref[n_carry - 2] = jnp.zeros(r_ref.shape[1:], F32)

    z_ref[SB_STEP - 1] = jnp.full(z_ref.shape[1:], SB_MASKED, F32)
    sp_ref[SB_STEP - 1] = jnp.zeros(sp_ref.shape[1:], BF16)
    for slot in range(SB_STEP - SB_LAG, SB_STEP):
        w_ref[slot] = jnp.zeros(w_ref.shape[1:], BF16)
    r_ref[n_carry - 1] = jnp.full(r_ref.shape[1:], SB_IDLE_CARRY, F32)
    m_ref[1] = jnp.zeros(m_ref.shape[1:], F32)
    acc_ref[...] = jnp.zeros_like(acc_ref)

    def rows(blk):
        return pl.ds(pl.multiple_of(blk * t, t), t)

    def softplus2(z):
        return jnp.maximum(z, 0.0) + jnp.log2(1.0 + jnp.exp2(-jnp.abs(z)))

    half = t // 2

    def lower_rows(a):
        return [a[r0:r0 + half] for r0 in (half, t + half)]

    def zero_upper_rows(lower):
        zeros = jnp.zeros((half, half), lower[0].dtype)
        return jnp.concatenate([zeros, lower[0], zeros, lower[1]], axis=0)

    def scores(j, slot, diagonal):
        qb, kb = tab_ref[0, j], tab_ref[1, j]
        q = q_ref[rows(qb), :]
        q2 = jnp.concatenate([jnp.where(first, q, 0), jnp.where(second, q, 0)], axis=0)
        z = _dot_nt(q2, k_ref[rows(kb), :])
        if diagonal:
            bias = bias_ref[...]
            z = z + jnp.concatenate([bias, bias], axis=0)
            sp = jnp.concatenate(
                [softplus2(z[:, :half]),
                 zero_upper_rows([softplus2(p) for p in lower_rows(z[:, half:])])], axis=1)
        else:
            sp = softplus2(z)
        z_ref[slot] = z
        sp_ref[slot] = sp.astype(BF16)

    def weights(j, slot, track_minimum, diagonal):
        cum = _dot(sp_ref[slot], tri_ref[...])
        r = r_ref[tab_ref[2, j]]
        z = z_ref[slot]
        if diagonal:
            pieces = zip(lower_rows(z[:, half:]), lower_rows(cum[:, half:]), lower_rows(r))
            halves = [jnp.exp2(z[:, :half] - cum[:, :half] - r),
                      zero_upper_rows([jnp.exp2(a - b - c) for a, b, c in pieces])]
        else:
            halves = [jnp.exp2(z[:, c:c + LANES] - cum[:, c:c + LANES] - r) for c in (0, LANES)]
        w_ref[slot] = jnp.concatenate(halves, axis=1).astype(BF16)
        r_new = r + cum[:, 0:1]
        r_ref[tab_ref[3, j]] = r_new
        if track_minimum:
            m_slot = tab_ref[4, j]
            m_ref[m_slot] = jnp.minimum(m_ref[m_slot], r_new)

    def values(j, slot):
        qb, kb = tab_ref[0, j], tab_ref[1, j]
        pv = _dot(w_ref[slot], v_ref[rows(kb), :])
        acc_ref[rows(qb), :] += jnp.where(first, pv[:t], pv[t:])

    def run_wave(start, n_steps, diagonal):
        def step(n, _):
            for i in range(SB_STEP):
                c = start + SB_STEP * n + i
                values(c - 2, (i - 2) % SB_STEP)
                weights(c - 1, (i - 1) % SB_STEP, not diagonal, diagonal)
                scores(c, i, diagonal)
            return 0

        lax.fori_loop(0, n_steps, step, 0)
        return start + SB_STEP * n_steps

    def later_wave(state):
        w = state[0]
        m_ref[0] = jnp.full(m_ref.shape[1:], jnp.inf, F32)
        end = run_wave(wave_ref[0, w], wave_ref[1, w], False)
        return w + 1, end, jnp.min(m_ref[0])

    end = run_wave(SB_STEP, first_wave_steps, True)
    _, end, _ = lax.while_loop(lambda s: jnp.logical_and(s[0] < n_waves, s[2] < SB_DEAD),
                               later_wave, (jnp.int32(1), jnp.int32(end), jnp.float32(0.0)))
    values(end - 2, SB_STEP - 2)
    weights(end - 1, SB_STEP - 1, False, False)
    values(end - 1, SB_STEP - 1)
    o_ref[...] = (acc_ref[...] * _silu(g_ref[...].astype(F32))).astype(BF16)


def _sb_attention(proj, b, s):
    nq = s // SB_BLOCK
    assert SB_BLOCK == 2 * LANES
    table, waves, first_wave_steps = _sb_schedule(nq)
    t = SB_BLOCK
    seq = lambda col: pl.BlockSpec((s, LANES), lambda bi, p, tab, wav: (bi, col + p))
    return pl.pallas_call(
        functools.partial(_sb_kernel, n_waves=nq, first_wave_steps=first_wave_steps),
        grid_spec=pltpu.PrefetchScalarGridSpec(
            num_scalar_prefetch=2,
            grid=(b, SB_PAIRS),
            in_specs=[seq(COL_SB_Q), seq(COL_SB_K), seq(COL_SB_V), seq(COL_SB_G)],
            out_specs=pl.BlockSpec((s, LANES), lambda bi, p, tab, wav: (bi, p)),
            scratch_shapes=[pltpu.VMEM((t, t), BF16),
                            pltpu.VMEM((t, t), F32),
                            pltpu.VMEM((SB_STEP, 2 * t, t), F32),
                            pltpu.VMEM((SB_STEP, 2 * t, t), BF16),
                            pltpu.VMEM((SB_STEP, 2 * t, t), BF16),
                            pltpu.VMEM((nq + 2, 2 * t, LANES), F32),
                            pltpu.VMEM((2, 2 * t, LANES), F32),
                            pltpu.VMEM((s, LANES), F32)]),
        out_shape=jax.ShapeDtypeStruct((b * s, SB_WIDTH), BF16),
        compiler_params=pltpu.CompilerParams(
            dimension_semantics=("arbitrary", "arbitrary"),
            vmem_limit_bytes=VMEM_LIMIT),
        name="sb_attention",
    )(table, waves, proj, proj, proj, proj)


def _ret_kernel(q_ref, k_ref, v_ref, g_ref, ng_ref, lg_ref, o_ref, kv_ref, state_ref):
    c = RET_CHUNK
    first, second = _head_masks()
    nchunks = q_ref.shape[0] // c

    lg_lane = lg_ref[0]
    lg_a = lg_lane[:, 0:1]
    lg_b = lg_lane[:, HEAD_DIM:HEAD_DIM + 1]

    ri = lax.broadcasted_iota(jnp.int32, (c, c), 0)
    ci = lax.broadcasted_iota(jnp.int32, (c, c), 1)
    diff = (ri - ci).astype(F32)
    keep = ri >= ci
    decay_a = jnp.where(keep, jnp.exp(lg_a * jnp.maximum(diff, 0.0)), 0.0)
    decay_b = jnp.where(keep, jnp.exp(lg_b * jnp.maximum(diff, 0.0)), 0.0)
    idx = lax.broadcasted_iota(jnp.int32, (c, LANES), 0).astype(F32)
    q_decay = jnp.exp(lg_lane * (idx + 1.0))
    k_decay = jnp.exp(lg_lane * (c - 1.0 - idx))
    si = lax.broadcasted_iota(jnp.int32, (LANES, LANES), 0)
    sj = lax.broadcasted_iota(jnp.int32, (LANES, LANES), 1)
    same_head = (si < HEAD_DIM) == (sj < HEAD_DIM)
    state_decay = jnp.where(same_head, jnp.exp(lg_lane * float(c)), 0.0)
    norm_g = ng_ref[0]

    chunk_rows = [slice(n * c, (n + 1) * c) for n in range(nchunks)]

    for n, rows in enumerate(chunk_rows):
        kd = (k_ref[rows, :].astype(F32) * k_decay).astype(BF16)
        kv_ref[n] = jnp.where(same_head, _dot_tn(kd, v_ref[rows, :]), 0.0)

    state = jnp.zeros((LANES, LANES), F32)
    for n in range(nchunks):
        state_ref[n] = state.astype(BF16)
        state = state * state_decay + kv_ref[n]

    decay2 = jnp.concatenate([decay_a, decay_b], axis=1)
    for n, rows in enumerate(chunk_rows):
        qb = q_ref[rows, :]
        kb = k_ref[rows, :]
        v = v_ref[rows, :]
        k2 = jnp.concatenate([jnp.where(first, kb, 0), jnp.where(second, kb, 0)], axis=0)
        v2 = jnp.concatenate([jnp.where(first, v, 0), jnp.where(second, v, 0)], axis=0)
        scores = (_dot_nt(qb, k2) * decay2).astype(BF16)
        out = _dot(scores, v2) + _dot(qb, state_ref[n]) * q_decay
        inv = lax.rsqrt(_pair_mean_square(out, first) + EPS)
        y = out * inv * norm_g
        o_ref[rows, :] = (y * _silu(g_ref[rows, :].astype(F32))).astype(BF16)


def _retention(proj, ret_norm_g, layer, b, s):
    seq = lambda col: pl.BlockSpec((s, LANES), lambda bi, p: (bi, col + p))
    gamma = 1.0 - 2.0 ** (-5.0 - jnp.arange(RET_HEADS, dtype=F32))
    log_gamma = jnp.repeat(jnp.log(gamma), HEAD_DIM).reshape(RET_PAIRS, 1, LANES)
    return pl.pallas_call(
        _ret_kernel,
        grid=(b, RET_PAIRS),
        in_specs=[seq(COL_RET_Q), seq(COL_RET_K), seq(COL_RET_V), seq(COL_RET_G),
                  pl.BlockSpec((1, 1, LANES), lambda bi, p: (layer, 0, p)),
                  pl.BlockSpec((1, 1, LANES), lambda bi, p: (p, 0, 0))],
        out_specs=pl.BlockSpec((s, LANES), lambda bi, p: (bi, p)),
        out_shape=jax.ShapeDtypeStruct((b * s, RET_WIDTH), BF16),
        scratch_shapes=[pltpu.VMEM((s // RET_CHUNK, LANES, LANES), F32),
                        pltpu.VMEM((s // RET_CHUNK, LANES, LANES), BF16)],
        compiler_params=pltpu.CompilerParams(vmem_limit_bytes=VMEM_LIMIT),
        name="retention",
    )(proj, proj, proj, proj, ret_norm_g, log_gamma)


def _mem_kv_kernel(mem_ref, g_ref, w_ref, kg_ref, k_ref, v_ref):
    first, _ = _head_masks()
    x = mem_ref[...]
    ms = jnp.mean(x * x, axis=-1, keepdims=True)
    h = (x * lax.rsqrt(ms + EPS) * g_ref[0]).astype(BF16)
    kv = _dot(h, w_ref[0].astype(BF16))
    kg = kg_ref[0]
    for pair in range(MEM_PAIRS):
        cols = slice(pair * LANES, (pair + 1) * LANES)
        kp = kv[:, cols]
        kn = kp * lax.rsqrt(_pair_mean_square(kp, first) + EPS) * kg
        k_ref[0, :, cols] = kn.astype(BF16)
    v_ref[0] = kv[:, MEM_WIDTH:].astype(BF16)


def _mem_kv(mem2, mem_norm_g, w_mem_kv, k_norm_g):
    depth = w_mem_kv.shape[0]
    rows = mem2.shape[0]
    out = jax.ShapeDtypeStruct((depth, rows, MEM_WIDTH), BF16)
    return pl.pallas_call(
        _mem_kv_kernel,
        grid=(depth,),
        in_specs=[pl.BlockSpec((rows, D_MODEL), lambda l: (0, 0)),
                  pl.BlockSpec((1, 1, D_MODEL), lambda l: (l, 0, 0)),
                  pl.BlockSpec((1, D_MODEL, 2 * MEM_WIDTH), lambda l: (l, 0, 0)),
                  pl.BlockSpec((1, 1, LANES), lambda l: (l, 0, 0))],
        out_specs=[pl.BlockSpec((1, rows, MEM_WIDTH), lambda l: (l, 0, 0)),
                   pl.BlockSpec((1, rows, MEM_WIDTH), lambda l: (l, 0, 0))],
        out_shape=[out, out],
        compiler_params=pltpu.CompilerParams(vmem_limit_bytes=VMEM_LIMIT),
        name="mem_kv",
    )(mem2, mem_norm_g, w_mem_kv, k_norm_g)


def _mem_attn_kernel(q_ref, g_ref, k_ref, v_ref, qg_ref, o_ref, e_ref, den_ref):
    first, second = _head_masks()
    tile = MEM_ROWS
    n_tiles = q_ref.shape[0] // tile
    k = k_ref[0]
    v = v_ref[0]

    def probabilities(i, slot):
        rows = slice(i * tile, (i + 1) * tile)
        q = q_ref[rows, :].astype(F32)
        inv = lax.rsqrt(_pair_mean_square(q, first) + EPS) * QK_SCALE
        qn = (q * inv * qg_ref[0]).astype(BF16)
        q2 = jnp.concatenate([jnp.where(first, qn, 0), jnp.where(second, qn, 0)], axis=0)
        sc = _dot_nt(q2, k)
        e = jnp.exp(sc - jnp.max(sc, axis=-1, keepdims=True))
        den_ref[slot] = jnp.broadcast_to(jnp.sum(e, axis=-1, keepdims=True), (2 * tile, LANES))
        e_ref[slot] = e.astype(BF16)

    def output(i, slot):
        rows = slice(i * tile, (i + 1) * tile)
        pv = _dot(e_ref[slot], v) / den_ref[slot]
        out = jnp.where(first, pv[:tile], pv[tile:])
        o_ref[rows, :] = (out * _silu(g_ref[rows, :].astype(F32))).astype(BF16)

    for i in range(n_tiles + 1):
        if i >= 1:
            output(i - 1, (i - 1) % 2)
        if i < n_tiles:
            probabilities(i, i % 2)


def _mem_attention(proj, mk, mv, q_norm_g, layer, b, s, tokens):
    seq = lambda col: pl.BlockSpec((s, LANES), lambda bi, p: (bi, col + p))
    kv_spec = pl.BlockSpec((1, tokens, LANES), lambda bi, p: (layer, bi, p))
    return pl.pallas_call(
        _mem_attn_kernel,
        grid=(b, MEM_PAIRS),
        in_specs=[seq(COL_MEM_Q), seq(COL_MEM_G), kv_spec, kv_spec,
                  pl.BlockSpec((1, 1, LANES), lambda bi, p: (layer, 0, 0))],
        out_specs=pl.BlockSpec((s, LANES), lambda bi, p: (bi, p)),
        out_shape=jax.ShapeDtypeStruct((b * s, MEM_WIDTH), BF16),
        scratch_shapes=[pltpu.VMEM((2, 2 * MEM_ROWS, tokens), BF16),
                        pltpu.VMEM((2, 2 * MEM_ROWS, LANES), F32)],
        compiler_params=pltpu.CompilerParams(vmem_limit_bytes=VMEM_LIMIT),
        name="mem_attention",
    )(proj, proj, mk, mv, q_norm_g)


def kernel(x, mem, positions, norm_g, w_in, w_out, mem_norm_g, w_mem_kv, mem_q_norm_g,
           mem_k_norm_g, ret_norm_g):
    b, s, d = x.shape
    tokens = mem.shape[1]
    depth = w_in.shape[0]
    assert d == D_MODEL and w_in.shape[2] == IN_WIDTH
    assert s % SB_BLOCK == 0 and s % RET_CHUNK == 0 and s % MEM_ROWS == 0
    assert (b * s) % PROJ_ROWS == 0 and (b * s) % OUT_ROWS == 0

    norm_g3 = norm_g[:, None, :]
    ret_norm_g3 = ret_norm_g[:, None, :]
    q_norm_g3 = jnp.tile(mem_q_norm_g, (1, PAIR))[:, None, :]
    k_norm_g3 = jnp.tile(mem_k_norm_g, (1, PAIR))[:, None, :]

    cos, sin = _rope_tables(positions)
    mk_all, mv_all = _mem_kv(mem.reshape(b * tokens, d), mem_norm_g[:, None, :], w_mem_kv, k_norm_g3)

    x2 = x.reshape(b * s, d)
    proj = _in_proj(x2, norm_g3, w_in, 0, cos, sin)
    for l in range(depth):
        sb_o = _sb_attention(proj, b, s)
        ret_o = _retention(proj, ret_norm_g3, l, b, s)
        mem_o = _mem_attention(proj, mk_all, mv_all, q_norm_g3, l, b, s, tokens)
        if l + 1 < depth:
            x2, proj = _out_in_proj(sb_o, ret_o, mem_o, w_out, l, x2, norm_g3, w_in, cos, sin)
        else:
            x2 = _out_proj(sb_o, ret_o, mem_o, w_out, l, x2)
    return x2.reshape(b, s, d)
```

```python
import functools
import math

import jax
import jax.numpy as jnp
from jax import lax
from jax.experimental import pallas as pl
from jax.experimental.pallas import tpu as pltpu

D_MODEL = 1024
HEAD_DIM = 64
SB_HEADS = 6
RET_HEADS = 6
MEM_HEADS = 4
SB_WIDTH = SB_HEADS * HEAD_DIM
RET_WIDTH = RET_HEADS * HEAD_DIM
MEM_WIDTH = MEM_HEADS * HEAD_DIM
MIX_WIDTH = SB_WIDTH + RET_WIDTH + MEM_WIDTH
IN_WIDTH = 4 * SB_WIDTH + 4 * RET_WIDTH + 2 * MEM_WIDTH
ROPE_BASE = 10000.0
EPS = 1e-6
QK_SCALE = HEAD_DIM ** -0.5
SB_Q_SCALE = QK_SCALE * math.log2(math.e)

LANES = 128
PAIR = LANES // HEAD_DIM
SB_PAIRS = SB_HEADS // PAIR
RET_PAIRS = RET_HEADS // PAIR
MEM_PAIRS = MEM_HEADS // PAIR

COL_SB_Q = 0
COL_SB_K = COL_SB_Q + SB_PAIRS
COL_SB_V = COL_SB_K + SB_PAIRS
COL_SB_G = COL_SB_V + SB_PAIRS
COL_RET_Q = COL_SB_G + SB_PAIRS
COL_RET_K = COL_RET_Q + RET_PAIRS
COL_RET_V = COL_RET_K + RET_PAIRS
COL_RET_G = COL_RET_V + RET_PAIRS
COL_MEM_Q = COL_RET_G + RET_PAIRS
COL_MEM_G = COL_MEM_Q + MEM_PAIRS

PROJ_ROWS = 512
OUT_ROWS = 1024
PROJ_COLS = 512
SB_BLOCK = 256
RET_CHUNK = 128
MEM_ROWS = 512
V7X_VMEM_BYTES = 64 * 1024 * 1024
VMEM_LIMIT = V7X_VMEM_BYTES * 3 // 4

F32 = jnp.float32
BF16 = jnp.bfloat16


def _dot(a, b):
    return jnp.dot(a, b, preferred_element_type=F32)


def _dot_nt(a, b):
    return lax.dot_general(a, b, (((1,), (1,)), ((), ())), preferred_element_type=F32)


def _dot_tn(a, b):
    return lax.dot_general(a, b, (((0,), (0,)), ((), ())), preferred_element_type=F32)


def _silu(g):
    return g / (1.0 + jnp.exp(-g))


def _head_masks():
    lane = lax.broadcasted_iota(jnp.int32, (1, LANES), 1)
    first = lane < HEAD_DIM
    return first, jnp.logical_not(first)


def _pair_mean_square(t, first):
    sq = t * t
    s_a = jnp.sum(jnp.where(first, sq, 0.0), axis=-1, keepdims=True)
    s_b = jnp.sum(jnp.where(first, 0.0, sq), axis=-1, keepdims=True)
    return jnp.where(first, s_a, s_b) * (1.0 / HEAD_DIM)


def _rope_kernel(pos_ref, invf_ref, trig_ref):
    ang = pos_ref[...].astype(F32) * invf_ref[...]
    lane = lax.broadcasted_iota(jnp.int32, (1, LANES), 1)
    phase = jnp.where((lane % HEAD_DIM) < HEAD_DIM // 2, 0.0, math.pi / 2)
    trig_ref[...] = jnp.cos(ang - phase)


def _rope_tables(positions):
    b, s = positions.shape
    half = HEAD_DIM // 2
    inv_freq = ROPE_BASE ** (-jnp.arange(half, dtype=F32) / half)
    invf = jnp.tile(inv_freq, LANES // half)[None, :]
    pos = positions.reshape(b * s, 1)
    rows = s
    return pl.pallas_call(
        _rope_kernel,
        grid=(b * s // rows,),
        in_specs=[pl.BlockSpec((rows, 1), lambda i: (i, 0)),
                  pl.BlockSpec((1, LANES), lambda i: (0, 0))],
        out_specs=pl.BlockSpec((rows, LANES), lambda i: (i, 0)),
        out_shape=jax.ShapeDtypeStruct((b * s, LANES), F32),
        name="rope_tables",
    )(pos, invf)


def _half_lanes(t):
    lane = lax.broadcasted_iota(jnp.int32, (1, LANES), 1)
    half = HEAD_DIM // 2
    return (lane % HEAD_DIM) < half, pltpu.roll(t, LANES - half, 1), pltpu.roll(t, half, 1)


def _swap_halves(t):
    lower, from_upper, from_lower = _half_lanes(t)
    return jnp.where(lower, from_upper, from_lower)


def _norm_project(x, g_ref, w_ref, trig_ref, o_ref):
    ms = jnp.mean(x * x, axis=-1, keepdims=True)
    h = (x * lax.rsqrt(ms + EPS) * g_ref[0]).astype(BF16)
    trig = trig_ref[...]
    lower, from_upper, from_lower = _half_lanes(trig)
    cos = jnp.where(lower, trig, from_lower)
    sin = jnp.where(lower, -from_upper, trig)
    blocks_per_dot = PROJ_COLS // LANES
    for j in range(IN_WIDTH // PROJ_COLS):
        y = _dot(h, w_ref[0, :, j * PROJ_COLS:(j + 1) * PROJ_COLS].astype(BF16))
        for b in range(blocks_per_dot):
            blk = j * blocks_per_dot + b
            piece = y[:, b * LANES:(b + 1) * LANES]
            if blk < COL_SB_K:
                piece = piece * SB_Q_SCALE
            elif COL_RET_Q <= blk < COL_RET_V:
                piece = piece * cos + _swap_halves(piece) * sin
                if blk >= COL_RET_K:
                    piece = piece * QK_SCALE
            o_ref[:, blk * LANES:(blk + 1) * LANES] = piece.astype(BF16)


def _in_proj_kernel(x_ref, g_ref, w_ref, trig_ref, o_ref):
    _norm_project(x_ref[...], g_ref, w_ref, trig_ref, o_ref)


def _mix(sb_ref, ret_ref, mem_ref, w_ref, x_ref):
    mixed = jnp.concatenate([sb_ref[...], ret_ref[...], mem_ref[...]], axis=1)
    return x_ref[...] + _dot(mixed, w_ref[0].astype(BF16))


def _out_proj_kernel(sb_ref, ret_ref, mem_ref, w_ref, x_ref, o_ref):
    o_ref[...] = _mix(sb_ref, ret_ref, mem_ref, w_ref, x_ref)


def _out_in_proj_kernel(sb_ref, ret_ref, mem_ref, w_out_ref, x_ref, g_ref, w_in_ref, trig_ref,
                        x_out_ref, proj_ref):
    x_new = _mix(sb_ref, ret_ref, mem_ref, w_out_ref, x_ref)
    x_out_ref[...] = x_new
    _norm_project(x_new, g_ref, w_in_ref, trig_ref, proj_ref)


def _row_tiles(width, rows=PROJ_ROWS):
    return pl.BlockSpec((rows, width), lambda i: (i, 0))


def _layer_param(shape, layer):
    return pl.BlockSpec((1,) + shape, lambda i: (layer, 0, 0), pipeline_mode=pl.Buffered(1))


def _in_proj(x2, g, w, layer, trig):
    m = x2.shape[0]
    return pl.pallas_call(
        _in_proj_kernel,
        grid=(m // PROJ_ROWS,),
        in_specs=[_row_tiles(D_MODEL), _layer_param((1, D_MODEL), layer),
                  _layer_param((D_MODEL, IN_WIDTH), layer), _row_tiles(LANES)],
        out_specs=_row_tiles(IN_WIDTH),
        out_shape=jax.ShapeDtypeStruct((m, IN_WIDTH), BF16),
        compiler_params=pltpu.CompilerParams(vmem_limit_bytes=VMEM_LIMIT),
        name="in_proj",
    )(x2, g, w, trig)


def _out_proj(sb_o, ret_o, mem_o, w, layer, x2):
    m = x2.shape[0]
    return pl.pallas_call(
        _out_proj_kernel,
        grid=(m // OUT_ROWS,),
        in_specs=[_row_tiles(SB_WIDTH, OUT_ROWS), _row_tiles(RET_WIDTH, OUT_ROWS),
                  _row_tiles(MEM_WIDTH, OUT_ROWS), _layer_param((MIX_WIDTH, D_MODEL), layer),
                  _row_tiles(D_MODEL, OUT_ROWS)],
        out_specs=_row_tiles(D_MODEL, OUT_ROWS),
        out_shape=jax.ShapeDtypeStruct((m, D_MODEL), F32),
        compiler_params=pltpu.CompilerParams(vmem_limit_bytes=VMEM_LIMIT),
        name="out_proj",
    )(sb_o, ret_o, mem_o, w, x2)


def _out_in_proj(sb_o, ret_o, mem_o, w_out, layer, x2, g, w_in, trig):
    m = x2.shape[0]
    return pl.pallas_call(
        _out_in_proj_kernel,
        grid=(m // PROJ_ROWS,),
        in_specs=[_row_tiles(SB_WIDTH), _row_tiles(RET_WIDTH), _row_tiles(MEM_WIDTH),
                  _layer_param((MIX_WIDTH, D_MODEL), layer), _row_tiles(D_MODEL),
                  _layer_param((1, D_MODEL), layer + 1), _layer_param((D_MODEL, IN_WIDTH), layer + 1),
                  _row_tiles(LANES)],
        out_specs=[_row_tiles(D_MODEL), _row_tiles(IN_WIDTH)],
        out_shape=[jax.ShapeDtypeStruct((m, D_MODEL), F32),
                   jax.ShapeDtypeStruct((m, IN_WIDTH), BF16)],
        compiler_params=pltpu.CompilerParams(vmem_limit_bytes=VMEM_LIMIT),
        name="out_in_proj",
    )(sb_o, ret_o, mem_o, w_out, x2, g, w_in, trig)


SB_LAG = 2
SB_STEP = 4
SB_MASKED = -1e30
SB_DEAD = 160.0
SB_IDLE_CARRY = 1e30


def _sb_schedule(nq):
    zero_slot, junk_slot = nq, nq + 1
    idle = (0, 0, junk_slot, junk_slot, 1)
    items = [idle] * SB_STEP
    waves = []
    for w in range(nq):
        start = len(items)
        for qb in range(nq - 1, w - 1, -1):
            src = zero_slot if w == 0 else qb
            items.append((qb, qb - w, src, qb, 0 if qb > w else 1))
        items += [idle] * (-(len(items) - start) % SB_STEP)
        waves.append((start, (len(items) - start) // SB_STEP))
    return jnp.asarray(items, jnp.int32).T, jnp.asarray(waves, jnp.int32).T, waves[0][1]


def _sb_kernel(tab_ref, wave_ref, q_ref, k_ref, v_ref, g_ref, o_ref,
               tri_ref, bias_ref, z_ref, sp_ref, w_ref, r_ref, m_ref, acc_ref, *, n_waves,
               first_wave_steps):
    t = SB_BLOCK
    first, second = _head_masks()
    n_carry = r_ref.shape[0]

    @pl.when(jnp.logical_and(pl.program_id(0) == 0, pl.program_id(1) == 0))
    def _constants():
        row = lax.broadcasted_iota(jnp.int32, (t, t), 0)
        col = lax.broadcasted_iota(jnp.int32, (t, t), 1)
        tri_ref[...] = (row >= col).astype(BF16)
        bias_ref[...] = jnp.where(col < row, 0.0, SB_MASKED)
        r_ref[n_carry - 2] = jnp.zeros(r_ref.shape[1:], F32)

    z_ref[SB_STEP - 1] = jnp.full(z_ref.shape[1:], SB_MASKED, F32)
    sp_ref[SB_STEP - 1] = jnp.zeros(sp_ref.shape[1:], BF16)
    for slot in range(SB_STEP - SB_LAG, SB_STEP):
        w_ref[slot] = jnp.zeros(w_ref.shape[1:], BF16)
    r_ref[n_carry - 1] = jnp.full(r_ref.shape[1:], SB_IDLE_CARRY, F32)
    m_ref[1] = jnp.zeros(m_ref.shape[1:], F32)
    acc_ref[...] = jnp.zeros_like(acc_ref)

    def rows(blk):
        return pl.ds(pl.multiple_of(blk * t, t), t)

    def softplus2(z):
        return jnp.maximum(z, 0.0) + jnp.log2(1.0 + jnp.exp2(-jnp.abs(z)))

    half = t // 2

    def lower_rows(a):
        return [a[r0:r0 + half] for r0 in (half, t + half)]

    def zero_upper_rows(lower):
        zeros = jnp.zeros((half, half), lower[0].dtype)
        return jnp.concatenate([zeros, lower[0], zeros, lower[1]], axis=0)

    def scores(j, slot, diagonal):
        qb, kb = tab_ref[0, j], tab_ref[1, j]
        q = q_ref[rows(qb), :]
        q2 = jnp.concatenate([jnp.where(first, q, 0), jnp.where(second, q, 0)], axis=0)
        z = _dot_nt(q2, k_ref[rows(kb), :])
        if diagonal:
            bias = bias_ref[...]
            z = z + jnp.concatenate([bias, bias], axis=0)
            sp = jnp.concatenate(
                [softplus2(z[:, :half]),
                 zero_upper_rows([softplus2(p) for p in lower_rows(z[:, half:])])], axis=1)
        else:
            sp = softplus2(z)
        z_ref[slot] = z
        sp_ref[slot] = sp.astype(BF16)

    def weights(j, slot, track_minimum, diagonal):
        cum = _dot(sp_ref[slot], tri_ref[...])
        r = r_ref[tab_ref[2, j]]
        z = z_ref[slot]
        if diagonal:
            pieces = zip(lower_rows(z[:, half:]), lower_rows(cum[:, half:]), lower_rows(r))
            halves = [jnp.exp2(z[:, :half] - cum[:, :half] - r),
                      zero_upper_rows([jnp.exp2(a - b - c) for a, b, c in pieces])]
        else:
            halves = [jnp.exp2(z[:, c:c + LANES] - cum[:, c:c + LANES] - r) for c in (0, LANES)]
        w_ref[slot] = jnp.concatenate(halves, axis=1).astype(BF16)
        r_new = r + cum[:, 0:1]
        r_ref[tab_ref[3, j]] = r_new
        if track_minimum:
            m_slot = tab_ref[4, j]
            m_ref[m_slot] = jnp.minimum(m_ref[m_slot], r_new)

    def values(j, slot):
        qb, kb = tab_ref[0, j], tab_ref[1, j]
        pv = _dot(w_ref[slot], v_ref[rows(kb), :])
        acc_ref[rows(qb), :] += jnp.where(first, pv[:t], pv[t:])

    def run_wave(start, n_steps, diagonal):
        def step(n, _):
            for i in range(SB_STEP):
                c = start + SB_STEP * n + i
                values(c - 2, (i - 2) % SB_STEP)
                weights(c - 1, (i - 1) % SB_STEP, not diagonal, diagonal)
                scores(c, i, diagonal)
            return 0

        lax.fori_loop(0, n_steps, step, 0)
        return start + SB_STEP * n_steps

    def later_wave(state):
        w = state[0]
        m_ref[0] = jnp.full(m_ref.shape[1:], jnp.inf, F32)
        end = run_wave(wave_ref[0, w], wave_ref[1, w], False)
        return w + 1, end, jnp.min(m_ref[0])

    end = run_wave(SB_STEP, first_wave_steps, True)
    _, end, _ = lax.while_loop(lambda s: jnp.logical_and(s[0] < n_waves, s[2] < SB_DEAD),
                               later_wave, (jnp.int32(1), jnp.int32(end), jnp.float32(0.0)))
    values(end - 2, SB_STEP - 2)
    weights(end - 1, SB_STEP - 1, False, False)
    values(end - 1, SB_STEP - 1)
    o_ref[...] = (acc_ref[...] * _silu(g_ref[...].astype(F32))).astype(BF16)


def _sb_attention(proj, b, s):
    nq = s // SB_BLOCK
    assert SB_BLOCK == 2 * LANES
    table, waves, first_wave_steps = _sb_schedule(nq)
    t = SB_BLOCK
    seq = lambda col: pl.BlockSpec((s, LANES), lambda bi, p, tab, wav: (bi, col + p))
    return pl.pallas_call(
        functools.partial(_sb_kernel, n_waves=nq, first_wave_steps=first_wave_steps),
        grid_spec=pltpu.PrefetchScalarGridSpec(
            num_scalar_prefetch=2,
            grid=(b, SB_PAIRS),
            in_specs=[seq(COL_SB_Q), seq(COL_SB_K), seq(COL_SB_V), seq(COL_SB_G)],
            out_specs=pl.BlockSpec((s, LANES), lambda bi, p, tab, wav: (bi, p)),
            scratch_shapes=[pltpu.VMEM((t, t), BF16),
                            pltpu.VMEM((t, t), F32),
                            pltpu.VMEM((SB_STEP, 2 * t, t), F32),
                            pltpu.VMEM((SB_STEP, 2 * t, t), BF16),
                            pltpu.VMEM((SB_STEP, 2 * t, t), BF16),
                            pltpu.VMEM((nq + 2, 2 * t, LANES), F32),
                            pltpu.VMEM((2, 2 * t, LANES), F32),
                            pltpu.VMEM((s, LANES), F32)]),
        out_shape=jax.ShapeDtypeStruct((b * s, SB_WIDTH), BF16),
        compiler_params=pltpu.CompilerParams(
            dimension_semantics=("arbitrary", "arbitrary"),
            vmem_limit_bytes=VMEM_LIMIT),
        name="sb_attention",
    )(table, waves, proj, proj, proj, proj)


def _ret_kernel(q_ref, k_ref, v_ref, g_ref, ng_ref, lg_ref, o_ref, kv_ref, state_ref):
    c = RET_CHUNK
    first, second = _head_masks()
    nchunks = q_ref.shape[0] // c

    lg_lane = lg_ref[0]
    lg_a = lg_lane[:, 0:1]
    lg_b = lg_lane[:, HEAD_DIM:HEAD_DIM + 1]

    ri = lax.broadcasted_iota(jnp.int32, (c, c), 0)
    ci = lax.broadcasted_iota(jnp.int32, (c, c), 1)
    diff = (ri - ci).astype(F32)
    keep = ri >= ci
    decay_a = jnp.where(keep, jnp.exp(lg_a * jnp.maximum(diff, 0.0)), 0.0)
    decay_b = jnp.where(keep, jnp.exp(lg_b * jnp.maximum(diff, 0.0)), 0.0)
    idx = lax.broadcasted_iota(jnp.int32, (c, LANES), 0).astype(F32)
    q_decay = jnp.exp(lg_lane * (idx + 1.0))
    k_decay = jnp.exp(lg_lane * (c - 1.0 - idx))
    si = lax.broadcasted_iota(jnp.int32, (LANES, LANES), 0)
    sj = lax.broadcasted_iota(jnp.int32, (LANES, LANES), 1)
    same_head = (si < HEAD_DIM) == (sj < HEAD_DIM)
    state_decay = jnp.where(same_head, jnp.exp(lg_lane * float(c)), 0.0)
    norm_g = ng_ref[0]

    chunk_rows = [slice(n * c, (n + 1) * c) for n in range(nchunks)]

    for n, rows in enumerate(chunk_rows):
        kd = (k_ref[rows, :].astype(F32) * k_decay).astype(BF16)
        kv_ref[n] = jnp.where(same_head, _dot_tn(kd, v_ref[rows, :]), 0.0)

    state = jnp.zeros((LANES, LANES), F32)
    for n in range(nchunks):
        state_ref[n] = state.astype(BF16)
        state = state * state_decay + kv_ref[n]

    decay2 = jnp.concatenate([decay_a, decay_b], axis=1)
    for n, rows in enumerate(chunk_rows):
        qb = q_ref[rows, :]
        kb = k_ref[rows, :]
        v = v_ref[rows, :]
        k2 = jnp.concatenate([jnp.where(first, kb, 0), jnp.where(second, kb, 0)], axis=0)
        v2 = jnp.concatenate([jnp.where(first, v, 0), jnp.where(second, v, 0)], axis=0)
        scores = (_dot_nt(qb, k2) * decay2).astype(BF16)
        out = _dot(scores, v2) + _dot(qb, state_ref[n]) * q_decay
        inv = lax.rsqrt(_pair_mean_square(out, first) + EPS)
        y = out * inv * norm_g
        o_ref[rows, :] = (y * _silu(g_ref[rows, :].astype(F32))).astype(BF16)


def _retention(proj, ret_norm_g, layer, b, s):
    seq = lambda col: pl.BlockSpec((s, LANES), lambda bi, p: (bi, col + p))
    gamma = 1.0 - 2.0 ** (-5.0 - jnp.arange(RET_HEADS, dtype=F32))
    log_gamma = jnp.repeat(jnp.log(gamma), HEAD_DIM).reshape(RET_PAIRS, 1, LANES)
    return pl.pallas_call(
        _ret_kernel,
        grid=(b, RET_PAIRS),
        in_specs=[seq(COL_RET_Q), seq(COL_RET_K), seq(COL_RET_V), seq(COL_RET_G),
                  pl.BlockSpec((1, 1, LANES), lambda bi, p: (layer, 0, p)),
                  pl.BlockSpec((1, 1, LANES), lambda bi, p: (p, 0, 0))],
        out_specs=pl.BlockSpec((s, LANES), lambda bi, p: (bi, p)),
        out_shape=jax.ShapeDtypeStruct((b * s, RET_WIDTH), BF16),
        scratch_shapes=[pltpu.VMEM((s // RET_CHUNK, LANES, LANES), F32),
                        pltpu.VMEM((s // RET_CHUNK, LANES, LANES), BF16)],
        compiler_params=pltpu.CompilerParams(vmem_limit_bytes=VMEM_LIMIT),
        name="retention",
    )(proj, proj, proj, proj, ret_norm_g, log_gamma)


def _mem_kv_kernel(mem_ref, g_ref, w_ref, kg_ref, k_ref, v_ref):
    first, _ = _head_masks()
    x = mem_ref[...]
    ms = jnp.mean(x * x, axis=-1, keepdims=True)
    h = (x * lax.rsqrt(ms + EPS) * g_ref[0]).astype(BF16)
    kv = _dot(h, w_ref[0].astype(BF16))
    kg = kg_ref[0]
    for pair in range(MEM_PAIRS):
        cols = slice(pair * LANES, (pair + 1) * LANES)
        kp = kv[:, cols]
        kn = kp * lax.rsqrt(_pair_mean_square(kp, first) + EPS) * kg
        k_ref[0, :, cols] = kn.astype(BF16)
    v_ref[0] = kv[:, MEM_WIDTH:].astype(BF16)


def _mem_kv(mem2, mem_norm_g, w_mem_kv, k_norm_g):
    depth = w_mem_kv.shape[0]
    rows = mem2.shape[0]
    out = jax.ShapeDtypeStruct((depth, rows, MEM_WIDTH), BF16)
    return pl.pallas_call(
        _mem_kv_kernel,
        grid=(depth,),
        in_specs=[pl.BlockSpec((rows, D_MODEL), lambda l: (0, 0)),
                  pl.BlockSpec((1, 1, D_MODEL), lambda l: (l, 0, 0)),
                  pl.BlockSpec((1, D_MODEL, 2 * MEM_WIDTH), lambda l: (l, 0, 0)),
                  pl.BlockSpec((1, 1, LANES), lambda l: (l, 0, 0))],
        out_specs=[pl.BlockSpec((1, rows, MEM_WIDTH), lambda l: (l, 0, 0)),
                   pl.BlockSpec((1, rows, MEM_WIDTH), lambda l: (l, 0, 0))],
        out_shape=[out, out],
        compiler_params=pltpu.CompilerParams(vmem_limit_bytes=VMEM_LIMIT),
        name="mem_kv",
    )(mem2, mem_norm_g, w_mem_kv, k_norm_g)


def _mem_attn_kernel(q_ref, g_ref, k_ref, v_ref, qg_ref, o_ref, e_ref, den_ref):
    first, second = _head_masks()
    tile = MEM_ROWS
    n_tiles = q_ref.shape[0] // tile
    k = k_ref[0]
    v = v_ref[0]

    def probabilities(i, slot):
        rows = slice(i * tile, (i + 1) * tile)
        q = q_ref[rows, :].astype(F32)
        inv = lax.rsqrt(_pair_mean_square(q, first) + EPS) * QK_SCALE
        qn = (q * inv * qg_ref[0]).astype(BF16)
        q2 = jnp.concatenate([jnp.where(first, qn, 0), jnp.where(second, qn, 0)], axis=0)
        sc = _dot_nt(q2, k)
        e = jnp.exp(sc - jnp.max(sc, axis=-1, keepdims=True))
        den_ref[slot] = jnp.broadcast_to(jnp.sum(e, axis=-1, keepdims=True), (2 * tile, LANES))
        e_ref[slot] = e.astype(BF16)

    def output(i, slot):
        rows = slice(i * tile, (i + 1) * tile)
        pv = _dot(e_ref[slot], v) / den_ref[slot]
        out = jnp.where(first, pv[:tile], pv[tile:])
        o_ref[rows, :] = (out * _silu(g_ref[rows, :].astype(F32))).astype(BF16)

    for i in range(n_tiles + 1):
        if i >= 1:
            output(i - 1, (i - 1) % 2)
        if i < n_tiles:
            probabilities(i, i % 2)


def _mem_attention(proj, mk, mv, q_norm_g, layer, b, s, tokens):
    seq = lambda col: pl.BlockSpec((s, LANES), lambda bi, p: (bi, col + p))
    kv_spec = pl.BlockSpec((1, tokens, LANES), lambda bi, p: (layer, bi, p))
    return pl.pallas_call(
        _mem_attn_kernel,
        grid=(b, MEM_PAIRS),
        in_specs=[seq(COL_MEM_Q), seq(COL_MEM_G), kv_spec, kv_spec,
                  pl.BlockSpec((1, 1, LANES), lambda bi, p: (layer, 0, 0))],
        out_specs=pl.BlockSpec((s, LANES), lambda bi, p: (bi, p)),
        out_shape=jax.ShapeDtypeStruct((b * s, MEM_WIDTH), BF16),
        scratch_shapes=[pltpu.VMEM((2, 2 * MEM_ROWS, tokens), BF16),
                        pltpu.VMEM((2, 2 * MEM_ROWS, LANES), F32)],
        compiler_params=pltpu.CompilerParams(vmem_limit_bytes=VMEM_LIMIT),
        name="mem_attention",
    )(proj, proj, mk, mv, q_norm_g)


def kernel(x, mem, positions, norm_g, w_in, w_out, mem_norm_g, w_mem_kv, mem_q_norm_g,
           mem_k_norm_g, ret_norm_g):
    b, s, d = x.shape
    tokens = mem.shape[1]
    depth = w_in.shape[0]
    assert d == D_MODEL and w_in.shape[2] == IN_WIDTH
    assert s % SB_BLOCK == 0 and s % RET_CHUNK == 0 and s % MEM_ROWS == 0
    assert (b * s) % PROJ_ROWS == 0 and (b * s) % OUT_ROWS == 0

    norm_g3 = norm_g[:, None, :]
    ret_norm_g3 = ret_norm_g[:, None, :]
    q_norm_g3 = jnp.tile(mem_q_norm_g, (1, PAIR))[:, None, :]
    k_norm_g3 = jnp.tile(mem_k_norm_g, (1, PAIR))[:, None, :]

    trig = _rope_tables(positions)
    mk_all, mv_all = _mem_kv(mem.reshape(b * tokens, d), mem_norm_g[:, None, :], w_mem_kv, k_norm_g3)

    x2 = x.reshape(b * s, d)
    proj = _in_proj(x2, norm_g3, w_in, 0, trig)
    for l in range(depth):
        sb_o = _sb_attention(proj, b, s)
        ret_o = _retention(proj, ret_norm_g3, l, b, s)
        mem_o = _mem_attention(proj, mk_all, mv_all, q_norm_g3, l, b, s, tokens)
        if l + 1 < depth:
            x2, proj = _out_in_proj(sb_o, ret_o, mem_o, w_out, l, x2, norm_g3, w_in, trig)
        else:
            x2 = _out_proj(sb_o, ret_o, mem_o, w_out, l, x2)
    return x2.reshape(b, s, d)
```

```python
import functools
import math

import jax
import jax.numpy as jnp
from jax import lax
from jax.experimental import pallas as pl
from jax.experimental.pallas import tpu as pltpu

D_MODEL = 1024
HEAD_DIM = 64
SB_HEADS = 6
RET_HEADS = 6
MEM_HEADS = 4
SB_WIDTH = SB_HEADS * HEAD_DIM
RET_WIDTH = RET_HEADS * HEAD_DIM
MEM_WIDTH = MEM_HEADS * HEAD_DIM
MIX_WIDTH = SB_WIDTH + RET_WIDTH + MEM_WIDTH
IN_WIDTH = 4 * SB_WIDTH + 4 * RET_WIDTH + 2 * MEM_WIDTH
ROPE_BASE = 10000.0
EPS = 1e-6
QK_SCALE = HEAD_DIM ** -0.5
SB_Q_SCALE = QK_SCALE * math.log2(math.e)

LANES = 128
PAIR = LANES // HEAD_DIM
SB_PAIRS = SB_HEADS // PAIR
RET_PAIRS = RET_HEADS // PAIR
MEM_PAIRS = MEM_HEADS // PAIR

COL_SB_Q = 0
COL_SB_K = COL_SB_Q + SB_PAIRS
COL_SB_V = COL_SB_K + SB_PAIRS
COL_SB_G = COL_SB_V + SB_PAIRS
COL_RET_Q = COL_SB_G + SB_PAIRS
COL_RET_K = COL_RET_Q + RET_PAIRS
COL_RET_V = COL_RET_K + RET_PAIRS
COL_RET_G = COL_RET_V + RET_PAIRS
COL_MEM_Q = COL_RET_G + RET_PAIRS
COL_MEM_G = COL_MEM_Q + MEM_PAIRS

PROJ_ROWS = 512
OUT_ROWS = 1024
PROJ_COLS = 512
SB_BLOCK = 256
RET_CHUNK = 128
MEM_ROWS = 512
V7X_VMEM_BYTES = 64 * 1024 * 1024
VMEM_LIMIT = V7X_VMEM_BYTES * 3 // 4

F32 = jnp.float32
BF16 = jnp.bfloat16


def _dot(a, b):
    return jnp.dot(a, b, preferred_element_type=F32)


def _dot_nt(a, b):
    return lax.dot_general(a, b, (((1,), (1,)), ((), ())), preferred_element_type=F32)


def _dot_tn(a, b):
    return lax.dot_general(a, b, (((0,), (0,)), ((), ())), preferred_element_type=F32)


def _silu(g):
    return g / (1.0 + jnp.exp(-g))


def _head_masks():
    lane = lax.broadcasted_iota(jnp.int32, (1, LANES), 1)
    first = lane < HEAD_DIM
    return first, jnp.logical_not(first)


def _pair_mean_square(t, first):
    sq = t * t
    s_a = jnp.sum(jnp.where(first, sq, 0.0), axis=-1, keepdims=True)
    s_b = jnp.sum(jnp.where(first, 0.0, sq), axis=-1, keepdims=True)
    return jnp.where(first, s_a, s_b) * (1.0 / HEAD_DIM)


def _rope_kernel(pos_ref, invf_ref, trig_ref):
    ang = pos_ref[...].astype(F32) * invf_ref[...]
    lane = lax.broadcasted_iota(jnp.int32, (1, LANES), 1)
    phase = jnp.where((lane % HEAD_DIM) < HEAD_DIM // 2, 0.0, math.pi / 2)
    trig_ref[...] = jnp.cos(ang - phase)


def _rope_tables(positions):
    b, s = positions.shape
    half = HEAD_DIM // 2
    inv_freq = ROPE_BASE ** (-jnp.arange(half, dtype=F32) / half)
    invf = jnp.tile(inv_freq, LANES // half)[None, :]
    pos = positions.reshape(b * s, 1)
    rows = s
    return pl.pallas_call(
        _rope_kernel,
        grid=(b * s // rows,),
        in_specs=[pl.BlockSpec((rows, 1), lambda i: (i, 0)),
                  pl.BlockSpec((1, LANES), lambda i: (0, 0))],
        out_specs=pl.BlockSpec((rows, LANES), lambda i: (i, 0)),
        out_shape=jax.ShapeDtypeStruct((b * s, LANES), F32),
        name="rope_tables",
    )(pos, invf)


def _half_lanes(t):
    lane = lax.broadcasted_iota(jnp.int32, (1, LANES), 1)
    half = HEAD_DIM // 2
    return (lane % HEAD_DIM) < half, pltpu.roll(t, LANES - half, 1), pltpu.roll(t, half, 1)


def _swap_halves(t):
    lower, from_upper, from_lower = _half_lanes(t)
    return jnp.where(lower, from_upper, from_lower)


def _norm_project(x, g_ref, w_ref, trig_ref, o_ref):
    ms = jnp.mean(x * x, axis=-1, keepdims=True)
    h = (x * lax.rsqrt(ms + EPS) * g_ref[0]).astype(BF16)
    trig = trig_ref[...]
    lower, from_upper, from_lower = _half_lanes(trig)
    cos = jnp.where(lower, trig, from_lower)
    sin = jnp.where(lower, -from_upper, trig)
    blocks_per_dot = PROJ_COLS // LANES
    for j in range(IN_WIDTH // PROJ_COLS):
        y = _dot(h, w_ref[0, :, j * PROJ_COLS:(j + 1) * PROJ_COLS].astype(BF16))
        for b in range(blocks_per_dot):
            blk = j * blocks_per_dot + b
            piece = y[:, b * LANES:(b + 1) * LANES]
            if blk < COL_SB_K:
                piece = piece * SB_Q_SCALE
            elif COL_RET_Q <= blk < COL_RET_V:
                piece = piece * cos + _swap_halves(piece) * sin
                if blk >= COL_RET_K:
                    piece = piece * QK_SCALE
            o_ref[:, blk * LANES:(blk + 1) * LANES] = piece.astype(BF16)


def _in_proj_kernel(x_ref, g_ref, w_ref, trig_ref, o_ref):
    _norm_project(x_ref[...], g_ref, w_ref, trig_ref, o_ref)


def _mix(sb_ref, ret_ref, mem_ref, sb_g_ref, w_ref, x_ref):
    sb = (sb_ref[...].astype(F32) * _silu(sb_g_ref[...].astype(F32))).astype(BF16)
    mixed = jnp.concatenate([sb, ret_ref[...], mem_ref[...]], axis=1)
    return x_ref[...] + _dot(mixed, w_ref[0].astype(BF16))


def _out_proj_kernel(sb_ref, ret_ref, mem_ref, sb_g_ref, w_ref, x_ref, o_ref):
    o_ref[...] = _mix(sb_ref, ret_ref, mem_ref, sb_g_ref, w_ref, x_ref)


def _out_in_proj_kernel(sb_ref, ret_ref, mem_ref, sb_g_ref, w_out_ref, x_ref, g_ref, w_in_ref,
                        trig_ref, x_out_ref, proj_ref):
    x_new = _mix(sb_ref, ret_ref, mem_ref, sb_g_ref, w_out_ref, x_ref)
    x_out_ref[...] = x_new
    _norm_project(x_new, g_ref, w_in_ref, trig_ref, proj_ref)


def _row_tiles(width, rows=PROJ_ROWS, col=0):
    return pl.BlockSpec((rows, width), lambda i: (i, col))


def _mixer_tiles(rows):
    assert COL_SB_G * LANES % SB_WIDTH == 0
    return [_row_tiles(SB_WIDTH, rows), _row_tiles(RET_WIDTH, rows), _row_tiles(MEM_WIDTH, rows),
            _row_tiles(SB_WIDTH, rows, COL_SB_G * LANES // SB_WIDTH)]


def _layer_param(shape, layer):
    return pl.BlockSpec((1,) + shape, lambda i: (layer, 0, 0), pipeline_mode=pl.Buffered(1))


def _in_proj(x2, g, w, layer, trig):
    m = x2.shape[0]
    return pl.pallas_call(
        _in_proj_kernel,
        grid=(m // PROJ_ROWS,),
        in_specs=[_row_tiles(D_MODEL), _layer_param((1, D_MODEL), layer),
                  _layer_param((D_MODEL, IN_WIDTH), layer), _row_tiles(LANES)],
        out_specs=_row_tiles(IN_WIDTH),
        out_shape=jax.ShapeDtypeStruct((m, IN_WIDTH), BF16),
        compiler_params=pltpu.CompilerParams(vmem_limit_bytes=VMEM_LIMIT),
        name="in_proj",
    )(x2, g, w, trig)


def _out_proj(sb_o, ret_o, mem_o, proj, w, layer, x2):
    m = x2.shape[0]
    return pl.pallas_call(
        _out_proj_kernel,
        grid=(m // OUT_ROWS,),
        in_specs=_mixer_tiles(OUT_ROWS) + [_layer_param((MIX_WIDTH, D_MODEL), layer),
                                           _row_tiles(D_MODEL, OUT_ROWS)],
        out_specs=_row_tiles(D_MODEL, OUT_ROWS),
        out_shape=jax.ShapeDtypeStruct((m, D_MODEL), F32),
        compiler_params=pltpu.CompilerParams(vmem_limit_bytes=VMEM_LIMIT),
        name="out_proj",
    )(sb_o, ret_o, mem_o, proj, w, x2)


def _out_in_proj(sb_o, ret_o, mem_o, proj, w_out, layer, x2, g, w_in, trig):
    m = x2.shape[0]
    return pl.pallas_call(
        _out_in_proj_kernel,
        grid=(m // PROJ_ROWS,),
        in_specs=_mixer_tiles(PROJ_ROWS) + [
                  _layer_param((MIX_WIDTH, D_MODEL), layer), _row_tiles(D_MODEL),
                  _layer_param((1, D_MODEL), layer + 1), _layer_param((D_MODEL, IN_WIDTH), layer + 1),
                  _row_tiles(LANES)],
        out_specs=[_row_tiles(D_MODEL), _row_tiles(IN_WIDTH)],
        out_shape=[jax.ShapeDtypeStruct((m, D_MODEL), F32),
                   jax.ShapeDtypeStruct((m, IN_WIDTH), BF16)],
        compiler_params=pltpu.CompilerParams(vmem_limit_bytes=VMEM_LIMIT),
        name="out_in_proj",
    )(sb_o, ret_o, mem_o, proj, w_out, x2, g, w_in, trig)


SB_LAG = 2
SB_STEP = 4
SB_MASKED = -1e30
SB_DEAD = 160.0
SB_IDLE_CARRY = 1e30


def _sb_schedule(nq):
    zero_slot, junk_slot = nq, nq + 1
    idle = (0, 0, junk_slot, junk_slot, 1)
    items = [idle] * SB_STEP
    waves = []
    for w in range(nq):
        start = len(items)
        for qb in range(nq - 1, w - 1, -1):
            src = zero_slot if w == 0 else qb
            items.append((qb, qb - w, src, qb, 0 if qb > w else 1))
        items += [idle] * (-(len(items) - start) % SB_STEP)
        waves.append((start, (len(items) - start) // SB_STEP))
    return jnp.asarray(items, jnp.int32).T, jnp.asarray(waves, jnp.int32).T, waves[0][1]


def _sb_kernel(tab_ref, wave_ref, q_ref, k_ref, v_ref, o_ref,
               tri_ref, bias_ref, z_ref, sp_ref, w_ref, r_ref, m_ref, acc_ref, *, n_waves,
               first_wave_steps):
    t = SB_BLOCK
    first, second = _head_masks()
    n_carry = r_ref.shape[0]

    @pl.when(jnp.logical_and(pl.program_id(0) == 0, pl.program_id(1) == 0))
    def _constants():
        row = lax.broadcasted_iota(jnp.int32, (t, t), 0)
        col = lax.broadcasted_iota(jnp.int32, (t, t), 1)
        tri_ref[...] = (row >= col).astype(BF16)
        bias_ref[...] = jnp.where(col < row, 0.0, SB_MASKED)
        r_ref[n_carry - 2] = jnp.zeros(r_ref.shape[1:], F32)

    z_ref[SB_STEP - 1] = jnp.full(z_ref.shape[1:], SB_MASKED, F32)
    sp_ref[SB_STEP - 1] = jnp.zeros(sp_ref.shape[1:], BF16)
    for slot in range(SB_STEP - SB_LAG, SB_STEP):
        w_ref[slot] = jnp.zeros(w_ref.shape[1:], BF16)
    r_ref[n_carry - 1] = jnp.full(r_ref.shape[1:], SB_IDLE_CARRY, F32)
    m_ref[1] = jnp.zeros(m_ref.shape[1:], F32)
    acc_ref[...] = jnp.zeros_like(acc_ref)

    def rows(blk):
        return pl.ds(pl.multiple_of(blk * t, t), t)

    def softplus2(z):
        return jnp.maximum(z, 0.0) + jnp.log2(1.0 + jnp.exp2(-jnp.abs(z)))

    half = t // 2

    def lower_rows(a):
        return [a[r0:r0 + half] for r0 in (half, t + half)]

    def zero_upper_rows(lower):
        zeros = jnp.zeros((half, half), lower[0].dtype)
        return jnp.concatenate([zeros, lower[0], zeros, lower[1]], axis=0)

    def scores(j, slot, diagonal):
        qb, kb = tab_ref[0, j], tab_ref[1, j]
        q = q_ref[rows(qb), :]
        q2 = jnp.concatenate([jnp.where(first, q, 0), jnp.where(second, q, 0)], axis=0)
        z = _dot_nt(q2, k_ref[rows(kb), :])
        if diagonal:
            bias = bias_ref[...]
            z = z + jnp.concatenate([bias, bias], axis=0)
            sp = jnp.concatenate(
                [softplus2(z[:, :half]),
                 zero_upper_rows([softplus2(p) for p in lower_rows(z[:, half:])])], axis=1)
        else:
            sp = softplus2(z)
        z_ref[slot] = z
        sp_ref[slot] = sp.astype(BF16)

    def weights(j, slot, track_minimum, diagonal):
        cum = _dot(sp_ref[slot], tri_ref[...])
        r = r_ref[tab_ref[2, j]]
        z = z_ref[slot]
        if diagonal:
            pieces = zip(lower_rows(z[:, half:]), lower_rows(cum[:, half:]), lower_rows(r))
            halves = [jnp.exp2(z[:, :half] - cum[:, :half] - r),
                      zero_upper_rows([jnp.exp2(a - b - c) for a, b, c in pieces])]
        else:
            halves = [jnp.exp2(z[:, c:c + LANES] - cum[:, c:c + LANES] - r) for c in (0, LANES)]
        w_ref[slot] = jnp.concatenate(halves, axis=1).astype(BF16)
        r_new = r + cum[:, 0:1]
        r_ref[tab_ref[3, j]] = r_new
        if track_minimum:
            m_slot = tab_ref[4, j]
            m_ref[m_slot] = jnp.minimum(m_ref[m_slot], r_new)

    def values(j, slot):
        qb, kb = tab_ref[0, j], tab_ref[1, j]
        pv = _dot(w_ref[slot], v_ref[rows(kb), :])
        acc_ref[rows(qb), :] += jnp.where(first, pv[:t], pv[t:])

    def run_wave(start, n_steps, diagonal):
        def step(n, _):
            for i in range(SB_STEP):
                c = start + SB_STEP * n + i
                values(c - 2, (i - 2) % SB_STEP)
                weights(c - 1, (i - 1) % SB_STEP, not diagonal, diagonal)
                scores(c, i, diagonal)
            return 0

        lax.fori_loop(0, n_steps, step, 0)
        return start + SB_STEP * n_steps

    def later_wave(state):
        w = state[0]
        m_ref[0] = jnp.full(m_ref.shape[1:], jnp.inf, F32)
        end = run_wave(wave_ref[0, w], wave_ref[1, w], False)
        return w + 1, end, jnp.min(m_ref[0])

    end = run_wave(SB_STEP, first_wave_steps, True)
    _, end, _ = lax.while_loop(lambda s: jnp.logical_and(s[0] < n_waves, s[2] < SB_DEAD),
                               later_wave, (jnp.int32(1), jnp.int32(end), jnp.float32(0.0)))
    values(end - 2, SB_STEP - 2)
    weights(end - 1, SB_STEP - 1, False, False)
    values(end - 1, SB_STEP - 1)
    o_ref[...] = acc_ref[...].astype(BF16)


def _sb_attention(proj, b, s):
    nq = s // SB_BLOCK
    assert SB_BLOCK == 2 * LANES
    table, waves, first_wave_steps = _sb_schedule(nq)
    t = SB_BLOCK
    seq = lambda col: pl.BlockSpec((s, LANES), lambda bi, p, tab, wav: (bi, col + p))
    return pl.pallas_call(
        functools.partial(_sb_kernel, n_waves=nq, first_wave_steps=first_wave_steps),
        grid_spec=pltpu.PrefetchScalarGridSpec(
            num_scalar_prefetch=2,
            grid=(b, SB_PAIRS),
            in_specs=[seq(COL_SB_Q), seq(COL_SB_K), seq(COL_SB_V)],
            out_specs=pl.BlockSpec((s, LANES), lambda bi, p, tab, wav: (bi, p)),
            scratch_shapes=[pltpu.VMEM((t, t), BF16),
                            pltpu.VMEM((t, t), F32),
                            pltpu.VMEM((SB_STEP, 2 * t, t), F32),
                            pltpu.VMEM((SB_STEP, 2 * t, t), BF16),
                            pltpu.VMEM((SB_STEP, 2 * t, t), BF16),
                            pltpu.VMEM((nq + 2, 2 * t, LANES), F32),
                            pltpu.VMEM((2, 2 * t, LANES), F32),
                            pltpu.VMEM((s, LANES), F32)]),
        out_shape=jax.ShapeDtypeStruct((b * s, SB_WIDTH), BF16),
        compiler_params=pltpu.CompilerParams(
            dimension_semantics=("arbitrary", "arbitrary"),
            vmem_limit_bytes=VMEM_LIMIT),
        name="sb_attention",
    )(table, waves, proj, proj, proj)


def _ret_kernel(q_ref, k_ref, v_ref, g_ref, ng_ref, lg_ref, o_ref, kv_ref, state_ref):
    c = RET_CHUNK
    first, second = _head_masks()
    nchunks = q_ref.shape[0] // c

    lg_lane = lg_ref[0]
    lg_a = lg_lane[:, 0:1]
    lg_b = lg_lane[:, HEAD_DIM:HEAD_DIM + 1]

    ri = lax.broadcasted_iota(jnp.int32, (c, c), 0)
    ci = lax.broadcasted_iota(jnp.int32, (c, c), 1)
    diff = (ri - ci).astype(F32)
    keep = ri >= ci
    decay_a = jnp.where(keep, jnp.exp(lg_a * jnp.maximum(diff, 0.0)), 0.0)
    decay_b = jnp.where(keep, jnp.exp(lg_b * jnp.maximum(diff, 0.0)), 0.0)
    idx = lax.broadcasted_iota(jnp.int32, (c, LANES), 0).astype(F32)
    q_decay = jnp.exp(lg_lane * (idx + 1.0))
    k_decay = jnp.exp(lg_lane * (c - 1.0 - idx))
    si = lax.broadcasted_iota(jnp.int32, (LANES, LANES), 0)
    sj = lax.broadcasted_iota(jnp.int32, (LANES, LANES), 1)
    same_head = (si < HEAD_DIM) == (sj < HEAD_DIM)
    state_decay = jnp.where(same_head, jnp.exp(lg_lane * float(c)), 0.0)
    norm_g = ng_ref[0]

    chunk_rows = [slice(n * c, (n + 1) * c) for n in range(nchunks)]

    for n, rows in enumerate(chunk_rows):
        kd = (k_ref[rows, :].astype(F32) * k_decay).astype(BF16)
        kv_ref[n] = jnp.where(same_head, _dot_tn(kd, v_ref[rows, :]), 0.0)

    state = jnp.zeros((LANES, LANES), F32)
    for n in range(nchunks):
        state_ref[n] = state.astype(BF16)
        state = state * state_decay + kv_ref[n]

    decay2 = jnp.concatenate([decay_a, decay_b], axis=1)
    for n, rows in enumerate(chunk_rows):
        qb = q_ref[rows, :]
        kb = k_ref[rows, :]
        v = v_ref[rows, :]
        k2 = jnp.concatenate([jnp.where(first, kb, 0), jnp.where(second, kb, 0)], axis=0)
        v2 = jnp.concatenate([jnp.where(first, v, 0), jnp.where(second, v, 0)], axis=0)
        scores = (_dot_nt(qb, k2) * decay2).astype(BF16)
        out = _dot(scores, v2) + _dot(qb, state_ref[n]) * q_decay
        inv = lax.rsqrt(_pair_mean_square(out, first) + EPS)
        y = out * inv * norm_g
        o_ref[rows, :] = (y * _silu(g_ref[rows, :].astype(F32))).astype(BF16)


def _retention(proj, ret_norm_g, layer, b, s):
    seq = lambda col: pl.BlockSpec((s, LANES), lambda bi, p: (bi, col + p))
    gamma = 1.0 - 2.0 ** (-5.0 - jnp.arange(RET_HEADS, dtype=F32))
    log_gamma = jnp.repeat(jnp.log(gamma), HEAD_DIM).reshape(RET_PAIRS, 1, LANES)
    return pl.pallas_call(
        _ret_kernel,
        grid=(b, RET_PAIRS),
        in_specs=[seq(COL_RET_Q), seq(COL_RET_K), seq(COL_RET_V), seq(COL_RET_G),
                  pl.BlockSpec((1, 1, LANES), lambda bi, p: (layer, 0, p)),
                  pl.BlockSpec((1, 1, LANES), lambda bi, p: (p, 0, 0))],
        out_specs=pl.BlockSpec((s, LANES), lambda bi, p: (bi, p)),
        out_shape=jax.ShapeDtypeStruct((b * s, RET_WIDTH), BF16),
        scratch_shapes=[pltpu.VMEM((s // RET_CHUNK, LANES, LANES), F32),
                        pltpu.VMEM((s // RET_CHUNK, LANES, LANES), BF16)],
        compiler_params=pltpu.CompilerParams(vmem_limit_bytes=VMEM_LIMIT),
        name="retention",
    )(proj, proj, proj, proj, ret_norm_g, log_gamma)


def _mem_kv_kernel(mem_ref, g_ref, w_ref, kg_ref, k_ref, v_ref):
    first, _ = _head_masks()
    x = mem_ref[...]
    ms = jnp.mean(x * x, axis=-1, keepdims=True)
    h = (x * lax.rsqrt(ms + EPS) * g_ref[0]).astype(BF16)
    kv = _dot(h, w_ref[0].astype(BF16))
    kg = kg_ref[0]
    for pair in range(MEM_PAIRS):
        cols = slice(pair * LANES, (pair + 1) * LANES)
        kp = kv[:, cols]
        kn = kp * lax.rsqrt(_pair_mean_square(kp, first) + EPS) * kg
        k_ref[0, :, cols] = kn.astype(BF16)
    v_ref[0] = kv[:, MEM_WIDTH:].astype(BF16)


def _mem_kv(mem2, mem_norm_g, w_mem_kv, k_norm_g):
    depth = w_mem_kv.shape[0]
    rows = mem2.shape[0]
    out = jax.ShapeDtypeStruct((depth, rows, MEM_WIDTH), BF16)
    return pl.pallas_call(
        _mem_kv_kernel,
        grid=(depth,),
        in_specs=[pl.BlockSpec((rows, D_MODEL), lambda l: (0, 0)),
                  pl.BlockSpec((1, 1, D_MODEL), lambda l: (l, 0, 0)),
                  pl.BlockSpec((1, D_MODEL, 2 * MEM_WIDTH), lambda l: (l, 0, 0)),
                  pl.BlockSpec((1, 1, LANES), lambda l: (l, 0, 0))],
        out_specs=[pl.BlockSpec((1, rows, MEM_WIDTH), lambda l: (l, 0, 0)),
                   pl.BlockSpec((1, rows, MEM_WIDTH), lambda l: (l, 0, 0))],
        out_shape=[out, out],
        compiler_params=pltpu.CompilerParams(vmem_limit_bytes=VMEM_LIMIT),
        name="mem_kv",
    )(mem2, mem_norm_g, w_mem_kv, k_norm_g)


def _mem_attn_kernel(q_ref, g_ref, k_ref, v_ref, qg_ref, o_ref, e_ref, den_ref):
    first, second = _head_masks()
    tile = MEM_ROWS
    n_tiles = q_ref.shape[0] // tile
    k = k_ref[0]
    v = v_ref[0]

    def probabilities(i, slot):
        rows = slice(i * tile, (i + 1) * tile)
        q = q_ref[rows, :].astype(F32)
        inv = lax.rsqrt(_pair_mean_square(q, first) + EPS) * QK_SCALE
        qn = (q * inv * qg_ref[0]).astype(BF16)
        q2 = jnp.concatenate([jnp.where(first, qn, 0), jnp.where(second, qn, 0)], axis=0)
        sc = _dot_nt(q2, k)
        e = jnp.exp(sc - jnp.max(sc, axis=-1, keepdims=True))
        den_ref[slot] = jnp.broadcast_to(jnp.sum(e, axis=-1, keepdims=True), (2 * tile, LANES))
        e_ref[slot] = e.astype(BF16)

    def output(i, slot):
        rows = slice(i * tile, (i + 1) * tile)
        pv = _dot(e_ref[slot], v) / den_ref[slot]
        out = jnp.where(first, pv[:tile], pv[tile:])
        o_ref[rows, :] = (out * _silu(g_ref[rows, :].astype(F32))).astype(BF16)

    for i in range(n_tiles + 1):
        if i >= 1:
            output(i - 1, (i - 1) % 2)
        if i < n_tiles:
            probabilities(i, i % 2)


def _mem_attention(proj, mk, mv, q_norm_g, layer, b, s, tokens):
    seq = lambda col: pl.BlockSpec((s, LANES), lambda bi, p: (bi, col + p))
    kv_spec = pl.BlockSpec((1, tokens, LANES), lambda bi, p: (layer, bi, p))
    return pl.pallas_call(
        _mem_attn_kernel,
        grid=(b, MEM_PAIRS),
        in_specs=[seq(COL_MEM_Q), seq(COL_MEM_G), kv_spec, kv_spec,
                  pl.BlockSpec((1, 1, LANES), lambda bi, p: (layer, 0, 0))],
        out_specs=pl.BlockSpec((s, LANES), lambda bi, p: (bi, p)),
        out_shape=jax.ShapeDtypeStruct((b * s, MEM_WIDTH), BF16),
        scratch_shapes=[pltpu.VMEM((2, 2 * MEM_ROWS, tokens), BF16),
                        pltpu.VMEM((2, 2 * MEM_ROWS, LANES), F32)],
        compiler_params=pltpu.CompilerParams(vmem_limit_bytes=VMEM_LIMIT),
        name="mem_attention",
    )(proj, proj, mk, mv, q_norm_g)


def kernel(x, mem, positions, norm_g, w_in, w_out, mem_norm_g, w_mem_kv, mem_q_norm_g,
           mem_k_norm_g, ret_norm_g):
    b, s, d = x.shape
    tokens = mem.shape[1]
    depth = w_in.shape[0]
    assert d == D_MODEL and w_in.shape[2] == IN_WIDTH
    assert s % SB_BLOCK == 0 and s % RET_CHUNK == 0 and s % MEM_ROWS == 0
    assert (b * s) % PROJ_ROWS == 0 and (b * s) % OUT_ROWS == 0

    norm_g3 = norm_g[:, None, :]
    ret_norm_g3 = ret_norm_g[:, None, :]
    q_norm_g3 = jnp.tile(mem_q_norm_g, (1, PAIR))[:, None, :]
    k_norm_g3 = jnp.tile(mem_k_norm_g, (1, PAIR))[:, None, :]

    trig = _rope_tables(positions)
    mk_all, mv_all = _mem_kv(mem.reshape(b * tokens, d), mem_norm_g[:, None, :], w_mem_kv, k_norm_g3)

    x2 = x.reshape(b * s, d)
    proj = _in_proj(x2, norm_g3, w_in, 0, trig)
    for l in range(depth):
        sb_o = _sb_attention(proj, b, s)
        ret_o = _retention(proj, ret_norm_g3, l, b, s)
        mem_o = _mem_attention(proj, mk_all, mv_all, q_norm_g3, l, b, s, tokens)
        if l + 1 < depth:
            x2, proj = _out_in_proj(sb_o, ret_o, mem_o, proj, w_out, l, x2, norm_g3, w_in, trig)
        else:
            x2 = _out_proj(sb_o, ret_o, mem_o, proj, w_out, l, x2)
    return x2.reshape(b, s, d)
```

```python
import functools
import math

import jax
import jax.numpy as jnp
from jax import lax
from jax.experimental import pallas as pl
from jax.experimental.pallas import tpu as pltpu

D_MODEL = 1024
HEAD_DIM = 64
SB_HEADS = 6
RET_HEADS = 6
MEM_HEADS = 4
SB_WIDTH = SB_HEADS * HEAD_DIM
RET_WIDTH = RET_HEADS * HEAD_DIM
MEM_WIDTH = MEM_HEADS * HEAD_DIM
MIX_WIDTH = SB_WIDTH + RET_WIDTH + MEM_WIDTH
IN_WIDTH = 4 * SB_WIDTH + 4 * RET_WIDTH + 2 * MEM_WIDTH
ROPE_BASE = 10000.0
EPS = 1e-6
QK_SCALE = HEAD_DIM ** -0.5
SB_Q_SCALE = QK_SCALE * math.log2(math.e)

LANES = 128
PAIR = LANES // HEAD_DIM
SB_PAIRS = SB_HEADS // PAIR
RET_PAIRS = RET_HEADS // PAIR
MEM_PAIRS = MEM_HEADS // PAIR

COL_SB_Q = 0
COL_SB_K = COL_SB_Q + SB_PAIRS
COL_SB_V = COL_SB_K + SB_PAIRS
COL_SB_G = COL_SB_V + SB_PAIRS
COL_RET_Q = COL_SB_G + SB_PAIRS
COL_RET_K = COL_RET_Q + RET_PAIRS
COL_RET_V = COL_RET_K + RET_PAIRS
COL_RET_G = COL_RET_V + RET_PAIRS
COL_MEM_Q = COL_RET_G + RET_PAIRS
COL_MEM_G = COL_MEM_Q + MEM_PAIRS

PROJ_ROWS = 512
OUT_ROWS = 1024
PROJ_COLS = 512
SB_BLOCK = 256
RET_CHUNK = 128
MEM_ROWS = 512
V7X_VMEM_BYTES = 64 * 1024 * 1024
VMEM_LIMIT = V7X_VMEM_BYTES * 3 // 4

F32 = jnp.float32
BF16 = jnp.bfloat16


def _dot(a, b):
    return jnp.dot(a, b, preferred_element_type=F32)


def _dot_nt(a, b):
    return lax.dot_general(a, b, (((1,), (1,)), ((), ())), preferred_element_type=F32)


def _dot_tn(a, b):
    return lax.dot_general(a, b, (((0,), (0,)), ((), ())), preferred_element_type=F32)


def _silu(g):
    return g / (1.0 + jnp.exp(-g))


def _head_masks():
    lane = lax.broadcasted_iota(jnp.int32, (1, LANES), 1)
    first = lane < HEAD_DIM
    return first, jnp.logical_not(first)


def _pair_mean_square(t, first):
    sq = t * t
    s_a = jnp.sum(jnp.where(first, sq, 0.0), axis=-1, keepdims=True)
    s_b = jnp.sum(jnp.where(first, 0.0, sq), axis=-1, keepdims=True)
    return jnp.where(first, s_a, s_b) * (1.0 / HEAD_DIM)


def _rope_kernel(pos_ref, invf_ref, trig_ref):
    ang = pos_ref[...].astype(F32) * invf_ref[...]
    lane = lax.broadcasted_iota(jnp.int32, (1, LANES), 1)
    phase = jnp.where((lane % HEAD_DIM) < HEAD_DIM // 2, 0.0, math.pi / 2)
    trig_ref[...] = jnp.cos(ang - phase)


def _rope_tables(positions):
    b, s = positions.shape
    half = HEAD_DIM // 2
    inv_freq = ROPE_BASE ** (-jnp.arange(half, dtype=F32) / half)
    invf = jnp.tile(inv_freq, LANES // half)[None, :]
    pos = positions.reshape(b * s, 1)
    rows = s
    return pl.pallas_call(
        _rope_kernel,
        grid=(b * s // rows,),
        in_specs=[pl.BlockSpec((rows, 1), lambda i: (i, 0)),
                  pl.BlockSpec((1, LANES), lambda i: (0, 0))],
        out_specs=pl.BlockSpec((rows, LANES), lambda i: (i, 0)),
        out_shape=jax.ShapeDtypeStruct((b * s, LANES), F32),
        name="rope_tables",
    )(pos, invf)


def _half_lanes(t):
    lane = lax.broadcasted_iota(jnp.int32, (1, LANES), 1)
    half = HEAD_DIM // 2
    return (lane % HEAD_DIM) < half, pltpu.roll(t, LANES - half, 1), pltpu.roll(t, half, 1)


def _swap_halves(t):
    lower, from_upper, from_lower = _half_lanes(t)
    return jnp.where(lower, from_upper, from_lower)


def _norm_project(x, g_ref, w_ref, trig_ref, o_ref):
    ms = jnp.mean(x * x, axis=-1, keepdims=True)
    h = (x * lax.rsqrt(ms + EPS) * g_ref[0]).astype(BF16)
    trig = trig_ref[...]
    lower, from_upper, from_lower = _half_lanes(trig)
    cos = jnp.where(lower, trig, from_lower)
    sin = jnp.where(lower, -from_upper, trig)
    blocks_per_dot = PROJ_COLS // LANES
    for j in range(IN_WIDTH // PROJ_COLS):
        y = _dot(h, w_ref[0, :, j * PROJ_COLS:(j + 1) * PROJ_COLS].astype(BF16))
        for b in range(blocks_per_dot):
            blk = j * blocks_per_dot + b
            piece = y[:, b * LANES:(b + 1) * LANES]
            if blk < COL_SB_K:
                piece = piece * SB_Q_SCALE
            elif COL_RET_Q <= blk < COL_RET_V:
                piece = piece * cos + _swap_halves(piece) * sin
                if blk >= COL_RET_K:
                    piece = piece * QK_SCALE
            o_ref[:, blk * LANES:(blk + 1) * LANES] = piece.astype(BF16)


def _in_proj_kernel(x_ref, g_ref, w_ref, trig_ref, o_ref):
    _norm_project(x_ref[...], g_ref, w_ref, trig_ref, o_ref)


def _mix(sb_ref, ret_ref, mem_ref, sb_g_ref, w_ref, x_ref):
    sb = (sb_ref[...].astype(F32) * _silu(sb_g_ref[...].astype(F32))).astype(BF16)
    mixed = jnp.concatenate([sb, ret_ref[...], mem_ref[...]], axis=1)
    return x_ref[...] + _dot(mixed, w_ref[0].astype(BF16))


def _out_proj_kernel(sb_ref, ret_ref, mem_ref, sb_g_ref, w_ref, x_ref, o_ref):
    o_ref[...] = _mix(sb_ref, ret_ref, mem_ref, sb_g_ref, w_ref, x_ref)


def _out_in_proj_kernel(sb_ref, ret_ref, mem_ref, sb_g_ref, w_out_ref, x_ref, g_ref, w_in_ref,
                        trig_ref, x_out_ref, proj_ref):
    x_new = _mix(sb_ref, ret_ref, mem_ref, sb_g_ref, w_out_ref, x_ref)
    x_out_ref[...] = x_new
    _norm_project(x_new, g_ref, w_in_ref, trig_ref, proj_ref)


def _row_tiles(width, rows=PROJ_ROWS, col=0):
    return pl.BlockSpec((rows, width), lambda i: (i, col))


def _mixer_tiles(rows):
    assert COL_SB_G * LANES % SB_WIDTH == 0
    return [_row_tiles(SB_WIDTH, rows), _row_tiles(RET_WIDTH, rows), _row_tiles(MEM_WIDTH, rows),
            _row_tiles(SB_WIDTH, rows, COL_SB_G * LANES // SB_WIDTH)]


def _layer_param(shape, layer):
    return pl.BlockSpec((1,) + shape, lambda i: (layer, 0, 0), pipeline_mode=pl.Buffered(1))


def _in_proj(x2, g, w, layer, trig):
    m = x2.shape[0]
    return pl.pallas_call(
        _in_proj_kernel,
        grid=(m // PROJ_ROWS,),
        in_specs=[_row_tiles(D_MODEL), _layer_param((1, D_MODEL), layer),
                  _layer_param((D_MODEL, IN_WIDTH), layer), _row_tiles(LANES)],
        out_specs=_row_tiles(IN_WIDTH),
        out_shape=jax.ShapeDtypeStruct((m, IN_WIDTH), BF16),
        compiler_params=pltpu.CompilerParams(vmem_limit_bytes=VMEM_LIMIT),
        name="in_proj",
    )(x2, g, w, trig)


def _out_proj(sb_o, ret_o, mem_o, proj, w, layer, x2):
    m = x2.shape[0]
    return pl.pallas_call(
        _out_proj_kernel,
        grid=(m // OUT_ROWS,),
        in_specs=_mixer_tiles(OUT_ROWS) + [_layer_param((MIX_WIDTH, D_MODEL), layer),
                                           _row_tiles(D_MODEL, OUT_ROWS)],
        out_specs=_row_tiles(D_MODEL, OUT_ROWS),
        out_shape=jax.ShapeDtypeStruct((m, D_MODEL), F32),
        compiler_params=pltpu.CompilerParams(vmem_limit_bytes=VMEM_LIMIT),
        name="out_proj",
    )(sb_o, ret_o, mem_o, proj, w, x2)


def _out_in_proj(sb_o, ret_o, mem_o, proj, w_out, layer, x2, g, w_in, trig):
    m = x2.shape[0]
    return pl.pallas_call(
        _out_in_proj_kernel,
        grid=(m // PROJ_ROWS,),
        in_specs=_mixer_tiles(PROJ_ROWS) + [
                  _layer_param((MIX_WIDTH, D_MODEL), layer), _row_tiles(D_MODEL),
                  _layer_param((1, D_MODEL), layer + 1), _layer_param((D_MODEL, IN_WIDTH), layer + 1),
                  _row_tiles(LANES)],
        out_specs=[_row_tiles(D_MODEL), _row_tiles(IN_WIDTH)],
        out_shape=[jax.ShapeDtypeStruct((m, D_MODEL), F32),
                   jax.ShapeDtypeStruct((m, IN_WIDTH), BF16)],
        compiler_params=pltpu.CompilerParams(vmem_limit_bytes=VMEM_LIMIT),
        name="out_in_proj",
    )(sb_o, ret_o, mem_o, proj, w_out, x2, g, w_in, trig)


SB_LAG = 2
SB_STEP = 4
SB_MASKED = -1e30
SB_DEAD = 160.0
SB_IDLE_CARRY = 1e30


def _sb_schedule(nq):
    zero_slot, junk_slot = nq, nq + 1
    idle = (0, 0, junk_slot, junk_slot, 1, 1)
    items = [idle] * SB_STEP
    waves = []
    for w in range(nq):
        start = len(items)
        for qb in range(nq - 1, w - 1, -1):
            src = zero_slot if w == 0 else qb
            items.append((qb, qb - w, src, qb, 0 if qb > w else 1, 0))
        items += [idle] * (-(len(items) - start) % SB_STEP)
        waves.append((start, (len(items) - start) // SB_STEP))
    return jnp.asarray(items, jnp.int32).T, jnp.asarray(waves, jnp.int32).T, waves[0][1]


def _sb_kernel(tab_ref, wave_ref, q_ref, k_ref, v_ref, o_ref,
               tri_ref, bias_ref, z_ref, sp_ref, w_ref, r_ref, m_ref, acc_ref, *, n_waves,
               first_wave_steps):
    t = SB_BLOCK
    first, second = _head_masks()
    n_carry = r_ref.shape[0]

    @pl.when(jnp.logical_and(pl.program_id(0) == 0, pl.program_id(1) == 0))
    def _constants():
        row = lax.broadcasted_iota(jnp.int32, (t, t), 0)
        col = lax.broadcasted_iota(jnp.int32, (t, t), 1)
        tri_ref[...] = (row >= col).astype(BF16)
        bias_ref[...] = jnp.where(col < row, 0.0, SB_MASKED)
        r_ref[n_carry - 2] = jnp.zeros(r_ref.shape[1:], F32)

    z_ref[SB_STEP - 1] = jnp.full(z_ref.shape[1:], SB_MASKED, F32)
    sp_ref[SB_STEP - 1] = jnp.zeros(sp_ref.shape[1:], BF16)
    for slot in range(SB_STEP - SB_LAG, SB_STEP):
        w_ref[slot] = jnp.zeros(w_ref.shape[1:], BF16)
    r_ref[n_carry - 1] = jnp.full(r_ref.shape[1:], SB_IDLE_CARRY, F32)
    m_ref[1] = jnp.zeros(m_ref.shape[1:], F32)
    acc_ref[...] = jnp.zeros_like(acc_ref)

    def rows(blk):
        return pl.ds(pl.multiple_of(blk * t, t), t)

    def softplus2(z):
        return jnp.maximum(z, 0.0) + jnp.log2(1.0 + jnp.exp2(-jnp.abs(z)))

    half = t // 2

    def lower_rows(a):
        return [a[r0:r0 + half] for r0 in (half, t + half)]

    def zero_upper_rows(lower):
        zeros = jnp.zeros((half, half), lower[0].dtype)
        return jnp.concatenate([zeros, lower[0], zeros, lower[1]], axis=0)

    def scores(j, slot, diagonal):
        qb, kb = tab_ref[0, j], tab_ref[1, j]
        q = q_ref[rows(qb), :]
        q2 = jnp.concatenate([jnp.where(first, q, 0), jnp.where(second, q, 0)], axis=0)
        z = _dot_nt(q2, k_ref[rows(kb), :])
        if diagonal:
            bias = bias_ref[...]
            z = z + jnp.concatenate([bias, bias], axis=0)
            sp = jnp.concatenate(
                [softplus2(z[:, :half]),
                 zero_upper_rows([softplus2(p) for p in lower_rows(z[:, half:])])], axis=1)
        else:
            sp = softplus2(z)
        z_ref[slot] = z
        sp_ref[slot] = sp.astype(BF16)

    def weights(j, slot, track_minimum, diagonal):
        cum = _dot(sp_ref[slot], tri_ref[...])
        r = r_ref[tab_ref[2, j]]
        z = z_ref[slot]
        if diagonal:
            pieces = zip(lower_rows(z[:, half:]), lower_rows(cum[:, half:]), lower_rows(r))
            halves = [jnp.exp2(z[:, :half] - cum[:, :half] - r),
                      zero_upper_rows([jnp.exp2(a - b - c) for a, b, c in pieces])]
        else:
            halves = [jnp.exp2(z[:, c:c + LANES] - cum[:, c:c + LANES] - r) for c in (0, LANES)]
        w_ref[slot] = jnp.concatenate(halves, axis=1).astype(BF16)
        r_new = r + cum[:, 0:1]
        r_ref[tab_ref[3, j]] = r_new
        if track_minimum:
            m_slot = tab_ref[4, j]
            m_ref[m_slot] = jnp.minimum(m_ref[m_slot], r_new)

    def values(j, slot):
        qb, kb = tab_ref[0, j], tab_ref[1, j]
        pv = _dot(w_ref[slot], v_ref[rows(kb), :])
        acc_ref[rows(qb), :] += jnp.where(first, pv[:t], pv[t:])

    def run_wave(start, n_steps, diagonal):
        def step(n, _):
            for i in range(SB_STEP):
                c = start + SB_STEP * n + i
                values(c - 2, (i - 2) % SB_STEP)
                weights(c - 1, (i - 1) % SB_STEP, not diagonal, diagonal)
                scores(c, i, diagonal)
            return 0

        lax.fori_loop(0, n_steps, step, 0)
        return start + SB_STEP * n_steps

    def later_wave(state):
        w = state[0]
        m_ref[0] = jnp.full(m_ref.shape[1:], jnp.inf, F32)
        end = run_wave(wave_ref[0, w], wave_ref[1, w], False)
        return w + 1, end, jnp.min(m_ref[0])

    end = run_wave(SB_STEP, first_wave_steps, True)
    _, end, _ = lax.while_loop(lambda s: jnp.logical_and(s[0] < n_waves, s[2] < SB_DEAD),
                               later_wave, (jnp.int32(1), jnp.int32(end), jnp.float32(0.0)))
    @pl.when(tab_ref[5, end - 2] == 0)
    def _():
        values(end - 2, SB_STEP - 2)

    @pl.when(tab_ref[5, end - 1] == 0)
    def _():
        weights(end - 1, SB_STEP - 1, False, False)
        values(end - 1, SB_STEP - 1)

    o_ref[...] = acc_ref[...].astype(BF16)


def _sb_attention(proj, b, s):
    nq = s // SB_BLOCK
    assert SB_BLOCK == 2 * LANES
    table, waves, first_wave_steps = _sb_schedule(nq)
    t = SB_BLOCK
    seq = lambda col: pl.BlockSpec((s, LANES), lambda bi, p, tab, wav: (bi, col + p))
    return pl.pallas_call(
        functools.partial(_sb_kernel, n_waves=nq, first_wave_steps=first_wave_steps),
        grid_spec=pltpu.PrefetchScalarGridSpec(
            num_scalar_prefetch=2,
            grid=(b, SB_PAIRS),
            in_specs=[seq(COL_SB_Q), seq(COL_SB_K), seq(COL_SB_V)],
            out_specs=pl.BlockSpec((s, LANES), lambda bi, p, tab, wav: (bi, p)),
            scratch_shapes=[pltpu.VMEM((t, t), BF16),
                            pltpu.VMEM((t, t), F32),
                            pltpu.VMEM((SB_STEP, 2 * t, t), F32),
                            pltpu.VMEM((SB_STEP, 2 * t, t), BF16),
                            pltpu.VMEM((SB_STEP, 2 * t, t), BF16),
                            pltpu.VMEM((nq + 2, 2 * t, LANES), F32),
                            pltpu.VMEM((2, 2 * t, LANES), F32),
                            pltpu.VMEM((s, LANES), F32)]),
        out_shape=jax.ShapeDtypeStruct((b * s, SB_WIDTH), BF16),
        compiler_params=pltpu.CompilerParams(
            dimension_semantics=("arbitrary", "arbitrary"),
            vmem_limit_bytes=VMEM_LIMIT),
        name="sb_attention",
    )(table, waves, proj, proj, proj)


def _ret_kernel(q_ref, k_ref, v_ref, g_ref, ng_ref, lg_ref, o_ref, kv_ref, state_ref):
    c = RET_CHUNK
    first, second = _head_masks()
    nchunks = q_ref.shape[0] // c

    lg_lane = lg_ref[0]
    lg_a = lg_lane[:, 0:1]
    lg_b = lg_lane[:, HEAD_DIM:HEAD_DIM + 1]

    ri = lax.broadcasted_iota(jnp.int32, (c, c), 0)
    ci = lax.broadcasted_iota(jnp.int32, (c, c), 1)
    diff = (ri - ci).astype(F32)
    keep = ri >= ci
    decay_a = jnp.where(keep, jnp.exp(lg_a * jnp.maximum(diff, 0.0)), 0.0)
    decay_b = jnp.where(keep, jnp.exp(lg_b * jnp.maximum(diff, 0.0)), 0.0)
    idx = lax.broadcasted_iota(jnp.int32, (c, LANES), 0).astype(F32)
    q_decay = jnp.exp(lg_lane * (idx + 1.0))
    k_decay = jnp.exp(lg_lane * (c - 1.0 - idx))
    si = lax.broadcasted_iota(jnp.int32, (LANES, LANES), 0)
    sj = lax.broadcasted_iota(jnp.int32, (LANES, LANES), 1)
    same_head = (si < HEAD_DIM) == (sj < HEAD_DIM)
    state_decay = jnp.where(same_head, jnp.exp(lg_lane * float(c)), 0.0)
    norm_g = ng_ref[0]

    chunk_rows = [slice(n * c, (n + 1) * c) for n in range(nchunks)]

    for n, rows in enumerate(chunk_rows):
        kd = (k_ref[rows, :].astype(F32) * k_decay).astype(BF16)
        kv_ref[n] = jnp.where(same_head, _dot_tn(kd, v_ref[rows, :]), 0.0)

    state = jnp.zeros((LANES, LANES), F32)
    for n in range(nchunks):
        state_ref[n] = state.astype(BF16)
        state = state * state_decay + kv_ref[n]

    decay2 = jnp.concatenate([decay_a, decay_b], axis=1)
    for n, rows in enumerate(chunk_rows):
        qb = q_ref[rows, :]
        kb = k_ref[rows, :]
        v = v_ref[rows, :]
        k2 = jnp.concatenate([jnp.where(first, kb, 0), jnp.where(second, kb, 0)], axis=0)
        v2 = jnp.concatenate([jnp.where(first, v, 0), jnp.where(second, v, 0)], axis=0)
        scores = (_dot_nt(qb, k2) * decay2).astype(BF16)
        out = _dot(scores, v2) + _dot(qb, state_ref[n]) * q_decay
        inv = lax.rsqrt(_pair_mean_square(out, first) + EPS)
        y = out * inv * norm_g
        o_ref[rows, :] = (y * _silu(g_ref[rows, :].astype(F32))).astype(BF16)


def _retention(proj, ret_norm_g, layer, b, s):
    seq = lambda col: pl.BlockSpec((s, LANES), lambda bi, p: (bi, col + p))
    gamma = 1.0 - 2.0 ** (-5.0 - jnp.arange(RET_HEADS, dtype=F32))
    log_gamma = jnp.repeat(jnp.log(gamma), HEAD_DIM).reshape(RET_PAIRS, 1, LANES)
    return pl.pallas_call(
        _ret_kernel,
        grid=(b, RET_PAIRS),
        in_specs=[seq(COL_RET_Q), seq(COL_RET_K), seq(COL_RET_V), seq(COL_RET_G),
                  pl.BlockSpec((1, 1, LANES), lambda bi, p: (layer, 0, p)),
                  pl.BlockSpec((1, 1, LANES), lambda bi, p: (p, 0, 0))],
        out_specs=pl.BlockSpec((s, LANES), lambda bi, p: (bi, p)),
        out_shape=jax.ShapeDtypeStruct((b * s, RET_WIDTH), BF16),
        scratch_shapes=[pltpu.VMEM((s // RET_CHUNK, LANES, LANES), F32),
                        pltpu.VMEM((s // RET_CHUNK, LANES, LANES), BF16)],
        compiler_params=pltpu.CompilerParams(vmem_limit_bytes=VMEM_LIMIT),
        name="retention",
    )(proj, proj, proj, proj, ret_norm_g, log_gamma)


def _mem_kv_kernel(mem_ref, g_ref, w_ref, kg_ref, k_ref, v_ref):
    first, _ = _head_masks()
    x = mem_ref[...]
    ms = jnp.mean(x * x, axis=-1, keepdims=True)
    h = (x * lax.rsqrt(ms + EPS) * g_ref[0]).astype(BF16)
    kv = _dot(h, w_ref[0].astype(BF16))
    kg = kg_ref[0]
    for pair in range(MEM_PAIRS):
        cols = slice(pair * LANES, (pair + 1) * LANES)
        kp = kv[:, cols]
        kn = kp * lax.rsqrt(_pair_mean_square(kp, first) + EPS) * kg
        k_ref[0, :, cols] = kn.astype(BF16)
    v_ref[0] = kv[:, MEM_WIDTH:].astype(BF16)


def _mem_kv(mem2, mem_norm_g, w_mem_kv, k_norm_g):
    depth = w_mem_kv.shape[0]
    rows = mem2.shape[0]
    out = jax.ShapeDtypeStruct((depth, rows, MEM_WIDTH), BF16)
    return pl.pallas_call(
        _mem_kv_kernel,
        grid=(depth,),
        in_specs=[pl.BlockSpec((rows, D_MODEL), lambda l: (0, 0)),
                  pl.BlockSpec((1, 1, D_MODEL), lambda l: (l, 0, 0)),
                  pl.BlockSpec((1, D_MODEL, 2 * MEM_WIDTH), lambda l: (l, 0, 0)),
                  pl.BlockSpec((1, 1, LANES), lambda l: (l, 0, 0))],
        out_specs=[pl.BlockSpec((1, rows, MEM_WIDTH), lambda l: (l, 0, 0)),
                   pl.BlockSpec((1, rows, MEM_WIDTH), lambda l: (l, 0, 0))],
        out_shape=[out, out],
        compiler_params=pltpu.CompilerParams(vmem_limit_bytes=VMEM_LIMIT),
        name="mem_kv",
    )(mem2, mem_norm_g, w_mem_kv, k_norm_g)


def _mem_attn_kernel(q_ref, g_ref, k_ref, v_ref, qg_ref, o_ref, e_ref, den_ref):
    first, second = _head_masks()
    tile = MEM_ROWS
    n_tiles = q_ref.shape[0] // tile
    k = k_ref[0]
    v = v_ref[0]

    def probabilities(i, slot):
        rows = slice(i * tile, (i + 1) * tile)
        q = q_ref[rows, :].astype(F32)
        inv = lax.rsqrt(_pair_mean_square(q, first) + EPS) * QK_SCALE
        qn = (q * inv * qg_ref[0]).astype(BF16)
        q2 = jnp.concatenate([jnp.where(first, qn, 0), jnp.where(second, qn, 0)], axis=0)
        sc = _dot_nt(q2, k)
        e = jnp.exp(sc - jnp.max(sc, axis=-1, keepdims=True))
        den_ref[slot] = jnp.broadcast_to(jnp.sum(e, axis=-1, keepdims=True), (2 * tile, LANES))
        e_ref[slot] = e.astype(BF16)

    def output(i, slot):
        rows = slice(i * tile, (i + 1) * tile)
        pv = _dot(e_ref[slot], v) / den_ref[slot]
        out = jnp.where(first, pv[:tile], pv[tile:])
        o_ref[rows, :] = (out * _silu(g_ref[rows, :].astype(F32))).astype(BF16)

    for i in range(n_tiles + 1):
        if i >= 1:
            output(i - 1, (i - 1) % 2)
        if i < n_tiles:
            probabilities(i, i % 2)


def _mem_attention(proj, mk, mv, q_norm_g, layer, b, s, tokens):
    seq = lambda col: pl.BlockSpec((s, LANES), lambda bi, p: (bi, col + p))
    kv_spec = pl.BlockSpec((1, tokens, LANES), lambda bi, p: (layer, bi, p))
    return pl.pallas_call(
        _mem_attn_kernel,
        grid=(b, MEM_PAIRS),
        in_specs=[seq(COL_MEM_Q), seq(COL_MEM_G), kv_spec, kv_spec,
                  pl.BlockSpec((1, 1, LANES), lambda bi, p: (layer, 0, 0))],
        out_specs=pl.BlockSpec((s, LANES), lambda bi, p: (bi, p)),
        out_shape=jax.ShapeDtypeStruct((b * s, MEM_WIDTH), BF16),
        scratch_shapes=[pltpu.VMEM((2, 2 * MEM_ROWS, tokens), BF16),
                        pltpu.VMEM((2, 2 * MEM_ROWS, LANES), F32)],
        compiler_params=pltpu.CompilerParams(vmem_limit_bytes=VMEM_LIMIT),
        name="mem_attention",
    )(proj, proj, mk, mv, q_norm_g)


def kernel(x, mem, positions, norm_g, w_in, w_out, mem_norm_g, w_mem_kv, mem_q_norm_g,
           mem_k_norm_g, ret_norm_g):
    b, s, d = x.shape
    tokens = mem.shape[1]
    depth = w_in.shape[0]
    assert d == D_MODEL and w_in.shape[2] == IN_WIDTH
    assert s % SB_BLOCK == 0 and s % RET_CHUNK == 0 and s % MEM_ROWS == 0
    assert (b * s) % PROJ_ROWS == 0 and (b * s) % OUT_ROWS == 0

    norm_g3 = norm_g[:, None, :]
    ret_norm_g3 = ret_norm_g[:, None, :]
    q_norm_g3 = jnp.tile(mem_q_norm_g, (1, PAIR))[:, None, :]
    k_norm_g3 = jnp.tile(mem_k_norm_g, (1, PAIR))[:, None, :]

    trig = _rope_tables(positions)
    mk_all, mv_all = _mem_kv(mem.reshape(b * tokens, d), mem_norm_g[:, None, :], w_mem_kv, k_norm_g3)

    x2 = x.reshape(b * s, d)
    proj = _in_proj(x2, norm_g3, w_in, 0, trig)
    for l in range(depth):
        sb_o = _sb_attention(proj, b, s)
        ret_o = _retention(proj, ret_norm_g3, l, b, s)
        mem_o = _mem_attention(proj, mk_all, mv_all, q_norm_g3, l, b, s, tokens)
        if l + 1 < depth:
            x2, proj = _out_in_proj(sb_o, ret_o, mem_o, proj, w_out, l, x2, norm_g3, w_in, trig)
        else:
            x2 = _out_proj(sb_o, ret_o, mem_o, proj, w_out, l, x2)
    return x2.reshape(b, s, d)
```

```python
import functools
import math

import jax
import jax.numpy as jnp
from jax import lax
from jax.experimental import pallas as pl
from jax.experimental.pallas import tpu as pltpu

D_MODEL = 1024
HEAD_DIM = 64
SB_HEADS = 6
RET_HEADS = 6
MEM_HEADS = 4
SB_WIDTH = SB_HEADS * HEAD_DIM
RET_WIDTH = RET_HEADS * HEAD_DIM
MEM_WIDTH = MEM_HEADS * HEAD_DIM
MIX_WIDTH = SB_WIDTH + RET_WIDTH + MEM_WIDTH
IN_WIDTH = 4 * SB_WIDTH + 4 * RET_WIDTH + 2 * MEM_WIDTH
ROPE_BASE = 10000.0
EPS = 1e-6
QK_SCALE = HEAD_DIM ** -0.5
SB_Q_SCALE = QK_SCALE * math.log2(math.e)

LANES = 128
PAIR = LANES // HEAD_DIM
SB_PAIRS = SB_HEADS // PAIR
RET_PAIRS = RET_HEADS // PAIR
MEM_PAIRS = MEM_HEADS // PAIR

COL_SB_Q = 0
COL_SB_K = COL_SB_Q + SB_PAIRS
COL_SB_V = COL_SB_K + SB_PAIRS
COL_SB_G = COL_SB_V + SB_PAIRS
COL_RET_Q = COL_SB_G + SB_PAIRS
COL_RET_K = COL_RET_Q + RET_PAIRS
COL_RET_V = COL_RET_K + RET_PAIRS
COL_RET_G = COL_RET_V + RET_PAIRS
COL_MEM_Q = COL_RET_G + RET_PAIRS
COL_MEM_G = COL_MEM_Q + MEM_PAIRS

PROJ_ROWS = 512
OUT_ROWS = 1024
PROJ_COLS = 512
SB_BLOCK = 256
RET_CHUNK = 128
MEM_ROWS = 512
V7X_VMEM_BYTES = 64 * 1024 * 1024
VMEM_LIMIT = V7X_VMEM_BYTES * 3 // 4

F32 = jnp.float32
BF16 = jnp.bfloat16


def _dot(a, b):
    return jnp.dot(a, b, preferred_element_type=F32)


def _dot_nt(a, b):
    return lax.dot_general(a, b, (((1,), (1,)), ((), ())), preferred_element_type=F32)


def _dot_tn(a, b):
    return lax.dot_general(a, b, (((0,), (0,)), ((), ())), preferred_element_type=F32)


def _silu(g):
    return g / (1.0 + jnp.exp(-g))


def _head_masks():
    lane = lax.broadcasted_iota(jnp.int32, (1, LANES), 1)
    first = lane < HEAD_DIM
    return first, jnp.logical_not(first)


def _pair_mean_square(t, first):
    sq = t * t
    s_a = jnp.sum(jnp.where(first, sq, 0.0), axis=-1, keepdims=True)
    s_b = jnp.sum(jnp.where(first, 0.0, sq), axis=-1, keepdims=True)
    return jnp.where(first, s_a, s_b) * (1.0 / HEAD_DIM)


def _rope_kernel(pos_ref, invf_ref, trig_ref):
    ang = pos_ref[...].astype(F32) * invf_ref[...]
    lane = lax.broadcasted_iota(jnp.int32, (1, LANES), 1)
    phase = jnp.where((lane % HEAD_DIM) < HEAD_DIM // 2, 0.0, math.pi / 2)
    trig_ref[...] = jnp.cos(ang - phase)


def _rope_tables(positions):
    b, s = positions.shape
    half = HEAD_DIM // 2
    inv_freq = ROPE_BASE ** (-jnp.arange(half, dtype=F32) / half)
    invf = jnp.tile(inv_freq, LANES // half)[None, :]
    pos = positions.reshape(b * s, 1)
    rows = s
    return pl.pallas_call(
        _rope_kernel,
        grid=(b * s // rows,),
        in_specs=[pl.BlockSpec((rows, 1), lambda i: (i, 0)),
                  pl.BlockSpec((1, LANES), lambda i: (0, 0))],
        out_specs=pl.BlockSpec((rows, LANES), lambda i: (i, 0)),
        out_shape=jax.ShapeDtypeStruct((b * s, LANES), F32),
        name="rope_tables",
    )(pos, invf)


def _half_lanes(t):
    lane = lax.broadcasted_iota(jnp.int32, (1, LANES), 1)
    half = HEAD_DIM // 2
    return (lane % HEAD_DIM) < half, pltpu.roll(t, LANES - half, 1), pltpu.roll(t, half, 1)


def _swap_halves(t):
    lower, from_upper, from_lower = _half_lanes(t)
    return jnp.where(lower, from_upper, from_lower)


def _norm_project(x, g_ref, w_ref, trig_ref, o_ref):
    ms = jnp.mean(x * x, axis=-1, keepdims=True)
    h = (x * lax.rsqrt(ms + EPS) * g_ref[0]).astype(BF16)
    trig = trig_ref[...]
    lower, from_upper, from_lower = _half_lanes(trig)
    cos = jnp.where(lower, trig, from_lower)
    sin = jnp.where(lower, -from_upper, trig)
    blocks_per_dot = PROJ_COLS // LANES
    for j in range(IN_WIDTH // PROJ_COLS):
        y = _dot(h, w_ref[0, :, j * PROJ_COLS:(j + 1) * PROJ_COLS].astype(BF16))
        for b in range(blocks_per_dot):
            blk = j * blocks_per_dot + b
            piece = y[:, b * LANES:(b + 1) * LANES]
            if blk < COL_SB_K:
                piece = piece * SB_Q_SCALE
            elif COL_RET_Q <= blk < COL_RET_V:
                piece = piece * cos + _swap_halves(piece) * sin
                if blk >= COL_RET_K:
                    piece = piece * QK_SCALE
            o_ref[:, blk * LANES:(blk + 1) * LANES] = piece.astype(BF16)


def _in_proj_kernel(x_ref, g_ref, w_ref, trig_ref, o_ref):
    _norm_project(x_ref[...], g_ref, w_ref, trig_ref, o_ref)


def _mix(sb_ref, ret_ref, mem_ref, sb_g_ref, w_ref, x_ref):
    sb = (sb_ref[...].astype(F32) * _silu(sb_g_ref[...].astype(F32))).astype(BF16)
    mixed = jnp.concatenate([sb, ret_ref[...], mem_ref[...]], axis=1)
    return x_ref[...] + _dot(mixed, w_ref[0].astype(BF16))


def _out_proj_kernel(sb_ref, ret_ref, mem_ref, sb_g_ref, w_ref, x_ref, o_ref):
    o_ref[...] = _mix(sb_ref, ret_ref, mem_ref, sb_g_ref, w_ref, x_ref)


def _out_in_proj_kernel(sb_ref, ret_ref, mem_ref, sb_g_ref, w_out_ref, x_ref, g_ref, w_in_ref,
                        trig_ref, x_out_ref, proj_ref):
    x_new = _mix(sb_ref, ret_ref, mem_ref, sb_g_ref, w_out_ref, x_ref)
    x_out_ref[...] = x_new
    _norm_project(x_new, g_ref, w_in_ref, trig_ref, proj_ref)


def _row_tiles(width, rows=PROJ_ROWS, col=0):
    return pl.BlockSpec((rows, width), lambda i: (i, col))


def _mixer_tiles(rows):
    assert COL_SB_G * LANES % SB_WIDTH == 0
    return [_row_tiles(SB_WIDTH, rows), _row_tiles(RET_WIDTH, rows), _row_tiles(MEM_WIDTH, rows),
            _row_tiles(SB_WIDTH, rows, COL_SB_G * LANES // SB_WIDTH)]


def _layer_param(shape, layer):
    return pl.BlockSpec((1,) + shape, lambda i: (layer, 0, 0), pipeline_mode=pl.Buffered(1))


def _in_proj(x2, g, w, layer, trig):
    m = x2.shape[0]
    return pl.pallas_call(
        _in_proj_kernel,
        grid=(m // PROJ_ROWS,),
        in_specs=[_row_tiles(D_MODEL), _layer_param((1, D_MODEL), layer),
                  _layer_param((D_MODEL, IN_WIDTH), layer), _row_tiles(LANES)],
        out_specs=_row_tiles(IN_WIDTH),
        out_shape=jax.ShapeDtypeStruct((m, IN_WIDTH), BF16),
        compiler_params=pltpu.CompilerParams(vmem_limit_bytes=VMEM_LIMIT),
        name="in_proj",
    )(x2, g, w, trig)


def _out_proj(sb_o, ret_o, mem_o, proj, w, layer, x2):
    m = x2.shape[0]
    return pl.pallas_call(
        _out_proj_kernel,
        grid=(m // OUT_ROWS,),
        in_specs=_mixer_tiles(OUT_ROWS) + [_layer_param((MIX_WIDTH, D_MODEL), layer),
                                           _row_tiles(D_MODEL, OUT_ROWS)],
        out_specs=_row_tiles(D_MODEL, OUT_ROWS),
        out_shape=jax.ShapeDtypeStruct((m, D_MODEL), F32),
        compiler_params=pltpu.CompilerParams(vmem_limit_bytes=VMEM_LIMIT),
        name="out_proj",
    )(sb_o, ret_o, mem_o, proj, w, x2)


def _out_in_proj(sb_o, ret_o, mem_o, proj, w_out, layer, x2, g, w_in, trig):
    m = x2.shape[0]
    return pl.pallas_call(
        _out_in_proj_kernel,
        grid=(m // PROJ_ROWS,),
        in_specs=_mixer_tiles(PROJ_ROWS) + [
                  _layer_param((MIX_WIDTH, D_MODEL), layer), _row_tiles(D_MODEL),
                  _layer_param((1, D_MODEL), layer + 1), _layer_param((D_MODEL, IN_WIDTH), layer + 1),
                  _row_tiles(LANES)],
        out_specs=[_row_tiles(D_MODEL), _row_tiles(IN_WIDTH)],
        out_shape=[jax.ShapeDtypeStruct((m, D_MODEL), F32),
                   jax.ShapeDtypeStruct((m, IN_WIDTH), BF16)],
        compiler_params=pltpu.CompilerParams(vmem_limit_bytes=VMEM_LIMIT),
        name="out_in_proj",
    )(sb_o, ret_o, mem_o, proj, w_out, x2, g, w_in, trig)


SB_LAG = 2
SB_STEP = 8
SB_MASKED = -1e30
SB_DEAD = 160.0
SB_IDLE_CARRY = 1e30


def _sb_schedule(nq):
    zero_slot, junk_slot = nq, nq + 1
    idle = (0, 0, junk_slot, junk_slot, 1, 1)
    items = [idle] * SB_STEP
    waves = []
    for w in range(nq):
        start = len(items)
        for qb in range(nq - 1, w - 1, -1):
            src = zero_slot if w == 0 else qb
            items.append((qb, qb - w, src, qb, 0 if qb > w else 1, 0))
        items += [idle] * (-(len(items) - start) % SB_STEP)
        waves.append((start, (len(items) - start) // SB_STEP))
    return jnp.asarray(items, jnp.int32).T, jnp.asarray(waves, jnp.int32).T, waves[0][1]


def _sb_kernel(tab_ref, wave_ref, q_ref, k_ref, v_ref, o_ref,
               tri_ref, bias_ref, z_ref, sp_ref, w_ref, r_ref, m_ref, acc_ref, *, n_waves,
               first_wave_steps):
    t = SB_BLOCK
    first, second = _head_masks()
    n_carry = r_ref.shape[0]

    @pl.when(jnp.logical_and(pl.program_id(0) == 0, pl.program_id(1) == 0))
    def _constants():
        row = lax.broadcasted_iota(jnp.int32, (t, t), 0)
        col = lax.broadcasted_iota(jnp.int32, (t, t), 1)
        tri_ref[...] = (row >= col).astype(BF16)
        bias_ref[...] = jnp.where(col < row, 0.0, SB_MASKED)
        r_ref[n_carry - 2] = jnp.zeros(r_ref.shape[1:], F32)

    z_ref[SB_STEP - 1] = jnp.full(z_ref.shape[1:], SB_MASKED, F32)
    sp_ref[SB_STEP - 1] = jnp.zeros(sp_ref.shape[1:], BF16)
    for slot in range(SB_STEP - SB_LAG, SB_STEP):
        w_ref[slot] = jnp.zeros(w_ref.shape[1:], BF16)
    r_ref[n_carry - 1] = jnp.full(r_ref.shape[1:], SB_IDLE_CARRY, F32)
    m_ref[1] = jnp.zeros(m_ref.shape[1:], F32)
    acc_ref[...] = jnp.zeros_like(acc_ref)

    def rows(blk):
        return pl.ds(pl.multiple_of(blk * t, t), t)

    def softplus2(z):
        return jnp.maximum(z, 0.0) + jnp.log2(1.0 + jnp.exp2(-jnp.abs(z)))

    half = t // 2

    def lower_rows(a):
        return [a[r0:r0 + half] for r0 in (half, t + half)]

    def zero_upper_rows(lower):
        zeros = jnp.zeros((half, half), lower[0].dtype)
        return jnp.concatenate([zeros, lower[0], zeros, lower[1]], axis=0)

    def scores(j, slot, diagonal):
        qb, kb = tab_ref[0, j], tab_ref[1, j]
        q = q_ref[rows(qb), :]
        q2 = jnp.concatenate([jnp.where(first, q, 0), jnp.where(second, q, 0)], axis=0)
        z = _dot_nt(q2, k_ref[rows(kb), :])
        if diagonal:
            bias = bias_ref[...]
            z = z + jnp.concatenate([bias, bias], axis=0)
            sp = jnp.concatenate(
                [softplus2(z[:, :half]),
                 zero_upper_rows([softplus2(p) for p in lower_rows(z[:, half:])])], axis=1)
        else:
            sp = softplus2(z)
        z_ref[slot] = z
        sp_ref[slot] = sp.astype(BF16)

    def weights(j, slot, track_minimum, diagonal):
        cum = _dot(sp_ref[slot], tri_ref[...])
        r = r_ref[tab_ref[2, j]]
        z = z_ref[slot]
        if diagonal:
            pieces = zip(lower_rows(z[:, half:]), lower_rows(cum[:, half:]), lower_rows(r))
            halves = [jnp.exp2(z[:, :half] - cum[:, :half] - r),
                      zero_upper_rows([jnp.exp2(a - b - c) for a, b, c in pieces])]
        else:
            halves = [jnp.exp2(z[:, c:c + LANES] - cum[:, c:c + LANES] - r) for c in (0, LANES)]
        w_ref[slot] = jnp.concatenate(halves, axis=1).astype(BF16)
        r_new = r + cum[:, 0:1]
        r_ref[tab_ref[3, j]] = r_new
        if track_minimum:
            m_slot = tab_ref[4, j]
            m_ref[m_slot] = jnp.minimum(m_ref[m_slot], r_new)

    def values(j, slot):
        qb, kb = tab_ref[0, j], tab_ref[1, j]
        pv = _dot(w_ref[slot], v_ref[rows(kb), :])
        acc_ref[rows(qb), :] += jnp.where(first, pv[:t], pv[t:])

    def run_wave(start, n_steps, diagonal):
        def step(n, _):
            for i in range(SB_STEP):
                c = start + SB_STEP * n + i
                values(c - 2, (i - 2) % SB_STEP)
                weights(c - 1, (i - 1) % SB_STEP, not diagonal, diagonal)
                scores(c, i, diagonal)
            return 0

        lax.fori_loop(0, n_steps, step, 0)
        return start + SB_STEP * n_steps

    def later_wave(state):
        w = state[0]
        m_ref[0] = jnp.full(m_ref.shape[1:], jnp.inf, F32)
        end = run_wave(wave_ref[0, w], wave_ref[1, w], False)
        return w + 1, end, jnp.min(m_ref[0])

    end = run_wave(SB_STEP, first_wave_steps, True)
    _, end, _ = lax.while_loop(lambda s: jnp.logical_and(s[0] < n_waves, s[2] < SB_DEAD),
                               later_wave, (jnp.int32(1), jnp.int32(end), jnp.float32(0.0)))
    @pl.when(tab_ref[5, end - 2] == 0)
    def _():
        values(end - 2, SB_STEP - 2)

    @pl.when(tab_ref[5, end - 1] == 0)
    def _():
        weights(end - 1, SB_STEP - 1, False, False)
        values(end - 1, SB_STEP - 1)

    o_ref[...] = acc_ref[...].astype(BF16)


def _sb_attention(proj, b, s):
    nq = s // SB_BLOCK
    assert SB_BLOCK == 2 * LANES
    table, waves, first_wave_steps = _sb_schedule(nq)
    t = SB_BLOCK
    seq = lambda col: pl.BlockSpec((s, LANES), lambda bi, p, tab, wav: (bi, col + p))
    return pl.pallas_call(
        functools.partial(_sb_kernel, n_waves=nq, first_wave_steps=first_wave_steps),
        grid_spec=pltpu.PrefetchScalarGridSpec(
            num_scalar_prefetch=2,
            grid=(b, SB_PAIRS),
            in_specs=[seq(COL_SB_Q), seq(COL_SB_K), seq(COL_SB_V)],
            out_specs=pl.BlockSpec((s, LANES), lambda bi, p, tab, wav: (bi, p)),
            scratch_shapes=[pltpu.VMEM((t, t), BF16),
                            pltpu.VMEM((t, t), F32),
                            pltpu.VMEM((SB_STEP, 2 * t, t), F32),
                            pltpu.VMEM((SB_STEP, 2 * t, t), BF16),
                            pltpu.VMEM((SB_STEP, 2 * t, t), BF16),
                            pltpu.VMEM((nq + 2, 2 * t, LANES), F32),
                            pltpu.VMEM((2, 2 * t, LANES), F32),
                            pltpu.VMEM((s, LANES), F32)]),
        out_shape=jax.ShapeDtypeStruct((b * s, SB_WIDTH), BF16),
        compiler_params=pltpu.CompilerParams(
            dimension_semantics=("arbitrary", "arbitrary"),
            vmem_limit_bytes=VMEM_LIMIT),
        name="sb_attention",
    )(table, waves, proj, proj, proj)


def _ret_kernel(q_ref, k_ref, v_ref, g_ref, ng_ref, lg_ref, o_ref, kv_ref, state_ref):
    c = RET_CHUNK
    first, second = _head_masks()
    nchunks = q_ref.shape[0] // c

    lg_lane = lg_ref[0]
    lg_a = lg_lane[:, 0:1]
    lg_b = lg_lane[:, HEAD_DIM:HEAD_DIM + 1]

    ri = lax.broadcasted_iota(jnp.int32, (c, c), 0)
    ci = lax.broadcasted_iota(jnp.int32, (c, c), 1)
    diff = (ri - ci).astype(F32)
    keep = ri >= ci
    decay_a = jnp.where(keep, jnp.exp(lg_a * jnp.maximum(diff, 0.0)), 0.0)
    decay_b = jnp.where(keep, jnp.exp(lg_b * jnp.maximum(diff, 0.0)), 0.0)
    idx = lax.broadcasted_iota(jnp.int32, (c, LANES), 0).astype(F32)
    q_decay = jnp.exp(lg_lane * (idx + 1.0))
    k_decay = jnp.exp(lg_lane * (c - 1.0 - idx))
    si = lax.broadcasted_iota(jnp.int32, (LANES, LANES), 0)
    sj = lax.broadcasted_iota(jnp.int32, (LANES, LANES), 1)
    same_head = (si < HEAD_DIM) == (sj < HEAD_DIM)
    state_decay = jnp.where(same_head, jnp.exp(lg_lane * float(c)), 0.0)
    norm_g = ng_ref[0]

    chunk_rows = [slice(n * c, (n + 1) * c) for n in range(nchunks)]

    for n, rows in enumerate(chunk_rows):
        kd = (k_ref[rows, :].astype(F32) * k_decay).astype(BF16)
        kv_ref[n] = jnp.where(same_head, _dot_tn(kd, v_ref[rows, :]), 0.0)

    state = jnp.zeros((LANES, LANES), F32)
    for n in range(nchunks):
        state_ref[n] = state.astype(BF16)
        state = state * state_decay + kv_ref[n]

    decay2 = jnp.concatenate([decay_a, decay_b], axis=1)
    for n, rows in enumerate(chunk_rows):
        qb = q_ref[rows, :]
        kb = k_ref[rows, :]
        v = v_ref[rows, :]
        k2 = jnp.concatenate([jnp.where(first, kb, 0), jnp.where(second, kb, 0)], axis=0)
        v2 = jnp.concatenate([jnp.where(first, v, 0), jnp.where(second, v, 0)], axis=0)
        scores = (_dot_nt(qb, k2) * decay2).astype(BF16)
        out = _dot(scores, v2) + _dot(qb, state_ref[n]) * q_decay
        inv = lax.rsqrt(_pair_mean_square(out, first) + EPS)
        y = out * inv * norm_g
        o_ref[rows, :] = (y * _silu(g_ref[rows, :].astype(F32))).astype(BF16)


def _retention(proj, ret_norm_g, layer, b, s):
    seq = lambda col: pl.BlockSpec((s, LANES), lambda bi, p: (bi, col + p))
    gamma = 1.0 - 2.0 ** (-5.0 - jnp.arange(RET_HEADS, dtype=F32))
    log_gamma = jnp.repeat(jnp.log(gamma), HEAD_DIM).reshape(RET_PAIRS, 1, LANES)
    return pl.pallas_call(
        _ret_kernel,
        grid=(b, RET_PAIRS),
        in_specs=[seq(COL_RET_Q), seq(COL_RET_K), seq(COL_RET_V), seq(COL_RET_G),
                  pl.BlockSpec((1, 1, LANES), lambda bi, p: (layer, 0, p)),
                  pl.BlockSpec((1, 1, LANES), lambda bi, p: (p, 0, 0))],
        out_specs=pl.BlockSpec((s, LANES), lambda bi, p: (bi, p)),
        out_shape=jax.ShapeDtypeStruct((b * s, RET_WIDTH), BF16),
        scratch_shapes=[pltpu.VMEM((s // RET_CHUNK, LANES, LANES), F32),
                        pltpu.VMEM((s // RET_CHUNK, LANES, LANES), BF16)],
        compiler_params=pltpu.CompilerParams(vmem_limit_bytes=VMEM_LIMIT),
        name="retention",
    )(proj, proj, proj, proj, ret_norm_g, log_gamma)


def _mem_kv_kernel(mem_ref, g_ref, w_ref, kg_ref, k_ref, v_ref):
    first, _ = _head_masks()
    x = mem_ref[...]
    ms = jnp.mean(x * x, axis=-1, keepdims=True)
    h = (x * lax.rsqrt(ms + EPS) * g_ref[0]).astype(BF16)
    kv = _dot(h, w_ref[0].astype(BF16))
    kg = kg_ref[0]
    for pair in range(MEM_PAIRS):
        cols = slice(pair * LANES, (pair + 1) * LANES)
        kp = kv[:, cols]
        kn = kp * lax.rsqrt(_pair_mean_square(kp, first) + EPS) * kg
        k_ref[0, :, cols] = kn.astype(BF16)
    v_ref[0] = kv[:, MEM_WIDTH:].astype(BF16)


def _mem_kv(mem2, mem_norm_g, w_mem_kv, k_norm_g):
    depth = w_mem_kv.shape[0]
    rows = mem2.shape[0]
    out = jax.ShapeDtypeStruct((depth, rows, MEM_WIDTH), BF16)
    return pl.pallas_call(
        _mem_kv_kernel,
        grid=(depth,),
        in_specs=[pl.BlockSpec((rows, D_MODEL), lambda l: (0, 0)),
                  pl.BlockSpec((1, 1, D_MODEL), lambda l: (l, 0, 0)),
                  pl.BlockSpec((1, D_MODEL, 2 * MEM_WIDTH), lambda l: (l, 0, 0)),
                  pl.BlockSpec((1, 1, LANES), lambda l: (l, 0, 0))],
        out_specs=[pl.BlockSpec((1, rows, MEM_WIDTH), lambda l: (l, 0, 0)),
                   pl.BlockSpec((1, rows, MEM_WIDTH), lambda l: (l, 0, 0))],
        out_shape=[out, out],
        compiler_params=pltpu.CompilerParams(vmem_limit_bytes=VMEM_LIMIT),
        name="mem_kv",
    )(mem2, mem_norm_g, w_mem_kv, k_norm_g)


def _mem_attn_kernel(q_ref, g_ref, k_ref, v_ref, qg_ref, o_ref, e_ref, den_ref):
    first, second = _head_masks()
    tile = MEM_ROWS
    n_tiles = q_ref.shape[0] // tile
    k = k_ref[0]
    v = v_ref[0]

    def probabilities(i, slot):
        rows = slice(i * tile, (i + 1) * tile)
        q = q_ref[rows, :].astype(F32)
        inv = lax.rsqrt(_pair_mean_square(q, first) + EPS) * QK_SCALE
        qn = (q * inv * qg_ref[0]).astype(BF16)
        q2 = jnp.concatenate([jnp.where(first, qn, 0), jnp.where(second, qn, 0)], axis=0)
        sc = _dot_nt(q2, k)
        e = jnp.exp(sc - jnp.max(sc, axis=-1, keepdims=True))
        den_ref[slot] = jnp.broadcast_to(jnp.sum(e, axis=-1, keepdims=True), (2 * tile, LANES))
        e_ref[slot] = e.astype(BF16)

    def output(i, slot):
        rows = slice(i * tile, (i + 1) * tile)
        pv = _dot(e_ref[slot], v) / den_ref[slot]
        out = jnp.where(first, pv[:tile], pv[tile:])
        o_ref[rows, :] = (out * _silu(g_ref[rows, :].astype(F32))).astype(BF16)

    for i in range(n_tiles + 1):
        if i >= 1:
            output(i - 1, (i - 1) % 2)
        if i < n_tiles:
            probabilities(i, i % 2)


def _mem_attention(proj, mk, mv, q_norm_g, layer, b, s, tokens):
    seq = lambda col: pl.BlockSpec((s, LANES), lambda bi, p: (bi, col + p))
    kv_spec = pl.BlockSpec((1, tokens, LANES), lambda bi, p: (layer, bi, p))
    return pl.pallas_call(
        _mem_attn_kernel,
        grid=(b, MEM_PAIRS),
        in_specs=[seq(COL_MEM_Q), seq(COL_MEM_G), kv_spec, kv_spec,
                  pl.BlockSpec((1, 1, LANES), lambda bi, p: (layer, 0, 0))],
        out_specs=pl.BlockSpec((s, LANES), lambda bi, p: (bi, p)),
        out_shape=jax.ShapeDtypeStruct((b * s, MEM_WIDTH), BF16),
        scratch_shapes=[pltpu.VMEM((2, 2 * MEM_ROWS, tokens), BF16),
                        pltpu.VMEM((2, 2 * MEM_ROWS, LANES), F32)],
        compiler_params=pltpu.CompilerParams(vmem_limit_bytes=VMEM_LIMIT),
        name="mem_attention",
    )(proj, proj, mk, mv, q_norm_g)


def kernel(x, mem, positions, norm_g, w_in, w_out, mem_norm_g, w_mem_kv, mem_q_norm_g,
           mem_k_norm_g, ret_norm_g):
    b, s, d = x.shape
    tokens = mem.shape[1]
    depth = w_in.shape[0]
    assert d == D_MODEL and w_in.shape[2] == IN_WIDTH
    assert s % SB_BLOCK == 0 and s % RET_CHUNK == 0 and s % MEM_ROWS == 0
    assert (b * s) % PROJ_ROWS == 0 and (b * s) % OUT_ROWS == 0

    norm_g3 = norm_g[:, None, :]
    ret_norm_g3 = ret_norm_g[:, None, :]
    q_norm_g3 = jnp.tile(mem_q_norm_g, (1, PAIR))[:, None, :]
    k_norm_g3 = jnp.tile(mem_k_norm_g, (1, PAIR))[:, None, :]

    trig = _rope_tables(positions)
    mk_all, mv_all = _mem_kv(mem.reshape(b * tokens, d), mem_norm_g[:, None, :], w_mem_kv, k_norm_g3)

    x2 = x.reshape(b * s, d)
    proj = _in_proj(x2, norm_g3, w_in, 0, trig)
    for l in range(depth):
        sb_o = _sb_attention(proj, b, s)
        ret_o = _retention(proj, ret_norm_g3, l, b, s)
        mem_o = _mem_attention(proj, mk_all, mv_all, q_norm_g3, l, b, s, tokens)
        if l + 1 < depth:
            x2, proj = _out_in_proj(sb_o, ret_o, mem_o, proj, w_out, l, x2, norm_g3, w_in, trig)
        else:
            x2 = _out_proj(sb_o, ret_o, mem_o, proj, w_out, l, x2)
    return x2.reshape(b, s, d)
```

```python
import functools
import math

import jax
import jax.numpy as jnp
from jax import lax
from jax.experimental import pallas as pl
from jax.experimental.pallas import tpu as pltpu

D_MODEL = 1024
HEAD_DIM = 64
SB_HEADS = 6
RET_HEADS = 6
MEM_HEADS = 4
SB_WIDTH = SB_HEADS * HEAD_DIM
RET_WIDTH = RET_HEADS * HEAD_DIM
MEM_WIDTH = MEM_HEADS * HEAD_DIM
MIX_WIDTH = SB_WIDTH + RET_WIDTH + MEM_WIDTH
IN_WIDTH = 4 * SB_WIDTH + 4 * RET_WIDTH + 2 * MEM_WIDTH
ROPE_BASE = 10000.0
EPS = 1e-6
QK_SCALE = HEAD_DIM ** -0.5
SB_Q_SCALE = QK_SCALE * math.log2(math.e)

LANES = 128
PAIR = LANES // HEAD_DIM
SB_PAIRS = SB_HEADS // PAIR
RET_PAIRS = RET_HEADS // PAIR
MEM_PAIRS = MEM_HEADS // PAIR

COL_SB_Q = 0
COL_SB_K = COL_SB_Q + SB_PAIRS
COL_SB_V = COL_SB_K + SB_PAIRS
COL_SB_G = COL_SB_V + SB_PAIRS
COL_RET_Q = COL_SB_G + SB_PAIRS
COL_RET_K = COL_RET_Q + RET_PAIRS
COL_RET_V = COL_RET_K + RET_PAIRS
COL_RET_G = COL_RET_V + RET_PAIRS
COL_MEM_Q = COL_RET_G + RET_PAIRS
COL_MEM_G = COL_MEM_Q + MEM_PAIRS

PROJ_ROWS = 512
OUT_ROWS = 1024
PROJ_COLS = 512
SB_BLOCK = 256
RET_CHUNK = 128
MEM_ROWS = 512
V7X_VMEM_BYTES = 64 * 1024 * 1024
VMEM_LIMIT = V7X_VMEM_BYTES * 3 // 4

F32 = jnp.float32
BF16 = jnp.bfloat16


def _dot(a, b):
    return jnp.dot(a, b, preferred_element_type=F32)


def _dot_nt(a, b):
    return lax.dot_general(a, b, (((1,), (1,)), ((), ())), preferred_element_type=F32)


def _dot_tn(a, b):
    return lax.dot_general(a, b, (((0,), (0,)), ((), ())), preferred_element_type=F32)


def _silu(g):
    return g / (1.0 + jnp.exp(-g))


def _head_masks():
    lane = lax.broadcasted_iota(jnp.int32, (1, LANES), 1)
    first = lane < HEAD_DIM
    return first, jnp.logical_not(first)


def _pair_mean_square(t, first):
    sq = t * t
    s_a = jnp.sum(jnp.where(first, sq, 0.0), axis=-1, keepdims=True)
    s_b = jnp.sum(jnp.where(first, 0.0, sq), axis=-1, keepdims=True)
    return jnp.where(first, s_a, s_b) * (1.0 / HEAD_DIM)


def _rope_kernel(pos_ref, invf_ref, trig_ref):
    ang = pos_ref[...].astype(F32) * invf_ref[...]
    lane = lax.broadcasted_iota(jnp.int32, (1, LANES), 1)
    phase = jnp.where((lane % HEAD_DIM) < HEAD_DIM // 2, 0.0, math.pi / 2)
    trig_ref[...] = jnp.cos(ang - phase)


def _rope_tables(positions):
    b, s = positions.shape
    half = HEAD_DIM // 2
    inv_freq = ROPE_BASE ** (-jnp.arange(half, dtype=F32) / half)
    invf = jnp.tile(inv_freq, LANES // half)[None, :]
    pos = positions.reshape(b * s, 1)
    rows = s
    return pl.pallas_call(
        _rope_kernel,
        grid=(b * s // rows,),
        in_specs=[pl.BlockSpec((rows, 1), lambda i: (i, 0)),
                  pl.BlockSpec((1, LANES), lambda i: (0, 0))],
        out_specs=pl.BlockSpec((rows, LANES), lambda i: (i, 0)),
        out_shape=jax.ShapeDtypeStruct((b * s, LANES), F32),
        name="rope_tables",
    )(pos, invf)


def _half_lanes(t):
    lane = lax.broadcasted_iota(jnp.int32, (1, LANES), 1)
    half = HEAD_DIM // 2
    return (lane % HEAD_DIM) < half, pltpu.roll(t, LANES - half, 1), pltpu.roll(t, half, 1)


def _swap_halves(t):
    lower, from_upper, from_lower = _half_lanes(t)
    return jnp.where(lower, from_upper, from_lower)


def _norm_project(x, g_ref, w_ref, trig_ref, o_ref):
    ms = jnp.mean(x * x, axis=-1, keepdims=True)
    h = (x * lax.rsqrt(ms + EPS) * g_ref[0]).astype(BF16)
    trig = trig_ref[...]
    lower, from_upper, from_lower = _half_lanes(trig)
    cos = jnp.where(lower, trig, from_lower)
    sin = jnp.where(lower, -from_upper, trig)
    blocks_per_dot = PROJ_COLS // LANES
    for j in range(IN_WIDTH // PROJ_COLS):
        y = _dot(h, w_ref[0, :, j * PROJ_COLS:(j + 1) * PROJ_COLS].astype(BF16))
        for b in range(blocks_per_dot):
            blk = j * blocks_per_dot + b
            piece = y[:, b * LANES:(b + 1) * LANES]
            if blk < COL_SB_K:
                piece = piece * SB_Q_SCALE
            elif COL_RET_Q <= blk < COL_RET_V:
                piece = piece * cos + _swap_halves(piece) * sin
                if blk >= COL_RET_K:
                    piece = piece * QK_SCALE
            o_ref[:, blk * LANES:(blk + 1) * LANES] = piece.astype(BF16)


def _in_proj_kernel(x_ref, g_ref, w_ref, trig_ref, o_ref):
    _norm_project(x_ref[...], g_ref, w_ref, trig_ref, o_ref)


def _mix(sb_ref, ret_ref, mem_ref, sb_g_ref, w_ref, x_ref):
    sb = (sb_ref[...].astype(F32) * _silu(sb_g_ref[...].astype(F32))).astype(BF16)
    mixed = jnp.concatenate([sb, ret_ref[...], mem_ref[...]], axis=1)
    return x_ref[...] + _dot(mixed, w_ref[0].astype(BF16))


def _out_proj_kernel(sb_ref, ret_ref, mem_ref, sb_g_ref, w_ref, x_ref, o_ref):
    o_ref[...] = _mix(sb_ref, ret_ref, mem_ref, sb_g_ref, w_ref, x_ref)


def _out_in_proj_kernel(sb_ref, ret_ref, mem_ref, sb_g_ref, w_out_ref, x_ref, g_ref, w_in_ref,
                        trig_ref, x_out_ref, proj_ref):
    x_new = _mix(sb_ref, ret_ref, mem_ref, sb_g_ref, w_out_ref, x_ref)
    x_out_ref[...] = x_new
    _norm_project(x_new, g_ref, w_in_ref, trig_ref, proj_ref)


def _row_tiles(width, rows=PROJ_ROWS, col=0):
    return pl.BlockSpec((rows, width), lambda i: (i, col))


def _mixer_tiles(rows):
    assert COL_SB_G * LANES % SB_WIDTH == 0
    return [_row_tiles(SB_WIDTH, rows), _row_tiles(RET_WIDTH, rows), _row_tiles(MEM_WIDTH, rows),
            _row_tiles(SB_WIDTH, rows, COL_SB_G * LANES // SB_WIDTH)]


def _layer_param(shape, layer):
    return pl.BlockSpec((1,) + shape, lambda i: (layer, 0, 0), pipeline_mode=pl.Buffered(1))


def _in_proj(x2, g, w, layer, trig):
    m = x2.shape[0]
    return pl.pallas_call(
        _in_proj_kernel,
        grid=(m // PROJ_ROWS,),
        in_specs=[_row_tiles(D_MODEL), _layer_param((1, D_MODEL), layer),
                  _layer_param((D_MODEL, IN_WIDTH), layer), _row_tiles(LANES)],
        out_specs=_row_tiles(IN_WIDTH),
        out_shape=jax.ShapeDtypeStruct((m, IN_WIDTH), BF16),
        compiler_params=pltpu.CompilerParams(vmem_limit_bytes=VMEM_LIMIT),
        name="in_proj",
    )(x2, g, w, trig)


def _out_proj(sb_o, ret_o, mem_o, proj, w, layer, x2):
    m = x2.shape[0]
    return pl.pallas_call(
        _out_proj_kernel,
        grid=(m // OUT_ROWS,),
        in_specs=_mixer_tiles(OUT_ROWS) + [_layer_param((MIX_WIDTH, D_MODEL), layer),
                                           _row_tiles(D_MODEL, OUT_ROWS)],
        out_specs=_row_tiles(D_MODEL, OUT_ROWS),
        out_shape=jax.ShapeDtypeStruct((m, D_MODEL), F32),
        compiler_params=pltpu.CompilerParams(vmem_limit_bytes=VMEM_LIMIT),
        name="out_proj",
    )(sb_o, ret_o, mem_o, proj, w, x2)


def _out_in_proj(sb_o, ret_o, mem_o, proj, w_out, layer, x2, g, w_in, trig):
    m = x2.shape[0]
    return pl.pallas_call(
        _out_in_proj_kernel,
        grid=(m // PROJ_ROWS,),
        in_specs=_mixer_tiles(PROJ_ROWS) + [
                  _layer_param((MIX_WIDTH, D_MODEL), layer), _row_tiles(D_MODEL),
                  _layer_param((1, D_MODEL), layer + 1), _layer_param((D_MODEL, IN_WIDTH), layer + 1),
                  _row_tiles(LANES)],
        out_specs=[_row_tiles(D_MODEL), _row_tiles(IN_WIDTH)],
        out_shape=[jax.ShapeDtypeStruct((m, D_MODEL), F32),
                   jax.ShapeDtypeStruct((m, IN_WIDTH), BF16)],
        compiler_params=pltpu.CompilerParams(vmem_limit_bytes=VMEM_LIMIT),
        name="out_in_proj",
    )(sb_o, ret_o, mem_o, proj, w_out, x2, g, w_in, trig)


SB_LAG = 2
SB_STEP = 8
SB_MASKED = -1e30
SB_DEAD = 160.0
SB_IDLE_CARRY = 1e30


def _sb_schedule(nq):
    zero_slot, junk_slot = nq, nq + 1
    idle = (0, 0, junk_slot, junk_slot, 1, 1)
    items = [idle] * SB_STEP
    waves = []
    for w in range(nq):
        start = len(items)
        for qb in range(nq - 1, w - 1, -1):
            src = zero_slot if w == 0 else qb
            items.append((qb, qb - w, src, qb, 0 if qb > w else 1, 0))
        items += [idle] * (-(len(items) - start) % SB_STEP)
        waves.append((start, (len(items) - start) // SB_STEP))
    return jnp.asarray(items, jnp.int32).T, jnp.asarray(waves, jnp.int32).T, waves[:2]


def _sb_kernel(tab_ref, wave_ref, q_ref, k_ref, v_ref, o_ref,
               tri_ref, bias_ref, z_ref, sp_ref, w_ref, r_ref, m_ref, acc_ref, *, n_waves,
               static_waves):
    t = SB_BLOCK
    first, second = _head_masks()
    n_carry = r_ref.shape[0]

    @pl.when(jnp.logical_and(pl.program_id(0) == 0, pl.program_id(1) == 0))
    def _constants():
        row = lax.broadcasted_iota(jnp.int32, (t, t), 0)
        col = lax.broadcasted_iota(jnp.int32, (t, t), 1)
        tri_ref[...] = (row >= col).astype(BF16)
        bias_ref[...] = jnp.where(col < row, 0.0, SB_MASKED)
        r_ref[n_carry - 2] = jnp.zeros(r_ref.shape[1:], F32)

    z_ref[SB_STEP - 1] = jnp.full(z_ref.shape[1:], SB_MASKED, F32)
    sp_ref[SB_STEP - 1] = jnp.zeros(sp_ref.shape[1:], BF16)
    for slot in range(SB_STEP - SB_LAG, SB_STEP):
        w_ref[slot] = jnp.zeros(w_ref.shape[1:], BF16)
    r_ref[n_carry - 1] = jnp.full(r_ref.shape[1:], SB_IDLE_CARRY, F32)
    m_ref[1] = jnp.zeros(m_ref.shape[1:], F32)
    acc_ref[...] = jnp.zeros_like(acc_ref)

    def rows(blk):
        return pl.ds(pl.multiple_of(blk * t, t), t)

    def softplus2(z):
        return jnp.maximum(z, 0.0) + jnp.log2(1.0 + jnp.exp2(-jnp.abs(z)))

    half = t // 2

    def lower_rows(a):
        return [a[r0:r0 + half] for r0 in (half, t + half)]

    def zero_upper_rows(lower):
        zeros = jnp.zeros((half, half), lower[0].dtype)
        return jnp.concatenate([zeros, lower[0], zeros, lower[1]], axis=0)

    def scores(j, slot, diagonal):
        qb, kb = tab_ref[0, j], tab_ref[1, j]
        q = q_ref[rows(qb), :]
        q2 = jnp.concatenate([jnp.where(first, q, 0), jnp.where(second, q, 0)], axis=0)
        z = _dot_nt(q2, k_ref[rows(kb), :])
        if diagonal:
            bias = bias_ref[...]
            z = z + jnp.concatenate([bias, bias], axis=0)
            sp = jnp.concatenate(
                [softplus2(z[:, :half]),
                 zero_upper_rows([softplus2(p) for p in lower_rows(z[:, half:])])], axis=1)
        else:
            sp = softplus2(z)
        z_ref[slot] = z
        sp_ref[slot] = sp.astype(BF16)

    def weights(j, slot, track_minimum, diagonal):
        cum = _dot(sp_ref[slot], tri_ref[...])
        r = r_ref[tab_ref[2, j]]
        z = z_ref[slot]
        if diagonal:
            pieces = zip(lower_rows(z[:, half:]), lower_rows(cum[:, half:]), lower_rows(r))
            halves = [jnp.exp2(z[:, :half] - cum[:, :half] - r),
                      zero_upper_rows([jnp.exp2(a - b - c) for a, b, c in pieces])]
        else:
            halves = [jnp.exp2(z[:, c:c + LANES] - cum[:, c:c + LANES] - r) for c in (0, LANES)]
        w_ref[slot] = jnp.concatenate(halves, axis=1).astype(BF16)
        r_new = r + cum[:, 0:1]
        r_ref[tab_ref[3, j]] = r_new
        if track_minimum:
            m_slot = tab_ref[4, j]
            m_ref[m_slot] = jnp.minimum(m_ref[m_slot], r_new)

    def values(j, slot):
        qb, kb = tab_ref[0, j], tab_ref[1, j]
        pv = _dot(w_ref[slot], v_ref[rows(kb), :])
        acc_ref[rows(qb), :] += jnp.where(first, pv[:t], pv[t:])

    def run_wave(start, n_steps, diagonal):
        def step(n, _):
            for i in range(SB_STEP):
                c = start + SB_STEP * n + i
                values(c - 2, (i - 2) % SB_STEP)
                weights(c - 1, (i - 1) % SB_STEP, not diagonal, diagonal)
                scores(c, i, diagonal)
            return 0

        if isinstance(n_steps, int):
            for n in range(n_steps):
                step(n, 0)
        else:
            lax.fori_loop(0, n_steps, step, 0)
        return start + SB_STEP * n_steps

    def later_wave(state):
        w = state[0]
        m_ref[0] = jnp.full(m_ref.shape[1:], jnp.inf, F32)
        end = run_wave(wave_ref[0, w], wave_ref[1, w], False)
        return w + 1, end, jnp.min(m_ref[0])

    end = run_wave(*static_waves[0], True)
    state = (1, end, jnp.float32(0.0))
    if len(static_waves) > 1:
        m_ref[0] = jnp.full(m_ref.shape[1:], jnp.inf, F32)
        end = run_wave(*static_waves[1], False)
        state = (2, end, jnp.min(m_ref[0]))
    _, end, _ = lax.while_loop(lambda s: jnp.logical_and(s[0] < n_waves, s[2] < SB_DEAD),
                               later_wave, (jnp.int32(state[0]), jnp.int32(state[1]), state[2]))
    @pl.when(tab_ref[5, end - 2] == 0)
    def _():
        values(end - 2, SB_STEP - 2)

    @pl.when(tab_ref[5, end - 1] == 0)
    def _():
        weights(end - 1, SB_STEP - 1, False, False)
        values(end - 1, SB_STEP - 1)

    o_ref[...] = acc_ref[...].astype(BF16)


def _sb_attention(proj, b, s):
    nq = s // SB_BLOCK
    assert SB_BLOCK == 2 * LANES
    table, waves, static_waves = _sb_schedule(nq)
    t = SB_BLOCK
    seq = lambda col: pl.BlockSpec((s, LANES), lambda bi, p, tab, wav: (bi, col + p))
    return pl.pallas_call(
        functools.partial(_sb_kernel, n_waves=nq, static_waves=static_waves),
        grid_spec=pltpu.PrefetchScalarGridSpec(
            num_scalar_prefetch=2,
            grid=(b, SB_PAIRS),
            in_specs=[seq(COL_SB_Q), seq(COL_SB_K), seq(COL_SB_V)],
            out_specs=pl.BlockSpec((s, LANES), lambda bi, p, tab, wav: (bi, p)),
            scratch_shapes=[pltpu.VMEM((t, t), BF16),
                            pltpu.VMEM((t, t), F32),
                            pltpu.VMEM((SB_STEP, 2 * t, t), F32),
                            pltpu.VMEM((SB_STEP, 2 * t, t), BF16),
                            pltpu.VMEM((SB_STEP, 2 * t, t), BF16),
                            pltpu.VMEM((nq + 2, 2 * t, LANES), F32),
                            pltpu.VMEM((2, 2 * t, LANES), F32),
                            pltpu.VMEM((s, LANES), F32)]),
        out_shape=jax.ShapeDtypeStruct((b * s, SB_WIDTH), BF16),
        compiler_params=pltpu.CompilerParams(
            dimension_semantics=("arbitrary", "arbitrary"),
            vmem_limit_bytes=VMEM_LIMIT),
        name="sb_attention",
    )(table, waves, proj, proj, proj)


def _ret_kernel(q_ref, k_ref, v_ref, g_ref, ng_ref, lg_ref, o_ref, kv_ref, state_ref):
    c = RET_CHUNK
    first, second = _head_masks()
    nchunks = q_ref.shape[0] // c

    lg_lane = lg_ref[0]
    lg_a = lg_lane[:, 0:1]
    lg_b = lg_lane[:, HEAD_DIM:HEAD_DIM + 1]

    ri = lax.broadcasted_iota(jnp.int32, (c, c), 0)
    ci = lax.broadcasted_iota(jnp.int32, (c, c), 1)
    diff = (ri - ci).astype(F32)
    keep = ri >= ci
    decay_a = jnp.where(keep, jnp.exp(lg_a * jnp.maximum(diff, 0.0)), 0.0)
    decay_b = jnp.where(keep, jnp.exp(lg_b * jnp.maximum(diff, 0.0)), 0.0)
    idx = lax.broadcasted_iota(jnp.int32, (c, LANES), 0).astype(F32)
    q_decay = jnp.exp(lg_lane * (idx + 1.0))
    k_decay = jnp.exp(lg_lane * (c - 1.0 - idx))
    si = lax.broadcasted_iota(jnp.int32, (LANES, LANES), 0)
    sj = lax.broadcasted_iota(jnp.int32, (LANES, LANES), 1)
    same_head = (si < HEAD_DIM) == (sj < HEAD_DIM)
    state_decay = jnp.where(same_head, jnp.exp(lg_lane * float(c)), 0.0)
    norm_g = ng_ref[0]

    chunk_rows = [slice(n * c, (n + 1) * c) for n in range(nchunks)]

    for n, rows in enumerate(chunk_rows):
        kd = (k_ref[rows, :].astype(F32) * k_decay).astype(BF16)
        kv_ref[n] = jnp.where(same_head, _dot_tn(kd, v_ref[rows, :]), 0.0)

    state = jnp.zeros((LANES, LANES), F32)
    for n in range(nchunks):
        state_ref[n] = state.astype(BF16)
        state = state * state_decay + kv_ref[n]

    decay2 = jnp.concatenate([decay_a, decay_b], axis=1)
    for n, rows in enumerate(chunk_rows):
        qb = q_ref[rows, :]
        kb = k_ref[rows, :]
        v = v_ref[rows, :]
        k2 = jnp.concatenate([jnp.where(first, kb, 0), jnp.where(second, kb, 0)], axis=0)
        v2 = jnp.concatenate([jnp.where(first, v, 0), jnp.where(second, v, 0)], axis=0)
        scores = (_dot_nt(qb, k2) * decay2).astype(BF16)
        out = _dot(scores, v2) + _dot(qb, state_ref[n]) * q_decay
        inv = lax.rsqrt(_pair_mean_square(out, first) + EPS)
        y = out * inv * norm_g
        o_ref[rows, :] = (y * _silu(g_ref[rows, :].astype(F32))).astype(BF16)


def _retention(proj, ret_norm_g, layer, b, s):
    seq = lambda col: pl.BlockSpec((s, LANES), lambda bi, p: (bi, col + p))
    gamma = 1.0 - 2.0 ** (-5.0 - jnp.arange(RET_HEADS, dtype=F32))
    log_gamma = jnp.repeat(jnp.log(gamma), HEAD_DIM).reshape(RET_PAIRS, 1, LANES)
    return pl.pallas_call(
        _ret_kernel,
        grid=(b, RET_PAIRS),
        in_specs=[seq(COL_RET_Q), seq(COL_RET_K), seq(COL_RET_V), seq(COL_RET_G),
                  pl.BlockSpec((1, 1, LANES), lambda bi, p: (layer, 0, p)),
                  pl.BlockSpec((1, 1, LANES), lambda bi, p: (p, 0, 0))],
        out_specs=pl.BlockSpec((s, LANES), lambda bi, p: (bi, p)),
        out_shape=jax.ShapeDtypeStruct((b * s, RET_WIDTH), BF16),
        scratch_shapes=[pltpu.VMEM((s // RET_CHUNK, LANES, LANES), F32),
                        pltpu.VMEM((s // RET_CHUNK, LANES, LANES), BF16)],
        compiler_params=pltpu.CompilerParams(vmem_limit_bytes=VMEM_LIMIT),
        name="retention",
    )(proj, proj, proj, proj, ret_norm_g, log_gamma)


def _mem_kv_kernel(mem_ref, g_ref, w_ref, kg_ref, k_ref, v_ref):
    first, _ = _head_masks()
    x = mem_ref[...]
    ms = jnp.mean(x * x, axis=-1, keepdims=True)
    h = (x * lax.rsqrt(ms + EPS) * g_ref[0]).astype(BF16)
    kv = _dot(h, w_ref[0].astype(BF16))
    kg = kg_ref[0]
    for pair in range(MEM_PAIRS):
        cols = slice(pair * LANES, (pair + 1) * LANES)
        kp = kv[:, cols]
        kn = kp * lax.rsqrt(_pair_mean_square(kp, first) + EPS) * kg
        k_ref[0, :, cols] = kn.astype(BF16)
    v_ref[0] = kv[:, MEM_WIDTH:].astype(BF16)


def _mem_kv(mem2, mem_norm_g, w_mem_kv, k_norm_g):
    depth = w_mem_kv.shape[0]
    rows = mem2.shape[0]
    out = jax.ShapeDtypeStruct((depth, rows, MEM_WIDTH), BF16)
    return pl.pallas_call(
        _mem_kv_kernel,
        grid=(depth,),
        in_specs=[pl.BlockSpec((rows, D_MODEL), lambda l: (0, 0)),
                  pl.BlockSpec((1, 1, D_MODEL), lambda l: (l, 0, 0)),
                  pl.BlockSpec((1, D_MODEL, 2 * MEM_WIDTH), lambda l: (l, 0, 0)),
                  pl.BlockSpec((1, 1, LANES), lambda l: (l, 0, 0))],
        out_specs=[pl.BlockSpec((1, rows, MEM_WIDTH), lambda l: (l, 0, 0)),
                   pl.BlockSpec((1, rows, MEM_WIDTH), lambda l: (l, 0, 0))],
        out_shape=[out, out],
        compiler_params=pltpu.CompilerParams(vmem_limit_bytes=VMEM_LIMIT),
        name="mem_kv",
    )(mem2, mem_norm_g, w_mem_kv, k_norm_g)


def _mem_attn_kernel(q_ref, g_ref, k_ref, v_ref, qg_ref, o_ref, e_ref, den_ref):
    first, second = _head_masks()
    tile = MEM_ROWS
    n_tiles = q_ref.shape[0] // tile
    k = k_ref[0]
    v = v_ref[0]

    def probabilities(i, slot):
        rows = slice(i * tile, (i + 1) * tile)
        q = q_ref[rows, :].astype(F32)
        inv = lax.rsqrt(_pair_mean_square(q, first) + EPS) * QK_SCALE
        qn = (q * inv * qg_ref[0]).astype(BF16)
        q2 = jnp.concatenate([jnp.where(first, qn, 0), jnp.where(second, qn, 0)], axis=0)
        sc = _dot_nt(q2, k)
        e = jnp.exp(sc - jnp.max(sc, axis=-1, keepdims=True))
        den_ref[slot] = jnp.broadcast_to(jnp.sum(e, axis=-1, keepdims=True), (2 * tile, LANES))
        e_ref[slot] = e.astype(BF16)

    def output(i, slot):
        rows = slice(i * tile, (i + 1) * tile)
        pv = _dot(e_ref[slot], v) / den_ref[slot]
        out = jnp.where(first, pv[:tile], pv[tile:])
        o_ref[rows, :] = (out * _silu(g_ref[rows, :].astype(F32))).astype(BF16)

    for i in range(n_tiles + 1):
        if i >= 1:
            output(i - 1, (i - 1) % 2)
        if i < n_tiles:
            probabilities(i, i % 2)


def _mem_attention(proj, mk, mv, q_norm_g, layer, b, s, tokens):
    seq = lambda col: pl.BlockSpec((s, LANES), lambda bi, p: (bi, col + p))
    kv_spec = pl.BlockSpec((1, tokens, LANES), lambda bi, p: (layer, bi, p))
    return pl.pallas_call(
        _mem_attn_kernel,
        grid=(b, MEM_PAIRS),
        in_specs=[seq(COL_MEM_Q), seq(COL_MEM_G), kv_spec, kv_spec,
                  pl.BlockSpec((1, 1, LANES), lambda bi, p: (layer, 0, 0))],
        out_specs=pl.BlockSpec((s, LANES), lambda bi, p: (bi, p)),
        out_shape=jax.ShapeDtypeStruct((b * s, MEM_WIDTH), BF16),
        scratch_shapes=[pltpu.VMEM((2, 2 * MEM_ROWS, tokens), BF16),
                        pltpu.VMEM((2, 2 * MEM_ROWS, LANES), F32)],
        compiler_params=pltpu.CompilerParams(vmem_limit_bytes=VMEM_LIMIT),
        name="mem_attention",
    )(proj, proj, mk, mv, q_norm_g)


def kernel(x, mem, positions, norm_g, w_in, w_out, mem_norm_g, w_mem_kv, mem_q_norm_g,
           mem_k_norm_g, ret_norm_g):
    b, s, d = x.shape
    tokens = mem.shape[1]
    depth = w_in.shape[0]
    assert d == D_MODEL and w_in.shape[2] == IN_WIDTH
    assert s % SB_BLOCK == 0 and s % RET_CHUNK == 0 and s % MEM_ROWS == 0
    assert (b * s) % PROJ_ROWS == 0 and (b * s) % OUT_ROWS == 0

    norm_g3 = norm_g[:, None, :]
    ret_norm_g3 = ret_norm_g[:, None, :]
    q_norm_g3 = jnp.tile(mem_q_norm_g, (1, PAIR))[:, None, :]
    k_norm_g3 = jnp.tile(mem_k_norm_g, (1, PAIR))[:, None, :]

    trig = _rope_tables(positions)
    mk_all, mv_all = _mem_kv(mem.reshape(b * tokens, d), mem_norm_g[:, None, :], w_mem_kv, k_norm_g3)

    x2 = x.reshape(b * s, d)
    proj = _in_proj(x2, norm_g3, w_in, 0, trig)
    for l in range(depth):
        sb_o = _sb_attention(proj, b, s)
        ret_o = _retention(proj, ret_norm_g3, l, b, s)
        mem_o = _mem_attention(proj, mk_all, mv_all, q_norm_g3, l, b, s, tokens)
        if l + 1 < depth:
            x2, proj = _out_in_proj(sb_o, ret_o, mem_o, proj, w_out, l, x2, norm_g3, w_in, trig)
        else:
            x2 = _out_proj(sb_o, ret_o, mem_o, proj, w_out, l, x2)
    return x2.reshape(b, s, d)
```

```python
import functools
import math

import jax
import jax.numpy as jnp
from jax import lax
from jax.experimental import pallas as pl
from jax.experimental.pallas import tpu as pltpu

D_MODEL = 1024
HEAD_DIM = 64
SB_HEADS = 6
RET_HEADS = 6
MEM_HEADS = 4
SB_WIDTH = SB_HEADS * HEAD_DIM
RET_WIDTH = RET_HEADS * HEAD_DIM
MEM_WIDTH = MEM_HEADS * HEAD_DIM
MIX_WIDTH = SB_WIDTH + RET_WIDTH + MEM_WIDTH
IN_WIDTH = 4 * SB_WIDTH + 4 * RET_WIDTH + 2 * MEM_WIDTH
ROPE_BASE = 10000.0
EPS = 1e-6
QK_SCALE = HEAD_DIM ** -0.5
SB_Q_SCALE = QK_SCALE * math.log2(math.e)

LANES = 128
PAIR = LANES // HEAD_DIM
SB_PAIRS = SB_HEADS // PAIR
RET_PAIRS = RET_HEADS // PAIR
MEM_PAIRS = MEM_HEADS // PAIR

COL_SB_Q = 0
COL_SB_K = COL_SB_Q + SB_PAIRS
COL_SB_V = COL_SB_K + SB_PAIRS
COL_SB_G = COL_SB_V + SB_PAIRS
COL_RET_Q = COL_SB_G + SB_PAIRS
COL_RET_K = COL_RET_Q + RET_PAIRS
COL_RET_V = COL_RET_K + RET_PAIRS
COL_RET_G = COL_RET_V + RET_PAIRS
COL_MEM_Q = COL_RET_G + RET_PAIRS
COL_MEM_G = COL_MEM_Q + MEM_PAIRS

PROJ_ROWS = 512
OUT_ROWS = 1024
PROJ_COLS = 512
SB_BLOCK = 256
RET_CHUNK = 128
MEM_ROWS = 512
V7X_VMEM_BYTES = 64 * 1024 * 1024
VMEM_LIMIT = V7X_VMEM_BYTES * 3 // 4

F32 = jnp.float32
BF16 = jnp.bfloat16


def _dot(a, b):
    return jnp.dot(a, b, preferred_element_type=F32)


def _dot_nt(a, b):
    return lax.dot_general(a, b, (((1,), (1,)), ((), ())), preferred_element_type=F32)


def _dot_tn(a, b):
    return lax.dot_general(a, b, (((0,), (0,)), ((), ())), preferred_element_type=F32)


def _silu(g):
    return g / (1.0 + jnp.exp(-g))


def _head_masks():
    lane = lax.broadcasted_iota(jnp.int32, (1, LANES), 1)
    first = lane < HEAD_DIM
    return first, jnp.logical_not(first)


def _pair_mean_square(t, first):
    sq = t * t
    s_a = jnp.sum(jnp.where(first, sq, 0.0), axis=-1, keepdims=True)
    s_b = jnp.sum(jnp.where(first, 0.0, sq), axis=-1, keepdims=True)
    return jnp.where(first, s_a, s_b) * (1.0 / HEAD_DIM)


def _rope_kernel(pos_ref, invf_ref, trig_ref):
    ang = pos_ref[...].astype(F32) * invf_ref[...]
    lane = lax.broadcasted_iota(jnp.int32, (1, LANES), 1)
    phase = jnp.where((lane % HEAD_DIM) < HEAD_DIM // 2, 0.0, math.pi / 2)
    trig_ref[...] = jnp.cos(ang - phase)


def _rope_tables(positions):
    b, s = positions.shape
    half = HEAD_DIM // 2
    inv_freq = ROPE_BASE ** (-jnp.arange(half, dtype=F32) / half)
    invf = jnp.tile(inv_freq, LANES // half)[None, :]
    pos = positions.reshape(b * s, 1)
    rows = s
    return pl.pallas_call(
        _rope_kernel,
        grid=(b * s // rows,),
        in_specs=[pl.BlockSpec((rows, 1), lambda i: (i, 0)),
                  pl.BlockSpec((1, LANES), lambda i: (0, 0))],
        out_specs=pl.BlockSpec((rows, LANES), lambda i: (i, 0)),
        out_shape=jax.ShapeDtypeStruct((b * s, LANES), F32),
        name="rope_tables",
    )(pos, invf)


def _half_lanes(t):
    lane = lax.broadcasted_iota(jnp.int32, (1, LANES), 1)
    half = HEAD_DIM // 2
    return (lane % HEAD_DIM) < half, pltpu.roll(t, LANES - half, 1), pltpu.roll(t, half, 1)


def _swap_halves(t):
    lower, from_upper, from_lower = _half_lanes(t)
    return jnp.where(lower, from_upper, from_lower)


def _norm_project(x, g_ref, w_ref, trig_ref, o_ref):
    ms = jnp.mean(x * x, axis=-1, keepdims=True)
    h = (x * lax.rsqrt(ms + EPS) * g_ref[0]).astype(BF16)
    trig = trig_ref[...]
    lower, from_upper, from_lower = _half_lanes(trig)
    cos = jnp.where(lower, trig, from_lower)
    sin = jnp.where(lower, -from_upper, trig)
    blocks_per_dot = PROJ_COLS // LANES
    for j in range(IN_WIDTH // PROJ_COLS):
        y = _dot(h, w_ref[0, :, j * PROJ_COLS:(j + 1) * PROJ_COLS].astype(BF16))
        for b in range(blocks_per_dot):
            blk = j * blocks_per_dot + b
            piece = y[:, b * LANES:(b + 1) * LANES]
            if blk < COL_SB_K:
                piece = piece * SB_Q_SCALE
            elif COL_RET_Q <= blk < COL_RET_V:
                piece = piece * cos + _swap_halves(piece) * sin
                if blk >= COL_RET_K:
                    piece = piece * QK_SCALE
            o_ref[:, blk * LANES:(blk + 1) * LANES] = piece.astype(BF16)


def _in_proj_kernel(x_ref, g_ref, w_ref, trig_ref, o_ref):
    _norm_project(x_ref[...], g_ref, w_ref, trig_ref, o_ref)


def _mix(sb_ref, ret_ref, mem_ref, sb_g_ref, w_ref, x_ref):
    sb = (sb_ref[...].astype(F32) * _silu(sb_g_ref[...].astype(F32))).astype(BF16)
    mixed = jnp.concatenate([sb, ret_ref[...], mem_ref[...]], axis=1)
    return x_ref[...] + _dot(mixed, w_ref[0].astype(BF16))


def _out_proj_kernel(sb_ref, ret_ref, mem_ref, sb_g_ref, w_ref, x_ref, o_ref):
    o_ref[...] = _mix(sb_ref, ret_ref, mem_ref, sb_g_ref, w_ref, x_ref)


def _out_in_proj_kernel(sb_ref, ret_ref, mem_ref, sb_g_ref, w_out_ref, x_ref, g_ref, w_in_ref,
                        trig_ref, x_out_ref, proj_ref):
    x_new = _mix(sb_ref, ret_ref, mem_ref, sb_g_ref, w_out_ref, x_ref)
    x_out_ref[...] = x_new
    _norm_project(x_new, g_ref, w_in_ref, trig_ref, proj_ref)


def _row_tiles(width, rows=PROJ_ROWS, col=0):
    return pl.BlockSpec((rows, width), lambda i: (i, col))


def _mixer_tiles(rows):
    assert COL_SB_G * LANES % SB_WIDTH == 0
    return [_row_tiles(SB_WIDTH, rows), _row_tiles(RET_WIDTH, rows), _row_tiles(MEM_WIDTH, rows),
            _row_tiles(SB_WIDTH, rows, COL_SB_G * LANES // SB_WIDTH)]


def _layer_param(shape, layer):
    return pl.BlockSpec((1,) + shape, lambda i: (layer, 0, 0), pipeline_mode=pl.Buffered(1))


def _in_proj(x2, g, w, layer, trig):
    m = x2.shape[0]
    return pl.pallas_call(
        _in_proj_kernel,
        grid=(m // PROJ_ROWS,),
        in_specs=[_row_tiles(D_MODEL), _layer_param((1, D_MODEL), layer),
                  _layer_param((D_MODEL, IN_WIDTH), layer), _row_tiles(LANES)],
        out_specs=_row_tiles(IN_WIDTH),
        out_shape=jax.ShapeDtypeStruct((m, IN_WIDTH), BF16),
        compiler_params=pltpu.CompilerParams(vmem_limit_bytes=VMEM_LIMIT),
        name="in_proj",
    )(x2, g, w, trig)


def _out_proj(sb_o, ret_o, mem_o, proj, w, layer, x2):
    m = x2.shape[0]
    stream = pl.BlockSpec((OUT_ROWS, D_MODEL), lambda i: (i, 0), pipeline_mode=pl.Buffered(3))
    in_specs = _mixer_tiles(OUT_ROWS) + [
        pl.BlockSpec((1, MIX_WIDTH, D_MODEL), lambda i: (layer, 0, 0)), stream]

    def whole(*refs):
        pltpu.emit_pipeline(_out_proj_kernel, grid=(m // OUT_ROWS,), in_specs=in_specs,
                            out_specs=[_row_tiles(D_MODEL, OUT_ROWS)])(*refs)

    any_space = pl.BlockSpec(memory_space=pl.ANY)
    return pl.pallas_call(
        whole,
        in_specs=[any_space] * 6,
        out_specs=any_space,
        out_shape=jax.ShapeDtypeStruct((m, D_MODEL), F32),
        compiler_params=pltpu.CompilerParams(vmem_limit_bytes=VMEM_LIMIT),
        name="out_proj",
    )(sb_o, ret_o, mem_o, proj, w, x2)


def _out_in_proj(sb_o, ret_o, mem_o, proj, w_out, layer, x2, g, w_in, trig):
    m = x2.shape[0]
    return pl.pallas_call(
        _out_in_proj_kernel,
        grid=(m // PROJ_ROWS,),
        in_specs=_mixer_tiles(PROJ_ROWS) + [
                  _layer_param((MIX_WIDTH, D_MODEL), layer), _row_tiles(D_MODEL),
                  _layer_param((1, D_MODEL), layer + 1), _layer_param((D_MODEL, IN_WIDTH), layer + 1),
                  _row_tiles(LANES)],
        out_specs=[_row_tiles(D_MODEL), _row_tiles(IN_WIDTH)],
        out_shape=[jax.ShapeDtypeStruct((m, D_MODEL), F32),
                   jax.ShapeDtypeStruct((m, IN_WIDTH), BF16)],
        compiler_params=pltpu.CompilerParams(vmem_limit_bytes=VMEM_LIMIT),
        name="out_in_proj",
    )(sb_o, ret_o, mem_o, proj, w_out, x2, g, w_in, trig)


SB_LAG = 2
SB_STEP = 8
SB_MASKED = -1e30
SB_DEAD = 160.0
SB_IDLE_CARRY = 1e30


def _sb_schedule(nq):
    zero_slot, junk_slot = nq, nq + 1
    idle = (0, 0, junk_slot, junk_slot, 1, 1)
    items = [idle] * SB_STEP
    waves = []
    for w in range(nq):
        start = len(items)
        for qb in range(nq - 1, w - 1, -1):
            src = zero_slot if w == 0 else qb
            items.append((qb, qb - w, src, qb, 0 if qb > w else 1, 0))
        items += [idle] * (-(len(items) - start) % SB_STEP)
        waves.append((start, (len(items) - start) // SB_STEP))
    return jnp.asarray(items, jnp.int32).T, jnp.asarray(waves, jnp.int32).T, waves[:2]


def _sb_kernel(tab_ref, wave_ref, q_ref, k_ref, v_ref, o_ref,
               tri_ref, bias_ref, z_ref, sp_ref, w_ref, r_ref, m_ref, acc_ref, *, n_waves,
               static_waves):
    t = SB_BLOCK
    first, second = _head_masks()
    n_carry = r_ref.shape[0]

    @pl.when(jnp.logical_and(pl.program_id(0) == 0, pl.program_id(1) == 0))
    def _constants():
        row = lax.broadcasted_iota(jnp.int32, (t, t), 0)
        col = lax.broadcasted_iota(jnp.int32, (t, t), 1)
        tri_ref[...] = (row >= col).astype(BF16)
        bias_ref[...] = jnp.where(col < row, 0.0, SB_MASKED)
        r_ref[n_carry - 2] = jnp.zeros(r_ref.shape[1:], F32)

    z_ref[SB_STEP - 1] = jnp.full(z_ref.shape[1:], SB_MASKED, F32)
    sp_ref[SB_STEP - 1] = jnp.zeros(sp_ref.shape[1:], BF16)
    for slot in range(SB_STEP - SB_LAG, SB_STEP):
        w_ref[slot] = jnp.zeros(w_ref.shape[1:], BF16)
    r_ref[n_carry - 1] = jnp.full(r_ref.shape[1:], SB_IDLE_CARRY, F32)
    m_ref[1] = jnp.zeros(m_ref.shape[1:], F32)
    acc_ref[...] = jnp.zeros_like(acc_ref)

    def rows(blk):
        return pl.ds(pl.multiple_of(blk * t, t), t)

    def softplus2(z):
        return jnp.maximum(z, 0.0) + jnp.log2(1.0 + jnp.exp2(-jnp.abs(z)))

    half = t // 2

    def lower_rows(a):
        return [a[r0:r0 + half] for r0 in (half, t + half)]

    def zero_upper_rows(lower):
        zeros = jnp.zeros((half, half), lower[0].dtype)
        return jnp.concatenate([zeros, lower[0], zeros, lower[1]], axis=0)

    def scores(j, slot, diagonal):
        qb, kb = tab_ref[0, j], tab_ref[1, j]
        q = q_ref[rows(qb), :]
        q2 = jnp.concatenate([jnp.where(first, q, 0), jnp.where(second, q, 0)], axis=0)
        z = _dot_nt(q2, k_ref[rows(kb), :])
        if diagonal:
            bias = bias_ref[...]
            z = z + jnp.concatenate([bias, bias], axis=0)
            sp = jnp.concatenate(
                [softplus2(z[:, :half]),
                 zero_upper_rows([softplus2(p) for p in lower_rows(z[:, half:])])], axis=1)
        else:
            sp = softplus2(z)
        z_ref[slot] = z
        sp_ref[slot] = sp.astype(BF16)

    def weights(j, slot, track_minimum, diagonal):
        cum = _dot(sp_ref[slot], tri_ref[...])
        r = r_ref[tab_ref[2, j]]
        z = z_ref[slot]
        if diagonal:
            pieces = zip(lower_rows(z[:, half:]), lower_rows(cum[:, half:]), lower_rows(r))
            halves = [jnp.exp2(z[:, :half] - cum[:, :half] - r),
                      zero_upper_rows([jnp.exp2(a - b - c) for a, b, c in pieces])]
        else:
            halves = [jnp.exp2(z[:, c:c + LANES] - cum[:, c:c + LANES] - r) for c in (0, LANES)]
        w_ref[slot] = jnp.concatenate(halves, axis=1).astype(BF16)
        r_new = r + cum[:, 0:1]
        r_ref[tab_ref[3, j]] = r_new
        if track_minimum:
            m_slot = tab_ref[4, j]
            m_ref[m_slot] = jnp.minimum(m_ref[m_slot], r_new)

    def values(j, slot):
        qb, kb = tab_ref[0, j], tab_ref[1, j]
        pv = _dot(w_ref[slot], v_ref[rows(kb), :])
        acc_ref[rows(qb), :] += jnp.where(first, pv[:t], pv[t:])

    def run_wave(start, n_steps, diagonal):
        def step(n, _):
            for i in range(SB_STEP):
                c = start + SB_STEP * n + i
                values(c - 2, (i - 2) % SB_STEP)
                weights(c - 1, (i - 1) % SB_STEP, not diagonal, diagonal)
                scores(c, i, diagonal)
            return 0

        if isinstance(n_steps, int):
            for n in range(n_steps):
                step(n, 0)
        else:
            lax.fori_loop(0, n_steps, step, 0)
        return start + SB_STEP * n_steps

    def later_wave(state):
        w = state[0]
        m_ref[0] = jnp.full(m_ref.shape[1:], jnp.inf, F32)
        end = run_wave(wave_ref[0, w], wave_ref[1, w], False)
        return w + 1, end, jnp.min(m_ref[0])

    end = run_wave(*static_waves[0], True)
    state = (1, end, jnp.float32(0.0))
    if len(static_waves) > 1:
        m_ref[0] = jnp.full(m_ref.shape[1:], jnp.inf, F32)
        end = run_wave(*static_waves[1], False)
        state = (2, end, jnp.min(m_ref[0]))
    _, end, _ = lax.while_loop(lambda s: jnp.logical_and(s[0] < n_waves, s[2] < SB_DEAD),
                               later_wave, (jnp.int32(state[0]), jnp.int32(state[1]), state[2]))
    @pl.when(tab_ref[5, end - 2] == 0)
    def _():
        values(end - 2, SB_STEP - 2)

    @pl.when(tab_ref[5, end - 1] == 0)
    def _():
        weights(end - 1, SB_STEP - 1, False, False)
        values(end - 1, SB_STEP - 1)

    o_ref[...] = acc_ref[...].astype(BF16)


def _sb_attention(proj, b, s):
    nq = s // SB_BLOCK
    assert SB_BLOCK == 2 * LANES
    table, waves, static_waves = _sb_schedule(nq)
    t = SB_BLOCK
    seq = lambda col: pl.BlockSpec((s, LANES), lambda bi, p, tab, wav: (bi, col + p))
    return pl.pallas_call(
        functools.partial(_sb_kernel, n_waves=nq, static_waves=static_waves),
        grid_spec=pltpu.PrefetchScalarGridSpec(
            num_scalar_prefetch=2,
            grid=(b, SB_PAIRS),
            in_specs=[seq(COL_SB_Q), seq(COL_SB_K), seq(COL_SB_V)],
            out_specs=pl.BlockSpec((s, LANES), lambda bi, p, tab, wav: (bi, p)),
            scratch_shapes=[pltpu.VMEM((t, t), BF16),
                            pltpu.VMEM((t, t), F32),
                            pltpu.VMEM((SB_STEP, 2 * t, t), F32),
                            pltpu.VMEM((SB_STEP, 2 * t, t), BF16),
                            pltpu.VMEM((SB_STEP, 2 * t, t), BF16),
                            pltpu.VMEM((nq + 2, 2 * t, LANES), F32),
                            pltpu.VMEM((2, 2 * t, LANES), F32),
                            pltpu.VMEM((s, LANES), F32)]),
        out_shape=jax.ShapeDtypeStruct((b * s, SB_WIDTH), BF16),
        compiler_params=pltpu.CompilerParams(
            dimension_semantics=("arbitrary", "arbitrary"),
            vmem_limit_bytes=VMEM_LIMIT),
        name="sb_attention",
    )(table, waves, proj, proj, proj)


def _ret_kernel(q_ref, k_ref, v_ref, g_ref, ng_ref, lg_ref, o_ref, kv_ref, state_ref):
    c = RET_CHUNK
    first, second = _head_masks()
    nchunks = q_ref.shape[0] // c

    lg_lane = lg_ref[0]
    lg_a = lg_lane[:, 0:1]
    lg_b = lg_lane[:, HEAD_DIM:HEAD_DIM + 1]

    ri = lax.broadcasted_iota(jnp.int32, (c, c), 0)
    ci = lax.broadcasted_iota(jnp.int32, (c, c), 1)
    diff = (ri - ci).astype(F32)
    keep = ri >= ci
    decay_a = jnp.where(keep, jnp.exp(lg_a * jnp.maximum(diff, 0.0)), 0.0)
    decay_b = jnp.where(keep, jnp.exp(lg_b * jnp.maximum(diff, 0.0)), 0.0)
    idx = lax.broadcasted_iota(jnp.int32, (c, LANES), 0).astype(F32)
    q_decay = jnp.exp(lg_lane * (idx + 1.0))
    k_decay = jnp.exp(lg_lane * (c - 1.0 - idx))
    si = lax.broadcasted_iota(jnp.int32, (LANES, LANES), 0)
    sj = lax.broadcasted_iota(jnp.int32, (LANES, LANES), 1)
    same_head = (si < HEAD_DIM) == (sj < HEAD_DIM)
    state_decay = jnp.where(same_head, jnp.exp(lg_lane * float(c)), 0.0)
    norm_g = ng_ref[0]

    chunk_rows = [slice(n * c, (n + 1) * c) for n in range(nchunks)]

    for n, rows in enumerate(chunk_rows):
        kd = (k_ref[rows, :].astype(F32) * k_decay).astype(BF16)
        kv_ref[n] = jnp.where(same_head, _dot_tn(kd, v_ref[rows, :]), 0.0)

    state = jnp.zeros((LANES, LANES), F32)
    for n in range(nchunks):
        state_ref[n] = state.astype(BF16)
        state = state * state_decay + kv_ref[n]

    decay2 = jnp.concatenate([decay_a, decay_b], axis=1)
    for n, rows in enumerate(chunk_rows):
        qb = q_ref[rows, :]
        kb = k_ref[rows, :]
        v = v_ref[rows, :]
        k2 = jnp.concatenate([jnp.where(first, kb, 0), jnp.where(second, kb, 0)], axis=0)
        v2 = jnp.concatenate([jnp.where(first, v, 0), jnp.where(second, v, 0)], axis=0)
        scores = (_dot_nt(qb, k2) * decay2).astype(BF16)
        out = _dot(scores, v2) + _dot(qb, state_ref[n]) * q_decay
        inv = lax.rsqrt(_pair_mean_square(out, first) + EPS)
        y = out * inv * norm_g
        o_ref[rows, :] = (y * _silu(g_ref[rows, :].astype(F32))).astype(BF16)


def _retention(proj, ret_norm_g, layer, b, s):
    seq = lambda col: pl.BlockSpec((s, LANES), lambda bi, p: (bi, col + p))
    gamma = 1.0 - 2.0 ** (-5.0 - jnp.arange(RET_HEADS, dtype=F32))
    log_gamma = jnp.repeat(jnp.log(gamma), HEAD_DIM).reshape(RET_PAIRS, 1, LANES)
    return pl.pallas_call(
        _ret_kernel,
        grid=(b, RET_PAIRS),
        in_specs=[seq(COL_RET_Q), seq(COL_RET_K), seq(COL_RET_V), seq(COL_RET_G),
                  pl.BlockSpec((1, 1, LANES), lambda bi, p: (layer, 0, p)),
                  pl.BlockSpec((1, 1, LANES), lambda bi, p: (p, 0, 0))],
        out_specs=pl.BlockSpec((s, LANES), lambda bi, p: (bi, p)),
        out_shape=jax.ShapeDtypeStruct((b * s, RET_WIDTH), BF16),
        scratch_shapes=[pltpu.VMEM((s // RET_CHUNK, LANES, LANES), F32),
                        pltpu.VMEM((s // RET_CHUNK, LANES, LANES), BF16)],
        compiler_params=pltpu.CompilerParams(vmem_limit_bytes=VMEM_LIMIT),
        name="retention",
    )(proj, proj, proj, proj, ret_norm_g, log_gamma)


def _mem_kv_kernel(mem_ref, g_ref, w_ref, kg_ref, k_ref, v_ref):
    first, _ = _head_masks()
    x = mem_ref[...]
    ms = jnp.mean(x * x, axis=-1, keepdims=True)
    h = (x * lax.rsqrt(ms + EPS) * g_ref[0]).astype(BF16)
    kv = _dot(h, w_ref[0].astype(BF16))
    kg = kg_ref[0]
    for pair in range(MEM_PAIRS):
        cols = slice(pair * LANES, (pair + 1) * LANES)
        kp = kv[:, cols]
        kn = kp * lax.rsqrt(_pair_mean_square(kp, first) + EPS) * kg
        k_ref[0, :, cols] = kn.astype(BF16)
    v_ref[0] = kv[:, MEM_WIDTH:].astype(BF16)


def _mem_kv(mem2, mem_norm_g, w_mem_kv, k_norm_g):
    depth = w_mem_kv.shape[0]
    rows = mem2.shape[0]
    out = jax.ShapeDtypeStruct((depth, rows, MEM_WIDTH), BF16)
    return pl.pallas_call(
        _mem_kv_kernel,
        grid=(depth,),
        in_specs=[pl.BlockSpec((rows, D_MODEL), lambda l: (0, 0)),
                  pl.BlockSpec((1, 1, D_MODEL), lambda l: (l, 0, 0)),
                  pl.BlockSpec((1, D_MODEL, 2 * MEM_WIDTH), lambda l: (l, 0, 0)),
                  pl.BlockSpec((1, 1, LANES), lambda l: (l, 0, 0))],
        out_specs=[pl.BlockSpec((1, rows, MEM_WIDTH), lambda l: (l, 0, 0)),
                   pl.BlockSpec((1, rows, MEM_WIDTH), lambda l: (l, 0, 0))],
        out_shape=[out, out],
        compiler_params=pltpu.CompilerParams(vmem_limit_bytes=VMEM_LIMIT),
        name="mem_kv",
    )(mem2, mem_norm_g, w_mem_kv, k_norm_g)


def _mem_attn_kernel(q_ref, g_ref, k_ref, v_ref, qg_ref, o_ref, e_ref, den_ref):
    first, second = _head_masks()
    tile = MEM_ROWS
    n_tiles = q_ref.shape[0] // tile
    k = k_ref[0]
    v = v_ref[0]

    def probabilities(i, slot):
        rows = slice(i * tile, (i + 1) * tile)
        q = q_ref[rows, :].astype(F32)
        inv = lax.rsqrt(_pair_mean_square(q, first) + EPS) * QK_SCALE
        qn = (q * inv * qg_ref[0]).astype(BF16)
        q2 = jnp.concatenate([jnp.where(first, qn, 0), jnp.where(second, qn, 0)], axis=0)
        sc = _dot_nt(q2, k)
        e = jnp.exp(sc - jnp.max(sc, axis=-1, keepdims=True))
        den_ref[slot] = jnp.broadcast_to(jnp.sum(e, axis=-1, keepdims=True), (2 * tile, LANES))
        e_ref[slot] = e.astype(BF16)

    def output(i, slot):
        rows = slice(i * tile, (i + 1) * tile)
        pv = _dot(e_ref[slot], v) / den_ref[slot]
        out = jnp.where(first, pv[:tile], pv[tile:])
        o_ref[rows, :] = (out * _silu(g_ref[rows, :].astype(F32))).astype(BF16)

    for i in range(n_tiles + 1):
        if i >= 1:
            output(i - 1, (i - 1) % 2)
        if i < n_tiles:
            probabilities(i, i % 2)


def _mem_attention(proj, mk, mv, q_norm_g, layer, b, s, tokens):
    seq = lambda col: pl.BlockSpec((s, LANES), lambda bi, p: (bi, col + p))
    kv_spec = pl.BlockSpec((1, tokens, LANES), lambda bi, p: (layer, bi, p))
    return pl.pallas_call(
        _mem_attn_kernel,
        grid=(b, MEM_PAIRS),
        in_specs=[seq(COL_MEM_Q), seq(COL_MEM_G), kv_spec, kv_spec,
                  pl.BlockSpec((1, 1, LANES), lambda bi, p: (layer, 0, 0))],
        out_specs=pl.BlockSpec((s, LANES), lambda bi, p: (bi, p)),
        out_shape=jax.ShapeDtypeStruct((b * s, MEM_WIDTH), BF16),
        scratch_shapes=[pltpu.VMEM((2, 2 * MEM_ROWS, tokens), BF16),
                        pltpu.VMEM((2, 2 * MEM_ROWS, LANES), F32)],
        compiler_params=pltpu.CompilerParams(vmem_limit_bytes=VMEM_LIMIT),
        name="mem_attention",
    )(proj, proj, mk, mv, q_norm_g)


def kernel(x, mem, positions, norm_g, w_in, w_out, mem_norm_g, w_mem_kv, mem_q_norm_g,
           mem_k_norm_g, ret_norm_g):
    b, s, d = x.shape
    tokens = mem.shape[1]
    depth = w_in.shape[0]
    assert d == D_MODEL and w_in.shape[2] == IN_WIDTH
    assert s % SB_BLOCK == 0 and s % RET_CHUNK == 0 and s % MEM_ROWS == 0
    assert (b * s) % PROJ_ROWS == 0 and (b * s) % OUT_ROWS == 0

    norm_g3 = norm_g[:, None, :]
    ret_norm_g3 = ret_norm_g[:, None, :]
    q_norm_g3 = jnp.tile(mem_q_norm_g, (1, PAIR))[:, None, :]
    k_norm_g3 = jnp.tile(mem_k_norm_g, (1, PAIR))[:, None, :]

    trig = _rope_tables(positions)
    mk_all, mv_all = _mem_kv(mem.reshape(b * tokens, d), mem_norm_g[:, None, :], w_mem_kv, k_norm_g3)

    x2 = x.reshape(b * s, d)
    proj = _in_proj(x2, norm_g3, w_in, 0, trig)
    for l in range(depth):
        sb_o = _sb_attention(proj, b, s)
        ret_o = _retention(proj, ret_norm_g3, l, b, s)
        mem_o = _mem_attention(proj, mk_all, mv_all, q_norm_g3, l, b, s, tokens)
        if l + 1 < depth:
            x2, proj = _out_in_proj(sb_o, ret_o, mem_o, proj, w_out, l, x2, norm_g3, w_in, trig)
        else:
            x2 = _out_proj(sb_o, ret_o, mem_o, proj, w_out, l, x2)
    return x2.reshape(b, s, d)
```

```python
import functools
import math

import jax
import jax.numpy as jnp
from jax import lax
from jax.experimental import pallas as pl
from jax.experimental.pallas import tpu as pltpu

D_MODEL = 1024
HEAD_DIM = 64
SB_HEADS = 6
RET_HEADS = 6
MEM_HEADS = 4
SB_WIDTH = SB_HEADS * HEAD_DIM
RET_WIDTH = RET_HEADS * HEAD_DIM
MEM_WIDTH = MEM_HEADS * HEAD_DIM
MIX_WIDTH = SB_WIDTH + RET_WIDTH + MEM_WIDTH
IN_WIDTH = 4 * SB_WIDTH + 4 * RET_WIDTH + 2 * MEM_WIDTH
ROPE_BASE = 10000.0
EPS = 1e-6
QK_SCALE = HEAD_DIM ** -0.5
SB_Q_SCALE = QK_SCALE * math.log2(math.e)

LANES = 128
PAIR = LANES // HEAD_DIM
SB_PAIRS = SB_HEADS // PAIR
RET_PAIRS = RET_HEADS // PAIR
MEM_PAIRS = MEM_HEADS // PAIR

COL_SB_Q = 0
COL_SB_K = COL_SB_Q + SB_PAIRS
COL_SB_V = COL_SB_K + SB_PAIRS
COL_SB_G = COL_SB_V + SB_PAIRS
COL_RET_Q = COL_SB_G + SB_PAIRS
COL_RET_K = COL_RET_Q + RET_PAIRS
COL_RET_V = COL_RET_K + RET_PAIRS
COL_RET_G = COL_RET_V + RET_PAIRS
COL_MEM_Q = COL_RET_G + RET_PAIRS
COL_MEM_G = COL_MEM_Q + MEM_PAIRS

PROJ_ROWS = 512
OUT_ROWS = 1024
PROJ_COLS = 512
SB_BLOCK = 256
RET_CHUNK = 128
MEM_ROWS = 512
V7X_VMEM_BYTES = 64 * 1024 * 1024
VMEM_LIMIT = V7X_VMEM_BYTES * 3 // 4

F32 = jnp.float32
BF16 = jnp.bfloat16


def _dot(a, b):
    return jnp.dot(a, b, preferred_element_type=F32)


def _dot_nt(a, b):
    return lax.dot_general(a, b, (((1,), (1,)), ((), ())), preferred_element_type=F32)


def _dot_tn(a, b):
    return lax.dot_general(a, b, (((0,), (0,)), ((), ())), preferred_element_type=F32)


def _silu(g):
    return g / (1.0 + jnp.exp(-g))


def _head_masks():
    lane = lax.broadcasted_iota(jnp.int32, (1, LANES), 1)
    first = lane < HEAD_DIM
    return first, jnp.logical_not(first)


def _pair_mean_square(t, first):
    sq = t * t
    s_a = jnp.sum(jnp.where(first, sq, 0.0), axis=-1, keepdims=True)
    s_b = jnp.sum(jnp.where(first, 0.0, sq), axis=-1, keepdims=True)
    return jnp.where(first, s_a, s_b) * (1.0 / HEAD_DIM)


def _rope_kernel(pos_ref, invf_ref, trig_ref):
    ang = pos_ref[...].astype(F32) * invf_ref[...]
    lane = lax.broadcasted_iota(jnp.int32, (1, LANES), 1)
    phase = jnp.where((lane % HEAD_DIM) < HEAD_DIM // 2, 0.0, math.pi / 2)
    trig_ref[...] = jnp.cos(ang - phase)


def _rope_tables(positions):
    b, s = positions.shape
    half = HEAD_DIM // 2
    inv_freq = ROPE_BASE ** (-jnp.arange(half, dtype=F32) / half)
    invf = jnp.tile(inv_freq, LANES // half)[None, :]
    pos = positions.reshape(b * s, 1)
    rows = s
    return pl.pallas_call(
        _rope_kernel,
        grid=(b * s // rows,),
        in_specs=[pl.BlockSpec((rows, 1), lambda i: (i, 0)),
                  pl.BlockSpec((1, LANES), lambda i: (0, 0))],
        out_specs=pl.BlockSpec((rows, LANES), lambda i: (i, 0)),
        out_shape=jax.ShapeDtypeStruct((b * s, LANES), F32),
        name="rope_tables",
    )(pos, invf)


def _half_lanes(t):
    lane = lax.broadcasted_iota(jnp.int32, (1, LANES), 1)
    half = HEAD_DIM // 2
    return (lane % HEAD_DIM) < half, pltpu.roll(t, LANES - half, 1), pltpu.roll(t, half, 1)


def _swap_halves(t):
    lower, from_upper, from_lower = _half_lanes(t)
    return jnp.where(lower, from_upper, from_lower)


def _norm_project(x, g_ref, w_ref, trig_ref, o_ref):
    ms = jnp.mean(x * x, axis=-1, keepdims=True)
    h = (x * lax.rsqrt(ms + EPS) * g_ref[0]).astype(BF16)
    trig = trig_ref[...]
    lower, from_upper, from_lower = _half_lanes(trig)
    cos = jnp.where(lower, trig, from_lower)
    sin = jnp.where(lower, -from_upper, trig)
    blocks_per_dot = PROJ_COLS // LANES
    for j in range(IN_WIDTH // PROJ_COLS):
        y = _dot(h, w_ref[0, :, j * PROJ_COLS:(j + 1) * PROJ_COLS].astype(BF16))
        for b in range(blocks_per_dot):
            blk = j * blocks_per_dot + b
            piece = y[:, b * LANES:(b + 1) * LANES]
            if blk < COL_SB_K:
                piece = piece * SB_Q_SCALE
            elif COL_RET_Q <= blk < COL_RET_V:
                piece = piece * cos + _swap_halves(piece) * sin
                if blk >= COL_RET_K:
                    piece = piece * QK_SCALE
            o_ref[:, blk * LANES:(blk + 1) * LANES] = piece.astype(BF16)


def _in_proj_kernel(x_ref, g_ref, w_ref, trig_ref, o_ref):
    _norm_project(x_ref[...], g_ref, w_ref, trig_ref, o_ref)


def _mix(sb_ref, ret_ref, mem_ref, sb_g_ref, w_ref, x_ref):
    sb = (sb_ref[...].astype(F32) * _silu(sb_g_ref[...].astype(F32))).astype(BF16)
    mixed = jnp.concatenate([sb, ret_ref[...], mem_ref[...]], axis=1)
    return x_ref[...] + _dot(mixed, w_ref[0].astype(BF16))


def _out_proj_kernel(sb_ref, ret_ref, mem_ref, sb_g_ref, w_ref, x_ref, o_ref):
    o_ref[...] = _mix(sb_ref, ret_ref, mem_ref, sb_g_ref, w_ref, x_ref)


def _out_in_proj_kernel(sb_ref, ret_ref, mem_ref, sb_g_ref, w_out_ref, x_ref, g_ref, w_in_ref,
                        trig_ref, x_out_ref, proj_ref):
    x_new = _mix(sb_ref, ret_ref, mem_ref, sb_g_ref, w_out_ref, x_ref)
    x_out_ref[...] = x_new
    _norm_project(x_new, g_ref, w_in_ref, trig_ref, proj_ref)


def _row_tiles(width, rows=PROJ_ROWS, col=0):
    return pl.BlockSpec((rows, width), lambda i: (i, col))


def _mixer_tiles(rows):
    assert COL_SB_G * LANES % SB_WIDTH == 0
    return [_row_tiles(SB_WIDTH, rows), _row_tiles(RET_WIDTH, rows), _row_tiles(MEM_WIDTH, rows),
            _row_tiles(SB_WIDTH, rows, COL_SB_G * LANES // SB_WIDTH)]


def _layer_param(shape, layer):
    return pl.BlockSpec((1,) + shape, lambda i: (layer, 0, 0), pipeline_mode=pl.Buffered(1))


def _in_proj(x2, g, w, layer, trig):
    m = x2.shape[0]
    return pl.pallas_call(
        _in_proj_kernel,
        grid=(m // PROJ_ROWS,),
        in_specs=[_row_tiles(D_MODEL), _layer_param((1, D_MODEL), layer),
                  _layer_param((D_MODEL, IN_WIDTH), layer), _row_tiles(LANES)],
        out_specs=_row_tiles(IN_WIDTH),
        out_shape=jax.ShapeDtypeStruct((m, IN_WIDTH), BF16),
        compiler_params=pltpu.CompilerParams(vmem_limit_bytes=VMEM_LIMIT),
        name="in_proj",
    )(x2, g, w, trig)


def _out_proj(sb_o, ret_o, mem_o, proj, w, layer, x2):
    m = x2.shape[0]
    stream = pl.BlockSpec((OUT_ROWS, D_MODEL), lambda i: (i, 0), pipeline_mode=pl.Buffered(3))
    deeper = lambda spec: pl.BlockSpec(spec.block_shape, spec.index_map, pipeline_mode=pl.Buffered(3))
    in_specs = [deeper(spec) for spec in _mixer_tiles(OUT_ROWS)] + [
        pl.BlockSpec((1, MIX_WIDTH, D_MODEL), lambda i: (layer, 0, 0)), stream]

    def whole(*refs):
        pltpu.emit_pipeline(_out_proj_kernel, grid=(m // OUT_ROWS,), in_specs=in_specs,
                            out_specs=[_row_tiles(D_MODEL, OUT_ROWS)])(*refs)

    any_space = pl.BlockSpec(memory_space=pl.ANY)
    return pl.pallas_call(
        whole,
        in_specs=[any_space] * 6,
        out_specs=any_space,
        out_shape=jax.ShapeDtypeStruct((m, D_MODEL), F32),
        compiler_params=pltpu.CompilerParams(vmem_limit_bytes=VMEM_LIMIT),
        name="out_proj",
    )(sb_o, ret_o, mem_o, proj, w, x2)


def _out_in_proj(sb_o, ret_o, mem_o, proj, w_out, layer, x2, g, w_in, trig):
    m = x2.shape[0]
    return pl.pallas_call(
        _out_in_proj_kernel,
        grid=(m // PROJ_ROWS,),
        in_specs=_mixer_tiles(PROJ_ROWS) + [
                  _layer_param((MIX_WIDTH, D_MODEL), layer), _row_tiles(D_MODEL),
                  _layer_param((1, D_MODEL), layer + 1), _layer_param((D_MODEL, IN_WIDTH), layer + 1),
                  _row_tiles(LANES)],
        out_specs=[_row_tiles(D_MODEL), _row_tiles(IN_WIDTH)],
        out_shape=[jax.ShapeDtypeStruct((m, D_MODEL), F32),
                   jax.ShapeDtypeStruct((m, IN_WIDTH), BF16)],
        compiler_params=pltpu.CompilerParams(vmem_limit_bytes=VMEM_LIMIT),
        name="out_in_proj",
    )(sb_o, ret_o, mem_o, proj, w_out, x2, g, w_in, trig)


SB_LAG = 2
SB_STEP = 8
SB_MASKED = -1e30
SB_DEAD = 160.0
SB_IDLE_CARRY = 1e30


def _sb_schedule(nq):
    zero_slot, junk_slot = nq, nq + 1
    idle = (0, 0, junk_slot, junk_slot, 1, 1)
    items = [idle] * SB_STEP
    waves = []
    for w in range(nq):
        start = len(items)
        for qb in range(nq - 1, w - 1, -1):
            src = zero_slot if w == 0 else qb
            items.append((qb, qb - w, src, qb, 0 if qb > w else 1, 0))
        items += [idle] * (-(len(items) - start) % SB_STEP)
        waves.append((start, (len(items) - start) // SB_STEP))
    return jnp.asarray(items, jnp.int32).T, jnp.asarray(waves, jnp.int32).T, waves[:2]


def _sb_kernel(tab_ref, wave_ref, q_ref, k_ref, v_ref, o_ref,
               tri_ref, bias_ref, z_ref, sp_ref, w_ref, r_ref, m_ref, acc_ref, *, n_waves,
               static_waves):
    t = SB_BLOCK
    first, second = _head_masks()
    n_carry = r_ref.shape[0]

    @pl.when(jnp.logical_and(pl.program_id(0) == 0, pl.program_id(1) == 0))
    def _constants():
        row = lax.broadcasted_iota(jnp.int32, (t, t), 0)
        col = lax.broadcasted_iota(jnp.int32, (t, t), 1)
        tri_ref[...] = (row >= col).astype(BF16)
        bias_ref[...] = jnp.where(col < row, 0.0, SB_MASKED)
        r_ref[n_carry - 2] = jnp.zeros(r_ref.shape[1:], F32)

    z_ref[SB_STEP - 1] = jnp.full(z_ref.shape[1:], SB_MASKED, F32)
    sp_ref[SB_STEP - 1] = jnp.zeros(sp_ref.shape[1:], BF16)
    for slot in range(SB_STEP - SB_LAG, SB_STEP):
        w_ref[slot] = jnp.zeros(w_ref.shape[1:], BF16)
    r_ref[n_carry - 1] = jnp.full(r_ref.shape[1:], SB_IDLE_CARRY, F32)
    m_ref[1] = jnp.zeros(m_ref.shape[1:], F32)
    acc_ref[...] = jnp.zeros_like(acc_ref)

    def rows(blk):
        return pl.ds(pl.multiple_of(blk * t, t), t)

    def softplus2(z):
        return jnp.maximum(z, 0.0) + jnp.log2(1.0 + jnp.exp2(-jnp.abs(z)))

    half = t // 2

    def lower_rows(a):
        return [a[r0:r0 + half] for r0 in (half, t + half)]

    def zero_upper_rows(lower):
        zeros = jnp.zeros((half, half), lower[0].dtype)
        return jnp.concatenate([zeros, lower[0], zeros, lower[1]], axis=0)

    def scores(j, slot, diagonal):
        qb, kb = tab_ref[0, j], tab_ref[1, j]
        q = q_ref[rows(qb), :]
        q2 = jnp.concatenate([jnp.where(first, q, 0), jnp.where(second, q, 0)], axis=0)
        z = _dot_nt(q2, k_ref[rows(kb), :])
        if diagonal:
            bias = bias_ref[...]
            z = z + jnp.concatenate([bias, bias], axis=0)
            sp = jnp.concatenate(
                [softplus2(z[:, :half]),
                 zero_upper_rows([softplus2(p) for p in lower_rows(z[:, half:])])], axis=1)
        else:
            sp = softplus2(z)
        z_ref[slot] = z
        sp_ref[slot] = sp.astype(BF16)

    def weights(j, slot, track_minimum, diagonal):
        cum = _dot(sp_ref[slot], tri_ref[...])
        r = r_ref[tab_ref[2, j]]
        z = z_ref[slot]
        if diagonal:
            pieces = zip(lower_rows(z[:, half:]), lower_rows(cum[:, half:]), lower_rows(r))
            halves = [jnp.exp2(z[:, :half] - cum[:, :half] - r),
                      zero_upper_rows([jnp.exp2(a - b - c) for a, b, c in pieces])]
        else:
            halves = [jnp.exp2(z[:, c:c + LANES] - cum[:, c:c + LANES] - r) for c in (0, LANES)]
        w_ref[slot] = jnp.concatenate(halves, axis=1).astype(BF16)
        r_new = r + cum[:, 0:1]
        r_ref[tab_ref[3, j]] = r_new
        if track_minimum:
            m_slot = tab_ref[4, j]
            m_ref[m_slot] = jnp.minimum(m_ref[m_slot], r_new)

    def values(j, slot):
        qb, kb = tab_ref[0, j], tab_ref[1, j]
        pv = _dot(w_ref[slot], v_ref[rows(kb), :])
        acc_ref[rows(qb), :] += jnp.where(first, pv[:t], pv[t:])

    def run_wave(start, n_steps, diagonal):
        def step(n, _):
            for i in range(SB_STEP):
                c = start + SB_STEP * n + i
                values(c - 2, (i - 2) % SB_STEP)
                weights(c - 1, (i - 1) % SB_STEP, not diagonal, diagonal)
                scores(c, i, diagonal)
            return 0

        if isinstance(n_steps, int):
            for n in range(n_steps):
                step(n, 0)
        else:
            lax.fori_loop(0, n_steps, step, 0)
        return start + SB_STEP * n_steps

    def later_wave(state):
        w = state[0]
        m_ref[0] = jnp.full(m_ref.shape[1:], jnp.inf, F32)
        end = run_wave(wave_ref[0, w], wave_ref[1, w], False)
        return w + 1, end, jnp.min(m_ref[0])

    end = run_wave(*static_waves[0], True)
    state = (1, end, jnp.float32(0.0))
    if len(static_waves) > 1:
        m_ref[0] = jnp.full(m_ref.shape[1:], jnp.inf, F32)
        end = run_wave(*static_waves[1], False)
        state = (2, end, jnp.min(m_ref[0]))
    _, end, _ = lax.while_loop(lambda s: jnp.logical_and(s[0] < n_waves, s[2] < SB_DEAD),
                               later_wave, (jnp.int32(state[0]), jnp.int32(state[1]), state[2]))
    @pl.when(tab_ref[5, end - 2] == 0)
    def _():
        values(end - 2, SB_STEP - 2)

    @pl.when(tab_ref[5, end - 1] == 0)
    def _():
        weights(end - 1, SB_STEP - 1, False, False)
        values(end - 1, SB_STEP - 1)

    o_ref[...] = acc_ref[...].astype(BF16)


def _sb_attention(proj, b, s):
    nq = s // SB_BLOCK
    assert SB_BLOCK == 2 * LANES
    table, waves, static_waves = _sb_schedule(nq)
    t = SB_BLOCK
    seq = lambda col: pl.BlockSpec((s, LANES), lambda bi, p, tab, wav: (bi, col + p))
    return pl.pallas_call(
        functools.partial(_sb_kernel, n_waves=nq, static_waves=static_waves),
        grid_spec=pltpu.PrefetchScalarGridSpec(
            num_scalar_prefetch=2,
            grid=(b, SB_PAIRS),
            in_specs=[seq(COL_SB_Q), seq(COL_SB_K), seq(COL_SB_V)],
            out_specs=pl.BlockSpec((s, LANES), lambda bi, p, tab, wav: (bi, p)),
            scratch_shapes=[pltpu.VMEM((t, t), BF16),
                            pltpu.VMEM((t, t), F32),
                            pltpu.VMEM((SB_STEP, 2 * t, t), F32),
                            pltpu.VMEM((SB_STEP, 2 * t, t), BF16),
                            pltpu.VMEM((SB_STEP, 2 * t, t), BF16),
                            pltpu.VMEM((nq + 2, 2 * t, LANES), F32),
                            pltpu.VMEM((2, 2 * t, LANES), F32),
                            pltpu.VMEM((s, LANES), F32)]),
        out_shape=jax.ShapeDtypeStruct((b * s, SB_WIDTH), BF16),
        compiler_params=pltpu.CompilerParams(
            dimension_semantics=("arbitrary", "arbitrary"),
            vmem_limit_bytes=VMEM_LIMIT),
        name="sb_attention",
    )(table, waves, proj, proj, proj)


def _ret_kernel(q_ref, k_ref, v_ref, g_ref, ng_ref, lg_ref, o_ref, kv_ref, state_ref):
    c = RET_CHUNK
    first, second = _head_masks()
    nchunks = q_ref.shape[0] // c

    lg_lane = lg_ref[0]
    lg_a = lg_lane[:, 0:1]
    lg_b = lg_lane[:, HEAD_DIM:HEAD_DIM + 1]

    ri = lax.broadcasted_iota(jnp.int32, (c, c), 0)
    ci = lax.broadcasted_iota(jnp.int32, (c, c), 1)
    diff = (ri - ci).astype(F32)
    keep = ri >= ci
    decay_a = jnp.where(keep, jnp.exp(lg_a * jnp.maximum(diff, 0.0)), 0.0)
    decay_b = jnp.where(keep, jnp.exp(lg_b * jnp.maximum(diff, 0.0)), 0.0)
    idx = lax.broadcasted_iota(jnp.int32, (c, LANES), 0).astype(F32)
    q_decay = jnp.exp(lg_lane * (idx + 1.0))
    k_decay = jnp.exp(lg_lane * (c - 1.0 - idx))
    si = lax.broadcasted_iota(jnp.int32, (LANES, LANES), 0)
    sj = lax.broadcasted_iota(jnp.int32, (LANES, LANES), 1)
    same_head = (si < HEAD_DIM) == (sj < HEAD_DIM)
    state_decay = jnp.where(same_head, jnp.exp(lg_lane * float(c)), 0.0)
    norm_g = ng_ref[0]

    chunk_rows = [slice(n * c, (n + 1) * c) for n in range(nchunks)]

    for n, rows in enumerate(chunk_rows):
        kd = (k_ref[rows, :].astype(F32) * k_decay).astype(BF16)
        kv_ref[n] = jnp.where(same_head, _dot_tn(kd, v_ref[rows, :]), 0.0)

    state = jnp.zeros((LANES, LANES), F32)
    for n in range(nchunks):
        state_ref[n] = state.astype(BF16)
        state = state * state_decay + kv_ref[n]

    decay2 = jnp.concatenate([decay_a, decay_b], axis=1)
    for n, rows in enumerate(chunk_rows):
        qb = q_ref[rows, :]
        kb = k_ref[rows, :]
        v = v_ref[rows, :]
        k2 = jnp.concatenate([jnp.where(first, kb, 0), jnp.where(second, kb, 0)], axis=0)
        v2 = jnp.concatenate([jnp.where(first, v, 0), jnp.where(second, v, 0)], axis=0)
        scores = (_dot_nt(qb, k2) * decay2).astype(BF16)
        out = _dot(scores, v2) + _dot(qb, state_ref[n]) * q_decay
        inv = lax.rsqrt(_pair_mean_square(out, first) + EPS)
        y = out * inv * norm_g
        o_ref[rows, :] = (y * _silu(g_ref[rows, :].astype(F32))).astype(BF16)


def _retention(proj, ret_norm_g, layer, b, s):
    seq = lambda col: pl.BlockSpec((s, LANES), lambda bi, p: (bi, col + p))
    gamma = 1.0 - 2.0 ** (-5.0 - jnp.arange(RET_HEADS, dtype=F32))
    log_gamma = jnp.repeat(jnp.log(gamma), HEAD_DIM).reshape(RET_PAIRS, 1, LANES)
    return pl.pallas_call(
        _ret_kernel,
        grid=(b, RET_PAIRS),
        in_specs=[seq(COL_RET_Q), seq(COL_RET_K), seq(COL_RET_V), seq(COL_RET_G),
                  pl.BlockSpec((1, 1, LANES), lambda bi, p: (layer, 0, p)),
                  pl.BlockSpec((1, 1, LANES), lambda bi, p: (p, 0, 0))],
        out_specs=pl.BlockSpec((s, LANES), lambda bi, p: (bi, p)),
        out_shape=jax.ShapeDtypeStruct((b * s, RET_WIDTH), BF16),
        scratch_shapes=[pltpu.VMEM((s // RET_CHUNK, LANES, LANES), F32),
                        pltpu.VMEM((s // RET_CHUNK, LANES, LANES), BF16)],
        compiler_params=pltpu.CompilerParams(vmem_limit_bytes=VMEM_LIMIT),
        name="retention",
    )(proj, proj, proj, proj, ret_norm_g, log_gamma)


def _mem_kv_kernel(mem_ref, g_ref, w_ref, kg_ref, k_ref, v_ref):
    first, _ = _head_masks()
    x = mem_ref[...]
    ms = jnp.mean(x * x, axis=-1, keepdims=True)
    h = (x * lax.rsqrt(ms + EPS) * g_ref[0]).astype(BF16)
    kv = _dot(h, w_ref[0].astype(BF16))
    kg = kg_ref[0]
    for pair in range(MEM_PAIRS):
        cols = slice(pair * LANES, (pair + 1) * LANES)
        kp = kv[:, cols]
        kn = kp * lax.rsqrt(_pair_mean_square(kp, first) + EPS) * kg
        k_ref[0, :, cols] = kn.astype(BF16)
    v_ref[0] = kv[:, MEM_WIDTH:].astype(BF16)


def _mem_kv(mem2, mem_norm_g, w_mem_kv, k_norm_g):
    depth = w_mem_kv.shape[0]
    rows = mem2.shape[0]
    out = jax.ShapeDtypeStruct((depth, rows, MEM_WIDTH), BF16)
    return pl.pallas_call(
        _mem_kv_kernel,
        grid=(depth,),
        in_specs=[pl.BlockSpec((rows, D_MODEL), lambda l: (0, 0)),
                  pl.BlockSpec((1, 1, D_MODEL), lambda l: (l, 0, 0)),
                  pl.BlockSpec((1, D_MODEL, 2 * MEM_WIDTH), lambda l: (l, 0, 0)),
                  pl.BlockSpec((1, 1, LANES), lambda l: (l, 0, 0))],
        out_specs=[pl.BlockSpec((1, rows, MEM_WIDTH), lambda l: (l, 0, 0)),
                   pl.BlockSpec((1, rows, MEM_WIDTH), lambda l: (l, 0, 0))],
        out_shape=[out, out],
        compiler_params=pltpu.CompilerParams(vmem_limit_bytes=VMEM_LIMIT),
        name="mem_kv",
    )(mem2, mem_norm_g, w_mem_kv, k_norm_g)


def _mem_attn_kernel(q_ref, g_ref, k_ref, v_ref, qg_ref, o_ref, e_ref, den_ref):
    first, second = _head_masks()
    tile = MEM_ROWS
    n_tiles = q_ref.shape[0] // tile
    k = k_ref[0]
    v = v_ref[0]

    def probabilities(i, slot):
        rows = slice(i * tile, (i + 1) * tile)
        q = q_ref[rows, :].astype(F32)
        inv = lax.rsqrt(_pair_mean_square(q, first) + EPS) * QK_SCALE
        qn = (q * inv * qg_ref[0]).astype(BF16)
        q2 = jnp.concatenate([jnp.where(first, qn, 0), jnp.where(second, qn, 0)], axis=0)
        sc = _dot_nt(q2, k)
        e = jnp.exp(sc - jnp.max(sc, axis=-1, keepdims=True))
        den_ref[slot] = jnp.broadcast_to(jnp.sum(e, axis=-1, keepdims=True), (2 * tile, LANES))
        e_ref[slot] = e.astype(BF16)

    def output(i, slot):
        rows = slice(i * tile, (i + 1) * tile)
        pv = _dot(e_ref[slot], v) / den_ref[slot]
        out = jnp.where(first, pv[:tile], pv[tile:])
        o_ref[rows, :] = (out * _silu(g_ref[rows, :].astype(F32))).astype(BF16)

    for i in range(n_tiles + 1):
        if i >= 1:
            output(i - 1, (i - 1) % 2)
        if i < n_tiles:
            probabilities(i, i % 2)


def _mem_attention(proj, mk, mv, q_norm_g, layer, b, s, tokens):
    seq = lambda col: pl.BlockSpec((s, LANES), lambda bi, p: (bi, col + p))
    kv_spec = pl.BlockSpec((1, tokens, LANES), lambda bi, p: (layer, bi, p))
    return pl.pallas_call(
        _mem_attn_kernel,
        grid=(b, MEM_PAIRS),
        in_specs=[seq(COL_MEM_Q), seq(COL_MEM_G), kv_spec, kv_spec,
                  pl.BlockSpec((1, 1, LANES), lambda bi, p: (layer, 0, 0))],
        out_specs=pl.BlockSpec((s, LANES), lambda bi, p: (bi, p)),
        out_shape=jax.ShapeDtypeStruct((b * s, MEM_WIDTH), BF16),
        scratch_shapes=[pltpu.VMEM((2, 2 * MEM_ROWS, tokens), BF16),
                        pltpu.VMEM((2, 2 * MEM_ROWS, LANES), F32)],
        compiler_params=pltpu.CompilerParams(vmem_limit_bytes=VMEM_LIMIT),
        name="mem_attention",
    )(proj, proj, mk, mv, q_norm_g)


def kernel(x, mem, positions, norm_g, w_in, w_out, mem_norm_g, w_mem_kv, mem_q_norm_g,
           mem_k_norm_g, ret_norm_g):
    b, s, d = x.shape
    tokens = mem.shape[1]
    depth = w_in.shape[0]
    assert d == D_MODEL and w_in.shape[2] == IN_WIDTH
    assert s % SB_BLOCK == 0 and s % RET_CHUNK == 0 and s % MEM_ROWS == 0
    assert (b * s) % PROJ_ROWS == 0 and (b * s) % OUT_ROWS == 0

    norm_g3 = norm_g[:, None, :]
    ret_norm_g3 = ret_norm_g[:, None, :]
    q_norm_g3 = jnp.tile(mem_q_norm_g, (1, PAIR))[:, None, :]
    k_norm_g3 = jnp.tile(mem_k_norm_g, (1, PAIR))[:, None, :]

    trig = _rope_tables(positions)
    mk_all, mv_all = _mem_kv(mem.reshape(b * tokens, d), mem_norm_g[:, None, :], w_mem_kv, k_norm_g3)

    x2 = x.reshape(b * s, d)
    proj = _in_proj(x2, norm_g3, w_in, 0, trig)
    for l in range(depth):
        sb_o = _sb_attention(proj, b, s)
        ret_o = _retention(proj, ret_norm_g3, l, b, s)
        mem_o = _mem_attention(proj, mk_all, mv_all, q_norm_g3, l, b, s, tokens)
        if l + 1 < depth:
            x2, proj = _out_in_proj(sb_o, ret_o, mem_o, proj, w_out, l, x2, norm_g3, w_in, trig)
        else:
            x2 = _out_proj(sb_o, ret_o, mem_o, proj, w_out, l, x2)
    return x2.reshape(b, s, d)
```
